```python
import jax, jax.numpy as jnp
from jax import lax
import numpy as np

D_MODEL = 2048
BATCH = 8
SEQ = 2048
DEPTH = 4
DEC_BATCH = 2
DEC_SEQ = 4096
PAST_LEN = 128

N_EVEN = (DEPTH + 1) // 2
N_ODD = DEPTH // 2
D_MIX = D_MODEL
D_HALF = D_MIX // 2
POOL_WINDOWS = (2, 4, 8, 16)
N_POOL = len(POOL_WINDOWS)
POOL_GROUP = D_HALF // N_POOL
SGU_CHUNK = 128
SGU_HEAD = 128
SGU_GROUPS = D_HALF // SGU_HEAD
RET_HEADS = 4
RET_DK = D_HALF // RET_HEADS
RET_DV = D_HALF // RET_HEADS
RET_CHUNK = 128
ROPE_BASE = 10000.0
DN_HEADS = 8
DN_DK = D_HALF // DN_HEADS
DN_DV = D_HALF // DN_HEADS
DN_CONV = 4
DN_CHUNK = 64
D_FF = 5632
N_MOD = 9
EPS = 1e-6

P_EVEN = D_HALF + 2 * D_HALF
P_RET = 4 * D_HALF
P_DN = 4 * D_HALF + 4 * DN_HEADS
P_ODD = P_RET + P_DN

kernel_name = "hybrid_pool_sgu_retention_deltanet_encoder"

F32 = jnp.float32


def _rmsnorm(x, gain):
    xf = x.astype(F32)
    y = xf * lax.rsqrt(jnp.mean(xf * xf, axis=-1, keepdims=True) + EPS)
    return (y * gain.astype(F32)).astype(x.dtype)


def _modulate(x, gain, shift, scale):
    return _rmsnorm(x, gain) * (1 + scale) + shift


def _l2norm(x):
    xf = x.astype(F32)
    return xf * lax.rsqrt(jnp.sum(xf * xf, axis=-1, keepdims=True) + EPS)


def _swiglu(h, w_in, w_out):
    g, u = jnp.split(h @ w_in, 2, axis=-1)
    return (jax.nn.silu(g) * u) @ w_out


def _pool_mixer(xa, pool_w, pool_scale):
    b, L, _ = xa.shape
    xf = xa.astype(F32)
    cs = jnp.concatenate([jnp.zeros((b, 1, D_HALF), F32), jnp.cumsum(xf, axis=1)], axis=1)
    t = jnp.arange(L)
    outs = []
    for gi, w in enumerate(POOL_WINDOWS):
        lo = jnp.clip(t - w // 2, 0, L)
        hi = jnp.clip(t + w - w // 2, 0, L)
        sl = slice(gi * POOL_GROUP, (gi + 1) * POOL_GROUP)
        csg = cs[:, :, sl]
        win_sum = jnp.take(csg, hi, axis=1) - jnp.take(csg, lo, axis=1)
        cnt = (hi - lo).astype(F32)[None, :, None]
        outs.append(win_sum / cnt - xf[:, :, sl])
    pooled = jnp.stack(outs, axis=2).astype(xa.dtype)
    y = jnp.einsum('blgc,gcd->blgd', pooled, pool_w).reshape(b, L, D_HALF)
    return y * pool_scale


def _sgu_mixer(uv, norm_g, sgu_w, sgu_b):
    b, L, _ = uv.shape
    u, v = jnp.split(jax.nn.gelu(uv), 2, axis=-1)
    v = _rmsnorm(v, norm_g)
    n = L // SGU_CHUNK
    vc = v.reshape(b, n, SGU_CHUNK, SGU_GROUPS, SGU_HEAD)
    mixed = jnp.einsum('gts,bnsgc->bntgc', sgu_w, vc) + sgu_b.T[:, :, None]
    return u * mixed.reshape(b, L, D_HALF)


def _rope(x, pos):
    d = x.shape[-1]
    inv = 1.0 / (ROPE_BASE ** jnp.linspace(0.0, 1.0, d // 2, dtype=F32))
    ang = pos.astype(F32)[:, None] * inv[None, :]
    cos = jnp.cos(ang)[None, :, None, :]
    sin = jnp.sin(ang)[None, :, None, :]
    x1, x2 = jnp.split(x.astype(F32), 2, axis=-1)
    return jnp.concatenate([x1 * cos - x2 * sin, x1 * sin + x2 * cos], axis=-1).astype(x.dtype)


def _retention_scan(q, k, v, log_g):
    b, L, h, dk = q.shape
    dv = v.shape[-1]
    n = L // RET_CHUNK

    def chunks(a):
        return jnp.moveaxis(a.astype(F32).reshape(b, n, RET_CHUNK, h, a.shape[-1]), 1, 0)

    i = jnp.arange(RET_CHUNK, dtype=F32)
    rel = i[:, None] - i[None, :]
    dmat = jnp.where(rel >= 0, jnp.exp(log_g[:, None, None] * jnp.maximum(rel, 0.0)), 0.0)
    q_dec = jnp.exp(log_g[None, :] * (i[:, None] + 1.0))
    k_dec = jnp.exp(log_g[None, :] * (RET_CHUNK - 1.0 - i[:, None]))
    chunk_dec = jnp.exp(log_g * RET_CHUNK)

    def step(S, xs):
        qn, kn, vn = xs
        scores = jnp.einsum('bihd,bjhd->bhij', qn, kn) * dmat
        o = (jnp.einsum('bhij,bjhe->bihe', scores, vn)
             + jnp.einsum('bihd,bhde->bihe', qn * q_dec[None, :, :, None], S))
        S = S * chunk_dec[None, :, None, None] + jnp.einsum('bjhd,bjhe->bhde', kn * k_dec[None, :, :, None], vn)
        return S, o

    S0 = jnp.zeros((b, h, dk, dv), F32)
    _, o = lax.scan(step, S0, (chunks(q), chunks(k), chunks(v)))
    return jnp.moveaxis(o, 0, 1).reshape(b, L, h, dv)


def _retention_mixer(proj, decay_f, decay_b, norm_g):
    b, L, _ = proj.shape
    q, k, v, g = jnp.split(proj, 4, axis=-1)
    pos = jnp.arange(L)
    q = _rope(q.reshape(b, L, RET_HEADS, RET_DK), pos)
    k = _rope(k.reshape(b, L, RET_HEADS, RET_DK), pos) * RET_DK ** -0.5
    v = v.reshape(b, L, RET_HEADS, RET_DV)
    lg_f = jnp.log1p(-jnp.exp2(-decay_f.astype(F32)))
    lg_b = jnp.log1p(-jnp.exp2(-decay_b.astype(F32)))
    o_f = _retention_scan(q, k, v, lg_f)
    o_b = jnp.flip(_retention_scan(jnp.flip(q, 1), jnp.flip(k, 1), jnp.flip(v, 1), lg_b), 1)
    o = o_f + o_b
    mu = jnp.mean(o, axis=-1, keepdims=True)
    var = jnp.mean(jnp.square(o - mu), axis=-1, keepdims=True)
    o = ((o - mu) * lax.rsqrt(var + EPS)).reshape(b, L, D_HALF) * norm_g.astype(F32)
    return (jax.nn.silu(g.astype(F32)) * o).astype(proj.dtype)


def _gated_delta_scan(q, k, v, beta, log_a):
    b, L, h, dk = q.shape
    dv = v.shape[-1]
    C = DN_CHUNK
    n = L // C

    def chunks(a):
        a = a.astype(F32).reshape((b, n, C, h) + a.shape[3:])
        return jnp.moveaxis(jnp.moveaxis(a, 1, 0), 2, 3)

    qc, kc, vc, bc, ac = chunks(q), chunks(k), chunks(v), chunks(beta), chunks(log_a)
    g = jnp.cumsum(ac, axis=-1)
    mask_incl = jnp.tril(jnp.ones((C, C), bool))
    mask_strict = jnp.tril(jnp.ones((C, C), bool), -1)
    diff = g[..., :, None] - g[..., None, :]
    gam = jnp.where(mask_incl, jnp.exp(jnp.where(mask_incl, diff, 0.0)), 0.0)
    kb = kc * bc[..., None]
    t_mat = jnp.where(mask_strict, jnp.einsum('nbhid,nbhjd->nbhij', kb, kc) * gam, 0.0) + jnp.eye(C, dtype=F32)
    u = lax.linalg.triangular_solve(t_mat, vc * bc[..., None], left_side=True, lower=True)
    w = lax.linalg.triangular_solve(t_mat, kb * jnp.exp(g)[..., None], left_side=True, lower=True)
    g_last = g[..., -1]
    k_tail = kc * jnp.exp(g_last[..., None] - g)[..., None]
    q_head = qc * jnp.exp(g)[..., None]

    def step(S, xs):
        qn, kn, qh, kt, un, wn, gm, gl = xs
        v_new = un - jnp.einsum('bhid,bhde->bhie', wn, S)
        attn = jnp.einsum('bhid,bhjd->bhij', qn, kn) * gm
        o = jnp.einsum('bhid,bhde->bhie', qh, S) + jnp.einsum('bhij,bhje->bhie', attn, v_new)
        S = S * jnp.exp(gl)[..., None, None] + jnp.einsum('bhjd,bhje->bhde', kt, v_new)
        return S, o

    S0 = jnp.zeros((b, h, dk, dv), F32)
    _, o = lax.scan(step, S0, (qc, kc, q_head, k_tail, u, w, gam, g_last))
    return jnp.moveaxis(o, 0, 1).transpose(0, 1, 3, 2, 4).reshape(b, L, h, dv)


def _deltanet_mixer(proj, conv_w, a_log_f, a_log_b, dt_bias_f, dt_bias_b, norm_g):
    b, L, _ = proj.shape
    qkv = proj[..., :3 * D_HALF]
    z = proj[..., 3 * D_HALF:4 * D_HALF]
    bf, bb, af, ab = jnp.split(proj[..., 4 * D_HALF:], 4, axis=-1)
    left = DN_CONV // 2
    right = DN_CONV - 1 - left
    xp = jnp.pad(qkv, ((0, 0), (left, right), (0, 0)))
    conv = xp[:, 0:L] * conv_w[0]
    for j in range(1, DN_CONV):
        conv = conv + xp[:, j:j + L] * conv_w[j]
    q, k, v = jnp.split(jax.nn.silu(conv), 3, axis=-1)
    q = _l2norm(q.reshape(b, L, DN_HEADS, DN_DK)) * DN_DK ** -0.5
    k = _l2norm(k.reshape(b, L, DN_HEADS, DN_DK))
    v = v.reshape(b, L, DN_HEADS, DN_DV)

    def gates(braw, araw, a_log, dt_bias):
        beta = jax.nn.sigmoid(braw.astype(F32))
        log_a = -jnp.exp(a_log.astype(F32)) * jax.nn.softplus(araw.astype(F32) + dt_bias.astype(F32))
        return beta, log_a

    beta_f, la_f = gates(bf, af, a_log_f, dt_bias_f)
    beta_b, la_b = gates(bb, ab, a_log_b, dt_bias_b)
    o_f = _gated_delta_scan(q, k, v, beta_f, la_f)
    o_b = jnp.flip(_gated_delta_scan(jnp.flip(q, 1), jnp.flip(k, 1), jnp.flip(v, 1),
                                     jnp.flip(beta_b, 1), jnp.flip(la_b, 1)), 1)
    o = o_f + o_b
    o = o * lax.rsqrt(jnp.mean(o * o, axis=-1, keepdims=True) + EPS) * norm_g.astype(F32)
    o = o.reshape(b, L, D_HALF) * jax.nn.silu(z.astype(F32))
    return o.astype(proj.dtype)


def _even_mixer(h, w_in, w_out, pool_w, pool_scale, sgu_norm, sgu_w, sgu_b):
    proj = h @ w_in
    ya = _pool_mixer(proj[..., :D_HALF], pool_w, pool_scale)
    yb = _sgu_mixer(proj[..., D_HALF:], sgu_norm, sgu_w, sgu_b)
    return jnp.concatenate([ya, yb], axis=-1) @ w_out


def _odd_mixer(h, w_in, w_out, ret_decay_f, ret_decay_b, ret_norm,
               dn_conv, dn_a_log_f, dn_a_log_b, dn_dt_bias_f, dn_dt_bias_b, dn_norm):
    proj = h @ w_in
    yc = _retention_mixer(proj[..., :P_RET], ret_decay_f, ret_decay_b, ret_norm)
    yd = _deltanet_mixer(proj[..., P_RET:], dn_conv, dn_a_log_f, dn_a_log_b, dn_dt_bias_f, dn_dt_bias_b, dn_norm)
    return jnp.concatenate([yc, yd], axis=-1) @ w_out


def _trunk(x, c, p):
    b = x.shape[0]
    c_act = jax.nn.silu(c)
    for layer in range(DEPTH):
        mod = (c_act @ p['w_ada'][layer] + p['b_ada'][layer]).reshape(b, N_MOD, 1, D_MODEL)
        sh1, sc1, g1, sh2, sc2, g2, sh3, sc3, g3 = [mod[:, j] for j in range(N_MOD)]
        h = _modulate(x, p['norm_ffn1'][layer], sh1, sc1)
        x = x + 0.5 * g1 * _swiglu(h, p['w_ffn1_in'][layer], p['w_ffn1_out'][layer])
        h = _modulate(x, p['norm_mix'][layer], sh2, sc2)
        idx = layer // 2
        if layer % 2 == 0:
            y = _even_mixer(h, p['w_in_even'][idx], p['w_out_even'][idx], p['pool_w'][idx], p['pool_scale'][idx],
                            p['sgu_norm'][idx], p['sgu_w'][idx], p['sgu_b'][idx])
        else:
            y = _odd_mixer(h, p['w_in_odd'][idx], p['w_out_odd'][idx], p['ret_decay_f'][idx], p['ret_decay_b'][idx],
                           p['ret_norm'][idx], p['dn_conv'][idx], p['dn_a_log_f'][idx], p['dn_a_log_b'][idx],
                           p['dn_dt_bias_f'][idx], p['dn_dt_bias_b'][idx], p['dn_norm'][idx])
        x = x + g2 * y
        h = _modulate(x, p['norm_ffn2'][layer], sh3, sc3)
        x = x + 0.5 * g3 * _swiglu(h, p['w_ffn2_in'][layer], p['w_ffn2_out'][layer])
    fm = (c_act @ p['w_ada_final'] + p['b_ada_final']).reshape(b, 2, 1, D_MODEL)
    return _modulate(x, p['norm_final'], fm[:, 0], fm[:, 1])


def setup_inputs(seed: int = 0) -> dict:
    key = jax.random.key(seed)
    ks = iter(jax.random.split(key, 48))

    def nrm(shape, scale):
        return jax.random.normal(next(ks), shape, F32) * scale

    def gain(shape, noise=0.02):
        return 1.0 + nrm(shape, noise)

    def a_log(shape):
        return jnp.log(jax.random.uniform(next(ks), shape, F32, 1.0, 16.0))

    def dt_bias(shape):
        dt = jnp.exp(jax.random.uniform(next(ks), shape, F32, np.log(1e-3), np.log(1e-1)))
        return dt + jnp.log(-jnp.expm1(-dt))

    ret_base = 5.0 + jnp.arange(RET_HEADS, dtype=F32)[None, :]
    return {
        'x_prompt': nrm((BATCH, SEQ, D_MODEL), 1.0),
        'x_sample': nrm((DEC_BATCH, DEC_SEQ, D_MODEL), 1.0),
        'c_prompt': nrm((BATCH, D_MODEL), 1.0),
        'c_sample': nrm((DEC_BATCH, D_MODEL), 1.0),
        'w_ada': nrm((DEPTH, D_MODEL, N_MOD * D_MODEL), D_MODEL ** -0.5),
        'b_ada': nrm((DEPTH, N_MOD * D_MODEL), 0.02),
        'norm_ffn1': gain((DEPTH, D_MODEL)),
        'w_ffn1_in': nrm((DEPTH, D_MODEL, 2 * D_FF), D_MODEL ** -0.5),
        'w_ffn1_out': nrm((DEPTH, D_FF, D_MODEL), D_FF ** -0.5),
        'norm_mix': gain((DEPTH, D_MODEL)),
        'norm_ffn2': gain((DEPTH, D_MODEL)),
        'w_ffn2_in': nrm((DEPTH, D_MODEL, 2 * D_FF), D_MODEL ** -0.5),
        'w_ffn2_out': nrm((DEPTH, D_FF, D_MODEL), D_FF ** -0.5),
        'w_in_even': nrm((N_EVEN, D_MODEL, P_EVEN), D_MODEL ** -0.5),
        'w_out_even': nrm((N_EVEN, D_MIX, D_MODEL), D_MIX ** -0.5),
        'pool_w': nrm((N_EVEN, N_POOL, POOL_GROUP, POOL_GROUP), POOL_GROUP ** -0.5),
        'pool_scale': gain((N_EVEN, D_HALF), 0.1),
        'sgu_norm': gain((N_EVEN, D_HALF)),
        'sgu_w': nrm((N_EVEN, SGU_GROUPS, SGU_CHUNK, SGU_CHUNK), SGU_CHUNK ** -0.5),
        'sgu_b': gain((N_EVEN, SGU_GROUPS, SGU_CHUNK), 0.1),
        'w_in_odd': nrm((N_ODD, D_MODEL, P_ODD), D_MODEL ** -0.5),
        'w_out_odd': nrm((N_ODD, D_MIX, D_MODEL), D_MIX ** -0.5),
        'ret_decay_f': ret_base + nrm((N_ODD, RET_HEADS), 0.1),
        'ret_decay_b': ret_base + nrm((N_ODD, RET_HEADS), 0.1),
        'ret_norm': gain((N_ODD, D_HALF)),
        'dn_conv': nrm((N_ODD, DN_CONV, 3 * D_HALF), DN_CONV ** -0.5),
        'dn_a_log_f': a_log((N_ODD, DN_HEADS)),
        'dn_a_log_b': a_log((N_ODD, DN_HEADS)),
        'dn_dt_bias_f': dt_bias((N_ODD, DN_HEADS)),
        'dn_dt_bias_b': dt_bias((N_ODD, DN_HEADS)),
        'dn_norm': gain((N_ODD, DN_DV)),
        'norm_final': gain((D_MODEL,)),
        'w_ada_final': nrm((D_MODEL, 2 * D_MODEL), D_MODEL ** -0.5),
        'b_ada_final': nrm((2 * D_MODEL,), 0.02),
    }


def reference(x_prompt, x_sample, c_prompt, c_sample, w_ada, b_ada, norm_ffn1, w_ffn1_in, w_ffn1_out,
              norm_mix, norm_ffn2, w_ffn2_in, w_ffn2_out, w_in_even, w_out_even, pool_w, pool_scale,
              sgu_norm, sgu_w, sgu_b, w_in_odd, w_out_odd, ret_decay_f, ret_decay_b, ret_norm,
              dn_conv, dn_a_log_f, dn_a_log_b, dn_dt_bias_f, dn_dt_bias_b, dn_norm,
              norm_final, w_ada_final, b_ada_final):
    p = {
        'w_ada': w_ada, 'b_ada': b_ada, 'norm_ffn1': norm_ffn1, 'w_ffn1_in': w_ffn1_in,
        'w_ffn1_out': w_ffn1_out, 'norm_mix': norm_mix, 'norm_ffn2': norm_ffn2,
        'w_ffn2_in': w_ffn2_in, 'w_ffn2_out': w_ffn2_out, 'w_in_even': w_in_even,
        'w_out_even': w_out_even, 'pool_w': pool_w, 'pool_scale': pool_scale,
        'sgu_norm': sgu_norm, 'sgu_w': sgu_w, 'sgu_b': sgu_b, 'w_in_odd': w_in_odd,
        'w_out_odd': w_out_odd, 'ret_decay_f': ret_decay_f, 'ret_decay_b': ret_decay_b,
        'ret_norm': ret_norm, 'dn_conv': dn_conv, 'dn_a_log_f': dn_a_log_f,
        'dn_a_log_b': dn_a_log_b, 'dn_dt_bias_f': dn_dt_bias_f, 'dn_dt_bias_b': dn_dt_bias_b,
        'dn_norm': dn_norm, 'norm_final': norm_final, 'w_ada_final': w_ada_final,
        'b_ada_final': b_ada_final,
    }
    y_prompt = _trunk(x_prompt, c_prompt, p)
    y_sample = _trunk(x_sample, c_sample, p)
    return (y_prompt, y_sample)
```

```python
import functools

import jax
import jax.numpy as jnp
import numpy as np
from jax import lax
from jax.experimental import pallas as pl
from jax.experimental.pallas import tpu as pltpu

F32 = jnp.float32
BF16 = jnp.bfloat16

D_MODEL = 2048
DEPTH = 4
D_HALF = D_MODEL // 2
POOL_WINDOWS = (2, 4, 8, 16)
POOL_GROUP = D_HALF // len(POOL_WINDOWS)
SGU_CHUNK = 128
SGU_HEAD = 128
SGU_GROUPS = D_HALF // SGU_HEAD
RET_HEADS = 4
RET_D = D_HALF // RET_HEADS
RET_CHUNK = 128
ROPE_BASE = 10000.0
DN_HEADS = 8
DN_D = D_HALF // DN_HEADS
DN_CONV = 4
DN_CHUNK = 64
D_FF = 5632
N_MOD = 9
EPS = 1e-6
P_EVEN = 3 * D_HALF
P_RET = 4 * D_HALF
P_ODD = P_RET + 4 * D_HALF + 4 * DN_HEADS

LANES = 128
SUBLANES = 8
HALO = SUBLANES
VMEM_LIMIT = 56 * 1024 * 1024

TM_FFN = 512
TF_FFN = 512
TM_PROJ = 1024
TN_PROJ_EVEN = 512
TN_PROJ_ODD = 640
TM_OUT = 512
TL_EVEN = 256
TL_RET = 512
TL_DN = 256
TL_PREP = 256
TM_FINAL = 512
TN_ADA = 1024
SEQ_PAD = 16

P_ODD_PAD = P_ODD + (-P_ODD) % TN_PROJ_ODD


def _seq_info(row0, groups):
    seq = start = length = None
    t0 = s0 = 0
    for gi, (nb, ln) in enumerate(groups):
        rel = row0 - t0
        q = rel // ln
        if gi == 0:
            seq, start, length = q, q * ln, ln
        else:
            here = row0 >= t0
            seq = jnp.where(here, s0 + q, seq)
            start = jnp.where(here, t0 + q * ln, start)
            length = jnp.where(here, ln, length)
        t0 += nb * ln
        s0 += nb
    return seq, start, length


def _params(*sem):
    return pltpu.CompilerParams(dimension_semantics=sem, vmem_limit_bytes=VMEM_LIMIT)


def _modulated(x, gain, shift, scale):
    ms = jnp.mean(x * x, axis=-1, keepdims=True)
    y = x * lax.rsqrt(ms + EPS)
    return (y * gain) * (1.0 + scale) + shift


def _dot(a, b):
    return jnp.dot(a, b, preferred_element_type=F32)


def _dot_nt(a, b):
    return lax.dot_general(a, b, (((1,), (1,)), ((), ())), preferred_element_type=F32)


def _dot_tn(a, b):
    return lax.dot_general(a, b, (((0,), (0,)), ((), ())), preferred_element_type=F32)


def _split_bf16(a):
    hi = a.astype(BF16)
    lo = (a - hi.astype(F32)).astype(BF16)
    return hi, lo


def _dot_hi(a, b):
    ah, al = _split_bf16(a)
    bh, bl = _split_bf16(b)
    return _dot(ah, bh) + (_dot(ah, bl) + _dot(al, bh))


def _ada_kernel(c_ref, w_ref, b_ref, o_ref):
    c = c_ref[...]
    act = jax.nn.silu(c).astype(BF16)
    o_ref[...] = _dot(act, w_ref[...].astype(BF16)) + b_ref[...]


def _ada(c_pad, w, b):
    ly, d, n = w.shape
    s = c_pad.shape[0]
    tn = min(TN_ADA, n)
    return pl.pallas_call(
        _ada_kernel,
        out_shape=jax.ShapeDtypeStruct((ly, s, n), F32),
        grid=(ly, n // tn),
        in_specs=[
            pl.BlockSpec((s, d), lambda l, j: (0, 0)),
            pl.BlockSpec((None, d, tn), lambda l, j: (l, 0, j)),
            pl.BlockSpec((None, 1, tn), lambda l, j: (l, 0, j)),
        ],
        out_specs=pl.BlockSpec((None, s, tn), lambda l, j: (l, 0, j)),
        compiler_params=_params("parallel", "parallel"),
        name="ada_rows",
    )(c_pad, w, b.reshape(ly, 1, n))


def _ffn_kernel(x_ref, gain_ref, sh_ref, sc_ref, gt_ref, wg_ref, wu_ref, wo_ref, o_ref,
                h_ref, acc_ref, *, nf):
    f = pl.program_id(1)

    @pl.when(f == 0)
    def _():
        h = _modulated(x_ref[...], gain_ref[...], sh_ref[...], sc_ref[...])
        h_ref[...] = h.astype(BF16)
        acc_ref[...] = jnp.zeros_like(acc_ref)

    h = h_ref[...]
    g = _dot(h, wg_ref[...])
    u = _dot(h, wu_ref[...])
    a = (jax.nn.silu(g) * u).astype(BF16)
    acc_ref[...] += _dot(a, wo_ref[...])

    @pl.when(f == nf - 1)
    def _():
        o_ref[...] = x_ref[...] + (0.5 * gt_ref[...]) * acc_ref[...]


def _row_spec(groups, tm, d):
    return pl.BlockSpec((None, 1, d), lambda i, j: (_seq_info(i * tm, groups)[0], 0, 0))


def _ffn(x, gain, shift, scale, gate, w_in, w_out, groups):
    t, d = x.shape
    ff = w_out.shape[0]
    tm = min(TM_FFN, groups[0][1])
    tf = min(TF_FFN, ff)
    nf = ff // tf
    row = _row_spec(groups, tm, d)
    return pl.pallas_call(
        functools.partial(_ffn_kernel, nf=nf),
        out_shape=jax.ShapeDtypeStruct((t, d), F32),
        grid=(t // tm, nf),
        in_specs=[
            pl.BlockSpec((tm, d), lambda i, f: (i, 0)),
            pl.BlockSpec((1, d), lambda i, f: (0, 0)),
            row, row, row,
            pl.BlockSpec((d, tf), lambda i, f: (0, f)),
            pl.BlockSpec((d, tf), lambda i, f: (0, nf + f)),
            pl.BlockSpec((tf, d), lambda i, f: (f, 0)),
        ],
        out_specs=pl.BlockSpec((tm, d), lambda i, f: (i, 0)),
        scratch_shapes=[pltpu.VMEM((tm, d), BF16), pltpu.VMEM((tm, d), F32)],
        compiler_params=_params("parallel", "arbitrary"),
        name="ffn",
    )(x, gain, shift, scale, gate, w_in, w_in, w_out)


def _proj_kernel(x_ref, gain_ref, sh_ref, sc_ref, w_ref, o_ref, h_ref):
    @pl.when(pl.program_id(1) == 0)
    def _():
        h = _modulated(x_ref[...], gain_ref[...], sh_ref[...], sc_ref[...])
        h_ref[...] = h.astype(BF16)

    o_ref[...] = _dot(h_ref[...], w_ref[...])


def _proj(x, gain, shift, scale, w, tn, groups):
    t, d = x.shape
    n = w.shape[1]
    tm = min(TM_PROJ, groups[0][1])
    row = _row_spec(groups, tm, d)
    return pl.pallas_call(
        _proj_kernel,
        out_shape=jax.ShapeDtypeStruct((t, n), F32),
        grid=(t // tm, n // tn),
        in_specs=[
            pl.BlockSpec((tm, d), lambda i, j: (i, 0)),
            pl.BlockSpec((1, d), lambda i, j: (0, 0)),
            row, row,
            pl.BlockSpec((d, tn), lambda i, j: (0, j)),
        ],
        out_specs=pl.BlockSpec((tm, tn), lambda i, j: (i, j)),
        scratch_shapes=[pltpu.VMEM((tm, d), BF16)],
        compiler_params=_params("parallel", "arbitrary"),
        name="mix_proj",
    )(x, gain, shift, scale, w)


def _outproj_kernel(x_ref, ya_ref, yb_ref, gt_ref, wa_ref, wb_ref, o_ref):
    y = _dot(ya_ref[...], wa_ref[...]) + _dot(yb_ref[...], wb_ref[...])
    o_ref[...] = x_ref[...] + gt_ref[...] * y


def _outproj(x, ya, yb, ca, cb, gate, w, groups):
    t, d = x.shape
    dh = d // 2
    tm = min(TM_OUT, groups[0][1])
    row = _row_spec(groups, tm, d)
    return pl.pallas_call(
        _outproj_kernel,
        out_shape=jax.ShapeDtypeStruct((t, d), F32),
        grid=(t // tm, 1),
        in_specs=[
            pl.BlockSpec((tm, d), lambda i, j: (i, 0)),
            pl.BlockSpec((tm, dh), lambda i, j: (i, ca)),
            pl.BlockSpec((tm, dh), lambda i, j: (i, cb)),
            row,
            pl.BlockSpec((dh, d), lambda i, j: (0, 0)),
            pl.BlockSpec((dh, d), lambda i, j: (1, 0)),
        ],
        out_specs=pl.BlockSpec((tm, d), lambda i, j: (i, 0)),
        compiler_params=_params("parallel", "arbitrary"),
        name="mix_out",
    )(x, ya, yb, gate, w, w)


def _final_kernel(x_ref, gain_ref, sh_ref, sc_ref, o_ref):
    o_ref[...] = _modulated(x_ref[...], gain_ref[...], sh_ref[...], sc_ref[...])


def _final(x, gain, shift, scale, groups):
    t, d = x.shape
    tm = min(TM_FINAL, groups[0][1])
    row = _row_spec(groups, tm, d)
    return pl.pallas_call(
        _final_kernel,
        out_shape=jax.ShapeDtypeStruct((t, d), F32),
        grid=(t // tm, 1),
        in_specs=[
            pl.BlockSpec((tm, d), lambda i, j: (i, 0)),
            pl.BlockSpec((1, d), lambda i, j: (0, 0)),
            row, row,
        ],
        out_specs=pl.BlockSpec((tm, d), lambda i, j: (i, 0)),
        compiler_params=_params("parallel", "arbitrary"),
        name="final_mod",
    )(x, gain, shift, scale)


def _halo_specs(tl, width, nrows, col_of):
    per = tl // HALO
    last = nrows // HALO - 1
    prev = pl.BlockSpec((HALO, width), lambda i, *r: (jnp.maximum(i * per - 1, 0), col_of(i, *r)))
    nxt = pl.BlockSpec((HALO, width), lambda i, *r: (jnp.minimum((i + 1) * per, last), col_of(i, *r)))
    return prev, nxt


def _fill_ext(ext_ref, x_ref, prev_ref, next_ref, first, last, tl):
    ext_ref[HALO:HALO + tl, :] = x_ref[...]
    ext_ref[0:HALO, :] = jnp.where(first, 0.0, prev_ref[...])
    ext_ref[HALO + tl:2 * HALO + tl, :] = jnp.where(last, 0.0, next_ref[...])


def _even_kernel(xa_ref, prev_ref, next_ref, u_ref, v_ref, pw_ref, ps_ref, ng_ref, sw_ref, sb_ref,
                 o_ref, ext_ref, vn_ref, *, tl, groups):
    row0 = pl.program_id(0) * tl
    _, sstart, slen = _seq_info(row0, groups)
    pos0 = row0 - sstart
    _fill_ext(ext_ref, xa_ref, prev_ref, next_ref, pos0 == 0, pos0 + tl == slen, tl)

    t = pos0 + lax.broadcasted_iota(jnp.int32, (tl, 1), 0)
    for gi, w in enumerate(POOL_WINDOWS):
        c0 = gi * POOL_GROUP
        cols = slice(c0, c0 + POOL_GROUP)
        base = HALO - w // 2
        s = ext_ref[base:base + tl, cols]
        for dlt in range(1, w):
            s = s + ext_ref[base + dlt:base + dlt + tl, cols]
        lo = jnp.clip(t - w // 2, 0, slen)
        hi = jnp.clip(t + (w - w // 2), 0, slen)
        cnt = (hi - lo).astype(F32)
        pooled = (s / cnt - xa_ref[:, cols]).astype(BF16)
        ya = _dot(pooled, pw_ref[gi]) * ps_ref[:, cols]
        o_ref[:, cols] = ya.astype(BF16)

    v = jax.nn.gelu(v_ref[...])
    vms = jnp.mean(v * v, axis=-1, keepdims=True)
    vn_ref[...] = ((v * lax.rsqrt(vms + EPS)) * ng_ref[...]).astype(BF16)
    for n in range(tl // SGU_CHUNK):
        rows = slice(n * SGU_CHUNK, (n + 1) * SGU_CHUNK)
        for g in range(SGU_GROUPS):
            cols = slice(g * SGU_HEAD, (g + 1) * SGU_HEAD)
            mixed = _dot(sw_ref[g], vn_ref[rows, cols]) + sb_ref[g]
            u = jax.nn.gelu(u_ref[rows, cols])
            o_ref[rows, D_HALF + g * SGU_HEAD:D_HALF + (g + 1) * SGU_HEAD] = (u * mixed).astype(BF16)


def _even_mix(proj, pool_w, pool_scale, sgu_norm, sgu_w, sgu_b, groups):
    t = proj.shape[0]
    tl = min(TL_EVEN, groups[0][1])
    prev, nxt = _halo_specs(tl, D_HALF, t, lambda i: 0)
    const2 = lambda i: (0, 0)
    const3 = lambda i: (0, 0, 0)
    return pl.pallas_call(
        functools.partial(_even_kernel, tl=tl, groups=groups),
        out_shape=jax.ShapeDtypeStruct((t, 2 * D_HALF), BF16),
        grid=(t // tl,),
        in_specs=[
            pl.BlockSpec((tl, D_HALF), lambda i: (i, 0)),
            prev, nxt,
            pl.BlockSpec((tl, D_HALF), lambda i: (i, 1)),
            pl.BlockSpec((tl, D_HALF), lambda i: (i, 2)),
            pl.BlockSpec(pool_w.shape, const3),
            pl.BlockSpec((1, D_HALF), const2),
            pl.BlockSpec((1, D_HALF), const2),
            pl.BlockSpec(sgu_w.shape, const3),
            pl.BlockSpec(sgu_b.shape, const3),
        ],
        out_specs=pl.BlockSpec((tl, 2 * D_HALF), lambda i: (i, 0)),
        scratch_shapes=[pltpu.VMEM((tl + 2 * HALO, D_HALF), F32), pltpu.VMEM((tl, D_HALF), BF16)],
        compiler_params=_params("parallel"),
        name="even_mix",
    )(proj, proj, proj, proj, proj, pool_w, pool_scale, sgu_norm, sgu_w, sgu_b)


def _rope_kernel(inv_ref, cos_ref, sin_ref, *, tl):
    pos = (pl.program_id(0) * tl + lax.broadcasted_iota(jnp.int32, (tl, 1), 0)).astype(F32)
    ang = pos * inv_ref[...]
    cos_ref[...] = jnp.cos(ang)
    sin_ref[...] = jnp.sin(ang)


def _rope_tables(max_len, tl):
    half = RET_D // 2
    inv = (1.0 / (ROPE_BASE ** jnp.linspace(0.0, 1.0, half, dtype=F32))).reshape(1, half)
    shp = jax.ShapeDtypeStruct((max_len, half), F32)
    return pl.pallas_call(
        functools.partial(_rope_kernel, tl=tl),
        out_shape=(shp, shp),
        grid=(max_len // tl,),
        in_specs=[pl.BlockSpec((1, half), lambda i: (0, 0))],
        out_specs=(pl.BlockSpec((tl, half), lambda i: (i, 0)), pl.BlockSpec((tl, half), lambda i: (i, 0))),
        compiler_params=_params("parallel"),
        name="rope_table",
    )(inv)


def _rope_apply(x, cos, sin):
    half = RET_D // 2
    x1 = x[:, :half]
    x2 = x[:, half:]
    return jnp.concatenate([x1 * cos - x2 * sin, x1 * sin + x2 * cos], axis=-1)


def _ret_kernel(*refs, tl, nt, groups, reverse):
    if reverse:
        q_ref, k_ref, v_ref, cos_ref, sin_ref, dec_ref, o_ref, s_ref = refs
    else:
        (q_ref, k_ref, v_ref, cos_ref, sin_ref, dec_ref, decb_ref, g_ref, ob_ref, ng_ref,
         o_ref, s_ref) = refs
    j = pl.program_id(1)
    it = nt - 1 - j if reverse else j
    row0 = it * tl
    _, sstart, slen = _seq_info(row0, groups)
    pos0 = row0 - sstart
    reset = (pos0 + tl == slen) if reverse else (pos0 == 0)

    @pl.when(reset)
    def _():
        s_ref[...] = jnp.zeros_like(s_ref)

    c = RET_CHUNK
    lg = jnp.log1p(-jnp.exp2(-dec_ref[...]))
    idx = lax.broadcasted_iota(jnp.int32, (c, 1), 0).astype(F32)
    if reverse:
        q_dec = jnp.exp(lg * (c - idx))
        k_dec = jnp.exp(lg * idx)
    else:
        q_dec = jnp.exp(lg * (idx + 1.0))
        k_dec = jnp.exp(lg * (c - 1.0 - idx))
        lgb = jnp.log1p(-jnp.exp2(-decb_ref[...]))
        ri = lax.broadcasted_iota(jnp.int32, (c, c), 0)
        ci = lax.broadcasted_iota(jnp.int32, (c, c), 1)
        rel = (ri - ci).astype(F32)
        dmat = (jnp.where(rel >= 0, jnp.exp(lg * jnp.maximum(rel, 0.0)), 0.0)
                + jnp.where(rel <= 0, jnp.exp(lgb * jnp.maximum(-rel, 0.0)), 0.0))
    chunk_dec = jnp.exp(lg * float(c))

    nc = tl // c
    order = range(nc - 1, -1, -1) if reverse else range(nc)
    for ch in order:
        rows = slice(ch * c, (ch + 1) * c)
        cos = cos_ref[rows, :]
        sin = sin_ref[rows, :]
        q = _rope_apply(q_ref[rows, :], cos, sin)
        k = _rope_apply(k_ref[rows, :], cos, sin) * (RET_D ** -0.5)
        v = v_ref[rows, :].astype(BF16)
        s = s_ref[...]
        inter = _dot((q * q_dec).astype(BF16), s.astype(BF16))
        s_ref[...] = s * chunk_dec + _dot_tn((k * k_dec).astype(BF16), v)
        if reverse:
            o_ref[rows, :] = inter
        else:
            scores = _dot_nt(q.astype(BF16), k.astype(BF16)) * dmat
            o = _dot(scores.astype(BF16), v) + inter + ob_ref[rows, :]
            mu = jnp.mean(o, axis=-1, keepdims=True)
            var = jnp.mean(jnp.square(o - mu), axis=-1, keepdims=True)
            on = ((o - mu) * lax.rsqrt(var + EPS)) * ng_ref[...]
            o_ref[rows, :] = (jax.nn.silu(g_ref[rows, :]) * on).astype(BF16)


def _retention(proj, cos, sin, decay_f, decay_b, norm_g, groups):
    t = proj.shape[0]
    tl = min(TL_RET, groups[0][1])
    nt = t // tl
    hb = D_HALF // RET_D
    half = RET_D // 2

    def tile(j, reverse):
        return nt - 1 - j if reverse else j

    def specs(reverse):
        def pos_block(h, j):
            row0 = tile(j, reverse) * tl
            return ((row0 - _seq_info(row0, groups)[1]) // tl, 0)
        sec = lambda s: pl.BlockSpec((tl, RET_D), lambda h, j: (tile(j, reverse), s * hb + h))
        tab = pl.BlockSpec((tl, half), pos_block)
        dec = pl.BlockSpec((None, 1, 1), lambda h, j: (h, 0, 0))
        return sec, tab, dec

    sec, tab, dec = specs(True)
    ob = pl.pallas_call(
        functools.partial(_ret_kernel, tl=tl, nt=nt, groups=groups, reverse=True),
        out_shape=jax.ShapeDtypeStruct((t, D_HALF), F32),
        grid=(RET_HEADS, nt),
        in_specs=[sec(0), sec(1), sec(2), tab, tab, dec],
        out_specs=pl.BlockSpec((tl, RET_D), lambda h, j: (nt - 1 - j, h)),
        scratch_shapes=[pltpu.VMEM((RET_D, RET_D), F32)],
        compiler_params=_params("parallel", "arbitrary"),
        name="ret_bwd",
    )(proj, proj, proj, cos, sin, decay_b.reshape(RET_HEADS, 1, 1))

    sec, tab, dec = specs(False)
    return pl.pallas_call(
        functools.partial(_ret_kernel, tl=tl, nt=nt, groups=groups, reverse=False),
        out_shape=jax.ShapeDtypeStruct((t, D_HALF), BF16),
        grid=(RET_HEADS, nt),
        in_specs=[sec(0), sec(1), sec(2), tab, tab, dec, dec, sec(3),
                  pl.BlockSpec((tl, RET_D), lambda h, j: (j, h)),
                  pl.BlockSpec((1, RET_D), lambda h, j: (0, h))],
        out_specs=pl.BlockSpec((tl, RET_D), lambda h, j: (j, h)),
        scratch_shapes=[pltpu.VMEM((RET_D, RET_D), F32)],
        compiler_params=_params("parallel", "arbitrary"),
        name="ret_fwd",
    )(proj, proj, proj, cos, sin, decay_f.reshape(RET_HEADS, 1, 1), decay_b.reshape(RET_HEADS, 1, 1),
      proj, ob, norm_g)


def _dnprep_kernel(x_ref, prev_ref, next_ref, w_ref, o_ref, ext_ref, *, tl, groups):
    row0 = pl.program_id(0) * tl
    part = pl.program_id(1)
    _, sstart, slen = _seq_info(row0, groups)
    pos0 = row0 - sstart
    _fill_ext(ext_ref, x_ref, prev_ref, next_ref, pos0 == 0, pos0 + tl == slen, tl)
    left = DN_CONV // 2
    q_scale = jnp.where(part == 0, DN_D ** -0.5, 1.0).astype(F32)
    for h in range(DN_HEADS):
        cols = slice(h * DN_D, (h + 1) * DN_D)
        conv = ext_ref[HALO - left:HALO - left + tl, cols] * w_ref[0:1, cols]
        for tap in range(1, DN_CONV):
            r0 = HALO - left + tap
            conv = conv + ext_ref[r0:r0 + tl, cols] * w_ref[tap:tap + 1, cols]
        y = jax.nn.silu(conv)
        inv_norm = lax.rsqrt(jnp.sum(y * y, axis=-1, keepdims=True) + EPS)
        o_ref[:, cols] = y * jnp.where(part < 2, inv_norm * q_scale, 1.0)


def _dn_prep(proj, conv_w, groups):
    t = proj.shape[0]
    tl = min(TL_PREP, groups[0][1])
    base = P_RET // D_HALF
    prev, nxt = _halo_specs(tl, D_HALF, t, lambda i, part: base + part)
    return pl.pallas_call(
        functools.partial(_dnprep_kernel, tl=tl, groups=groups),
        out_shape=jax.ShapeDtypeStruct((t, 3 * D_HALF), F32),
        grid=(t // tl, 3),
        in_specs=[pl.BlockSpec((tl, D_HALF), lambda i, part: (i, base + part)), prev, nxt,
                  pl.BlockSpec((DN_CONV, D_HALF), lambda i, part: (0, part))],
        out_specs=pl.BlockSpec((tl, D_HALF), lambda i, part: (i, part)),
        scratch_shapes=[pltpu.VMEM((tl + 2 * HALO, D_HALF), F32)],
        compiler_params=_params("parallel", "parallel"),
        name="dn_prep",
    )(proj, proj, proj, conv_w)


def _unit_triangular_inverse(a, eye, ri, ci):
    size = SUBLANES
    same = (ri // size) == (ci // size)
    d = jnp.where(same, a, 0.0)
    inv = eye - d
    p = d
    n = 2
    while n < size:
        p = _dot_hi(p, p)
        inv = inv + _dot_hi(inv, p)
        n *= 2
    while size < DN_CHUNK:
        size *= 2
        merged = (ri // size) == (ci // size)
        e = jnp.where(merged & ~same, a, 0.0)
        inv = inv - _dot_hi(_dot_hi(inv, e), inv)
        same = merged
    return inv


def _dn_kernel(*refs, tl, nt, groups, reverse):
    if reverse:
        q_ref, k_ref, v_ref, gates_ref, alog_ref, dtb_ref, o_ref, s_ref = refs
    else:
        (q_ref, k_ref, v_ref, gates_ref, alog_ref, dtb_ref, z_ref, ob_ref, ng_ref,
         o_ref, s_ref) = refs
    h = pl.program_id(0)
    j = pl.program_id(1)
    it = nt - 1 - j if reverse else j
    row0 = it * tl
    _, sstart, slen = _seq_info(row0, groups)
    pos0 = row0 - sstart
    reset = (pos0 + tl == slen) if reverse else (pos0 == 0)

    @pl.when(reset)
    def _():
        s_ref[...] = jnp.zeros_like(s_ref)

    gates = gates_ref[...]
    lane = lax.broadcasted_iota(jnp.int32, (1, LANES), 1)
    cb = h + (DN_HEADS if reverse else 0)
    ca = cb + 2 * DN_HEADS
    beta_all = jax.nn.sigmoid(gates)
    la_all = -jnp.exp(alog_ref[...]) * jax.nn.softplus(gates + dtb_ref[...])
    beta_t = jnp.sum(jnp.where(lane == cb, beta_all, 0.0), axis=-1, keepdims=True)
    la_t = jnp.sum(jnp.where(lane == ca, la_all, 0.0), axis=-1, keepdims=True)

    c = DN_CHUNK
    ri = lax.broadcasted_iota(jnp.int32, (c, c), 0)
    ci = lax.broadcasted_iota(jnp.int32, (c, c), 1)
    eye = (ri == ci).astype(F32)
    incl = (ri <= ci) if reverse else (ri >= ci)
    strict = (ri < ci) if reverse else (ri > ci)
    incl_t = (ri >= ci) if reverse else (ri <= ci)
    last = 0 if reverse else c - 1

    nc = tl // c
    order = range(nc - 1, -1, -1) if reverse else range(nc)
    for ch in order:
        rows = slice(ch * c, (ch + 1) * c)
        q = q_ref[rows, :]
        k = k_ref[rows, :]
        v = v_ref[rows, :]
        beta = beta_t[ch * c:(ch + 1) * c, :]
        la = la_t[ch * c:(ch + 1) * c, :]
        la_row = jnp.sum(eye * la, axis=0, keepdims=True)
        g_col = jnp.sum(jnp.where(incl, la_row, 0.0), axis=1, keepdims=True)
        g_row = jnp.sum(jnp.where(incl_t, la, 0.0), axis=0, keepdims=True)
        gam = jnp.where(incl, jnp.exp(jnp.where(incl, g_col - g_row, 0.0)), 0.0)
        kb = k * beta
        kbf = k.astype(BF16)
        a = jnp.where(strict, _dot_nt(kb.astype(BF16), kbf) * gam, 0.0)
        tinv = _unit_triangular_inverse(a, eye, ri, ci)
        eg = jnp.exp(g_col)
        u = _dot_hi(tinv, v * beta)
        w = _dot_hi(tinv, kb * eg)
        s = s_ref[...]
        sb = s.astype(BF16)
        v_new = u - _dot(w.astype(BF16), sb)
        attn = _dot_nt(q.astype(BF16), kbf) * gam
        vnb = v_new.astype(BF16)
        o = _dot((q * eg).astype(BF16), sb) + _dot(attn.astype(BF16), vnb)
        g_last = g_col[last:last + 1, :]
        s_ref[...] = s * jnp.exp(g_last) + _dot_tn((k * jnp.exp(g_last - g_col)).astype(BF16), vnb)
        if reverse:
            o_ref[rows, :] = o
        else:
            o = o + ob_ref[rows, :]
            on = (o * lax.rsqrt(jnp.mean(o * o, axis=-1, keepdims=True) + EPS)) * ng_ref[...]
            o_ref[rows, :] = (on * jax.nn.silu(z_ref[rows, :])).astype(BF16)


def _deltanet(proj, qkv, alog_row, dtb_row, norm_g, groups):
    t = proj.shape[0]
    tl = min(TL_DN, groups[0][1])
    nt = t // tl
    z_blk = (P_RET + 3 * D_HALF) // DN_D
    gate_blk = (P_RET + 4 * D_HALF) // LANES
    row = pl.BlockSpec((1, LANES), lambda h, j: (0, 0))

    def common(reverse):
        tile = (lambda j: nt - 1 - j) if reverse else (lambda j: j)
        head = lambda s: pl.BlockSpec((tl, DN_D), lambda h, j: (tile(j), s * DN_HEADS + h))
        gates = pl.BlockSpec((tl, LANES), lambda h, j: (tile(j), gate_blk))
        return tile, head, gates

    tile, head, gates = common(True)
    ob = pl.pallas_call(
        functools.partial(_dn_kernel, tl=tl, nt=nt, groups=groups, reverse=True),
        out_shape=jax.ShapeDtypeStruct((t, D_HALF), F32),
        grid=(DN_HEADS, nt),
        in_specs=[head(0), head(1), head(2), gates, row, row],
        out_specs=head(0),
        scratch_shapes=[pltpu.VMEM((DN_D, DN_D), F32)],
        compiler_params=_params("parallel", "arbitrary"),
        name="dn_bwd",
    )(qkv, qkv, qkv, proj, alog_row, dtb_row)

    tile, head, gates = common(False)
    return pl.pallas_call(
        functools.partial(_dn_kernel, tl=tl, nt=nt, groups=groups, reverse=False),
        out_shape=jax.ShapeDtypeStruct((t, D_HALF), BF16),
        grid=(DN_HEADS, nt),
        in_specs=[head(0), head(1), head(2), gates, row, row,
                  pl.BlockSpec((tl, DN_D), lambda h, j: (j, z_blk + h)),
                  head(0), row],
        out_specs=head(0),
        scratch_shapes=[pltpu.VMEM((DN_D, DN_D), F32)],
        compiler_params=_params("parallel", "arbitrary"),
        name="dn_fwd",
    )(qkv, qkv, qkv, proj, alog_row, dtb_row, proj, ob, norm_g)


def _gate_row(f_vals, b_vals):
    row = jnp.zeros((LANES,), F32)
    row = row.at[2 * DN_HEADS:3 * DN_HEADS].set(f_vals.astype(F32))
    row = row.at[3 * DN_HEADS:4 * DN_HEADS].set(b_vals.astype(F32))
    return row.reshape(1, LANES)


def _trunk(x, c, p, groups):
    t, d = x.shape
    n_seq = c.shape[0]
    c_pad = jnp.zeros((SEQ_PAD, d), F32).at[:n_seq].set(c)
    mods = _ada(c_pad, p['w_ada'], p['b_ada'])
    mods = mods.reshape(DEPTH, SEQ_PAD, N_MOD, 1, d).transpose(0, 2, 1, 3, 4)
    fin = _ada(c_pad, p['w_ada_final'][None], p['b_ada_final'][None])
    fin = fin.reshape(SEQ_PAD, 2, 1, d).transpose(1, 0, 2, 3)

    max_len = max(ln for _, ln in groups)
    cos, sin = _rope_tables(max_len, min(TL_RET, groups[0][1]))

    row = lambda a: a.reshape(1, -1)
    for layer in range(DEPTH):
        sh1, sc1, g1, sh2, sc2, g2, sh3, sc3, g3 = [mods[layer, jm] for jm in range(N_MOD)]
        x = _ffn(x, row(p['norm_ffn1'][layer]), sh1, sc1, g1,
                 p['w_ffn1_in'][layer], p['w_ffn1_out'][layer], groups)
        idx = layer // 2
        gain = row(p['norm_mix'][layer])
        if layer % 2 == 0:
            proj = _proj(x, gain, sh2, sc2, p['w_in_even'][idx], TN_PROJ_EVEN, groups)
            y = _even_mix(proj, p['pool_w'][idx], row(p['pool_scale'][idx]), row(p['sgu_norm'][idx]),
                          p['sgu_w'][idx], p['sgu_b'][idx][..., None], groups)
            x = _outproj(x, y, y, 0, 1, g2, p['w_out_even'][idx], groups)
        else:
            proj = _proj(x, gain, sh2, sc2, p['w_in_odd'][idx], TN_PROJ_ODD, groups)
            yc = _retention(proj, cos, sin, p['ret_decay_f'][idx], p['ret_decay_b'][idx],
                            row(p['ret_norm'][idx]), groups)
            qkv = _dn_prep(proj, p['dn_conv'][idx], groups)
            yd = _deltanet(proj, qkv,
                           _gate_row(p['dn_a_log_f'][idx], p['dn_a_log_b'][idx]),
                           _gate_row(p['dn_dt_bias_f'][idx], p['dn_dt_bias_b'][idx]),
                           row(p['dn_norm'][idx]), groups)
            x = _outproj(x, yc, yd, 0, 0, g2, p['w_out_odd'][idx], groups)
        x = _ffn(x, row(p['norm_ffn2'][layer]), sh3, sc3, g3,
                 p['w_ffn2_in'][layer], p['w_ffn2_out'][layer], groups)
    return _final(x, row(p['norm_final']), fin[0], fin[1], groups)


def _prepare(p):
    q = dict(p)
    for name in ('w_ffn1_in', 'w_ffn1_out', 'w_ffn2_in', 'w_ffn2_out', 'w_in_even', 'w_out_even',
                 'pool_w', 'sgu_w', 'w_out_odd'):
        q[name] = p[name].astype(BF16)
    w = p['w_in_odd'].astype(BF16)
    q['w_in_odd'] = jnp.pad(w, ((0, 0), (0, 0), (0, P_ODD_PAD - P_ODD)))
    return q


def kernel(x_prompt, x_sample, c_prompt, c_sample, w_ada, b_ada, norm_ffn1, w_ffn1_in, w_ffn1_out, norm_mix, norm_ffn2, w_ffn2_in, w_ffn2_out, w_in_even, w_out_even, pool_w, pool_scale, sgu_norm, sgu_w, sgu_b, w_in_odd, w_out_odd, ret_decay_f, ret_decay_b, ret_norm, dn_conv, dn_a_log_f, dn_a_log_b, dn_dt_bias_f, dn_dt_bias_b, dn_norm, norm_final, w_ada_final, b_ada_final):
    p = _prepare({
        'w_ada': w_ada, 'b_ada': b_ada, 'norm_ffn1': norm_ffn1, 'w_ffn1_in': w_ffn1_in,
        'w_ffn1_out': w_ffn1_out, 'norm_mix': norm_mix, 'norm_ffn2': norm_ffn2,
        'w_ffn2_in': w_ffn2_in, 'w_ffn2_out': w_ffn2_out, 'w_in_even': w_in_even,
        'w_out_even': w_out_even, 'pool_w': pool_w, 'pool_scale': pool_scale,
        'sgu_norm': sgu_norm, 'sgu_w': sgu_w, 'sgu_b': sgu_b, 'w_in_odd': w_in_odd,
        'w_out_odd': w_out_odd, 'ret_decay_f': ret_decay_f, 'ret_decay_b': ret_decay_b,
        'ret_norm': ret_norm, 'dn_conv': dn_conv, 'dn_a_log_f': dn_a_log_f,
        'dn_a_log_b': dn_a_log_b, 'dn_dt_bias_f': dn_dt_bias_f, 'dn_dt_bias_b': dn_dt_bias_b,
        'dn_norm': dn_norm, 'norm_final': norm_final, 'w_ada_final': w_ada_final,
        'b_ada_final': b_ada_final,
    })
    bp, lp, d = x_prompt.shape
    bs, ls, _ = x_sample.shape
    groups = ((bp, lp), (bs, ls))
    x = jnp.concatenate([x_prompt.reshape(bp * lp, d), x_sample.reshape(bs * ls, d)], axis=0)
    c = jnp.concatenate([c_prompt, c_sample], axis=0)
    y = _trunk(x, c, p, groups)
    return (y[:bp * lp].reshape(bp, lp, d), y[bp * lp:].reshape(bs, ls, d))
```

```python
import functools

import jax
import jax.numpy as jnp
import numpy as np
from jax import lax
from jax.experimental import pallas as pl
from jax.experimental.pallas import tpu as pltpu

F32 = jnp.float32
BF16 = jnp.bfloat16

D_MODEL = 2048
DEPTH = 4
D_HALF = D_MODEL // 2
POOL_WINDOWS = (2, 4, 8, 16)
POOL_GROUP = D_HALF // len(POOL_WINDOWS)
SGU_CHUNK = 128
SGU_HEAD = 128
SGU_GROUPS = D_HALF // SGU_HEAD
RET_HEADS = 4
RET_D = D_HALF // RET_HEADS
RET_CHUNK = 128
ROPE_BASE = 10000.0
DN_HEADS = 8
DN_D = D_HALF // DN_HEADS
DN_CONV = 4
DN_CHUNK = 64
D_FF = 5632
N_MOD = 9
EPS = 1e-6
P_EVEN = 3 * D_HALF
P_RET = 4 * D_HALF
P_ODD = P_RET + 4 * D_HALF + 4 * DN_HEADS

LANES = 128
SUBLANES = 8
HALO = SUBLANES
VMEM_LIMIT = 56 * 1024 * 1024

TM_FFN = 512
TF_FFN = 512
TM_PROJ = 1024
TN_PROJ_EVEN = 512
TN_PROJ_ODD = 640
TM_OUT = 512
TL_EVEN = 256
TL_RET = 512
TL_DN = 128
TL_PREP = 256
TM_FINAL = 512
TN_ADA = 1024
SEQ_PAD = 16

P_ODD_PAD = P_ODD + (-P_ODD) % TN_PROJ_ODD


def _seq_info(row0, groups):
    seq = start = length = None
    t0 = s0 = 0
    for gi, (nb, ln) in enumerate(groups):
        rel = row0 - t0
        q = rel // ln
        if gi == 0:
            seq, start, length = q, q * ln, ln
        else:
            here = row0 >= t0
            seq = jnp.where(here, s0 + q, seq)
            start = jnp.where(here, t0 + q * ln, start)
            length = jnp.where(here, ln, length)
        t0 += nb * ln
        s0 += nb
    return seq, start, length


def _params(*sem):
    return pltpu.CompilerParams(dimension_semantics=sem, vmem_limit_bytes=VMEM_LIMIT)


def _modulated(x, gain, shift, scale):
    ms = jnp.mean(x * x, axis=-1, keepdims=True)
    y = x * lax.rsqrt(ms + EPS)
    return (y * gain) * (1.0 + scale) + shift


def _dot(a, b):
    return jnp.dot(a, b, preferred_element_type=F32)


def _dot_nt(a, b):
    return lax.dot_general(a, b, (((1,), (1,)), ((), ())), preferred_element_type=F32)


def _dot_tn(a, b):
    return lax.dot_general(a, b, (((0,), (0,)), ((), ())), preferred_element_type=F32)


def _split_bf16(a):
    hi = a.astype(BF16)
    lo = (a - hi.astype(F32)).astype(BF16)
    return hi, lo


def _dot_hi(a, b):
    ah, al = _split_bf16(a)
    bh, bl = _split_bf16(b)
    return _dot(ah, bh) + (_dot(ah, bl) + _dot(al, bh))


def _ada_kernel(c_ref, w_ref, b_ref, o_ref):
    c = c_ref[...]
    act = jax.nn.silu(c).astype(BF16)
    o_ref[...] = _dot(act, w_ref[...].astype(BF16)) + b_ref[...]


def _ada(c_pad, w, b):
    ly, d, n = w.shape
    s = c_pad.shape[0]
    tn = min(TN_ADA, n)
    return pl.pallas_call(
        _ada_kernel,
        out_shape=jax.ShapeDtypeStruct((ly, s, n), F32),
        grid=(ly, n // tn),
        in_specs=[
            pl.BlockSpec((s, d), lambda l, j: (0, 0)),
            pl.BlockSpec((None, d, tn), lambda l, j: (l, 0, j)),
            pl.BlockSpec((None, 1, tn), lambda l, j: (l, 0, j)),
        ],
        out_specs=pl.BlockSpec((None, s, tn), lambda l, j: (l, 0, j)),
        compiler_params=_params("parallel", "parallel"),
        name="ada_rows",
    )(c_pad, w, b.reshape(ly, 1, n))


def _ffn_kernel(x_ref, gain_ref, sh_ref, sc_ref, gt_ref, wg_ref, wu_ref, wo_ref, o_ref,
                h_ref, acc_ref, *, nf):
    f = pl.program_id(1)

    @pl.when(f == 0)
    def _():
        h = _modulated(x_ref[...], gain_ref[...], sh_ref[...], sc_ref[...])
        h_ref[...] = h.astype(BF16)
        acc_ref[...] = jnp.zeros_like(acc_ref)

    h = h_ref[...]
    g = _dot(h, wg_ref[...])
    u = _dot(h, wu_ref[...])
    a = (jax.nn.silu(g) * u).astype(BF16)
    acc_ref[...] += _dot(a, wo_ref[...])

    @pl.when(f == nf - 1)
    def _():
        o_ref[...] = x_ref[...] + (0.5 * gt_ref[...]) * acc_ref[...]


def _row_spec(groups, tm, d):
    return pl.BlockSpec((None, 1, d), lambda i, j: (_seq_info(i * tm, groups)[0], 0, 0))


def _ffn(x, gain, shift, scale, gate, w_in, w_out, groups):
    t, d = x.shape
    ff = w_out.shape[0]
    tm = min(TM_FFN, groups[0][1])
    tf = min(TF_FFN, ff)
    nf = ff // tf
    row = _row_spec(groups, tm, d)
    return pl.pallas_call(
        functools.partial(_ffn_kernel, nf=nf),
        out_shape=jax.ShapeDtypeStruct((t, d), F32),
        grid=(t // tm, nf),
        in_specs=[
            pl.BlockSpec((tm, d), lambda i, f: (i, 0)),
            pl.BlockSpec((1, d), lambda i, f: (0, 0)),
            row, row, row,
            pl.BlockSpec((d, tf), lambda i, f: (0, f)),
            pl.BlockSpec((d, tf), lambda i, f: (0, nf + f)),
            pl.BlockSpec((tf, d), lambda i, f: (f, 0)),
        ],
        out_specs=pl.BlockSpec((tm, d), lambda i, f: (i, 0)),
        scratch_shapes=[pltpu.VMEM((tm, d), BF16), pltpu.VMEM((tm, d), F32)],
        compiler_params=_params("parallel", "arbitrary"),
        name="ffn",
    )(x, gain, shift, scale, gate, w_in, w_in, w_out)


def _proj_kernel(x_ref, gain_ref, sh_ref, sc_ref, w_ref, o_ref, h_ref):
    @pl.when(pl.program_id(1) == 0)
    def _():
        h = _modulated(x_ref[...], gain_ref[...], sh_ref[...], sc_ref[...])
        h_ref[...] = h.astype(BF16)

    o_ref[...] = _dot(h_ref[...], w_ref[...])


def _proj(x, gain, shift, scale, w, tn, groups):
    t, d = x.shape
    n = w.shape[1]
    tm = min(TM_PROJ, groups[0][1])
    row = _row_spec(groups, tm, d)
    return pl.pallas_call(
        _proj_kernel,
        out_shape=jax.ShapeDtypeStruct((t, n), F32),
        grid=(t // tm, n // tn),
        in_specs=[
            pl.BlockSpec((tm, d), lambda i, j: (i, 0)),
            pl.BlockSpec((1, d), lambda i, j: (0, 0)),
            row, row,
            pl.BlockSpec((d, tn), lambda i, j: (0, j)),
        ],
        out_specs=pl.BlockSpec((tm, tn), lambda i, j: (i, j)),
        scratch_shapes=[pltpu.VMEM((tm, d), BF16)],
        compiler_params=_params("parallel", "arbitrary"),
        name="mix_proj",
    )(x, gain, shift, scale, w)


def _outproj_kernel(x_ref, ya_ref, yb_ref, gt_ref, wa_ref, wb_ref, o_ref):
    y = _dot(ya_ref[...], wa_ref[...]) + _dot(yb_ref[...], wb_ref[...])
    o_ref[...] = x_ref[...] + gt_ref[...] * y


def _outproj(x, ya, yb, ca, cb, gate, w, groups):
    t, d = x.shape
    dh = d // 2
    tm = min(TM_OUT, groups[0][1])
    row = _row_spec(groups, tm, d)
    return pl.pallas_call(
        _outproj_kernel,
        out_shape=jax.ShapeDtypeStruct((t, d), F32),
        grid=(t // tm, 1),
        in_specs=[
            pl.BlockSpec((tm, d), lambda i, j: (i, 0)),
            pl.BlockSpec((tm, dh), lambda i, j: (i, ca)),
            pl.BlockSpec((tm, dh), lambda i, j: (i, cb)),
            row,
            pl.BlockSpec((dh, d), lambda i, j: (0, 0)),
            pl.BlockSpec((dh, d), lambda i, j: (1, 0)),
        ],
        out_specs=pl.BlockSpec((tm, d), lambda i, j: (i, 0)),
        compiler_params=_params("parallel", "arbitrary"),
        name="mix_out",
    )(x, ya, yb, gate, w, w)


def _final_kernel(x_ref, gain_ref, sh_ref, sc_ref, o_ref):
    o_ref[...] = _modulated(x_ref[...], gain_ref[...], sh_ref[...], sc_ref[...])


def _final(x, gain, shift, scale, groups):
    t, d = x.shape
    tm = min(TM_FINAL, groups[0][1])
    row = _row_spec(groups, tm, d)
    return pl.pallas_call(
        _final_kernel,
        out_shape=jax.ShapeDtypeStruct((t, d), F32),
        grid=(t // tm, 1),
        in_specs=[
            pl.BlockSpec((tm, d), lambda i, j: (i, 0)),
            pl.BlockSpec((1, d), lambda i, j: (0, 0)),
            row, row,
        ],
        out_specs=pl.BlockSpec((tm, d), lambda i, j: (i, 0)),
        compiler_params=_params("parallel", "arbitrary"),
        name="final_mod",
    )(x, gain, shift, scale)


def _halo_specs(tl, width, nrows, col_of):
    per = tl // HALO
    last = nrows // HALO - 1
    prev = pl.BlockSpec((HALO, width), lambda i, *r: (jnp.maximum(i * per - 1, 0), col_of(i, *r)))
    nxt = pl.BlockSpec((HALO, width), lambda i, *r: (jnp.minimum((i + 1) * per, last), col_of(i, *r)))
    return prev, nxt


def _fill_ext(ext_ref, x_ref, prev_ref, next_ref, first, last, tl):
    ext_ref[HALO:HALO + tl, :] = x_ref[...]
    ext_ref[0:HALO, :] = jnp.where(first, 0.0, prev_ref[...])
    ext_ref[HALO + tl:2 * HALO + tl, :] = jnp.where(last, 0.0, next_ref[...])


def _even_kernel(xa_ref, prev_ref, next_ref, u_ref, v_ref, pw_ref, ps_ref, ng_ref, sw_ref, sb_ref,
                 o_ref, ext_ref, vn_ref, *, tl, groups):
    row0 = pl.program_id(0) * tl
    _, sstart, slen = _seq_info(row0, groups)
    pos0 = row0 - sstart
    _fill_ext(ext_ref, xa_ref, prev_ref, next_ref, pos0 == 0, pos0 + tl == slen, tl)

    t = pos0 + lax.broadcasted_iota(jnp.int32, (tl, 1), 0)
    for gi, w in enumerate(POOL_WINDOWS):
        c0 = gi * POOL_GROUP
        cols = slice(c0, c0 + POOL_GROUP)
        base = HALO - w // 2
        s = ext_ref[base:base + tl, cols]
        for dlt in range(1, w):
            s = s + ext_ref[base + dlt:base + dlt + tl, cols]
        lo = jnp.clip(t - w // 2, 0, slen)
        hi = jnp.clip(t + (w - w // 2), 0, slen)
        cnt = (hi - lo).astype(F32)
        pooled = (s / cnt - xa_ref[:, cols]).astype(BF16)
        ya = _dot(pooled, pw_ref[gi]) * ps_ref[:, cols]
        o_ref[:, cols] = ya.astype(BF16)

    v = jax.nn.gelu(v_ref[...])
    vms = jnp.mean(v * v, axis=-1, keepdims=True)
    vn_ref[...] = ((v * lax.rsqrt(vms + EPS)) * ng_ref[...]).astype(BF16)
    for n in range(tl // SGU_CHUNK):
        rows = slice(n * SGU_CHUNK, (n + 1) * SGU_CHUNK)
        for g in range(SGU_GROUPS):
            cols = slice(g * SGU_HEAD, (g + 1) * SGU_HEAD)
            mixed = _dot(sw_ref[g], vn_ref[rows, cols]) + sb_ref[g]
            u = jax.nn.gelu(u_ref[rows, cols])
            o_ref[rows, D_HALF + g * SGU_HEAD:D_HALF + (g + 1) * SGU_HEAD] = (u * mixed).astype(BF16)


def _even_mix(proj, pool_w, pool_scale, sgu_norm, sgu_w, sgu_b, groups):
    t = proj.shape[0]
    tl = min(TL_EVEN, groups[0][1])
    prev, nxt = _halo_specs(tl, D_HALF, t, lambda i: 0)
    const2 = lambda i: (0, 0)
    const3 = lambda i: (0, 0, 0)
    return pl.pallas_call(
        functools.partial(_even_kernel, tl=tl, groups=groups),
        out_shape=jax.ShapeDtypeStruct((t, 2 * D_HALF), BF16),
        grid=(t // tl,),
        in_specs=[
            pl.BlockSpec((tl, D_HALF), lambda i: (i, 0)),
            prev, nxt,
            pl.BlockSpec((tl, D_HALF), lambda i: (i, 1)),
            pl.BlockSpec((tl, D_HALF), lambda i: (i, 2)),
            pl.BlockSpec(pool_w.shape, const3),
            pl.BlockSpec((1, D_HALF), const2),
            pl.BlockSpec((1, D_HALF), const2),
            pl.BlockSpec(sgu_w.shape, const3),
            pl.BlockSpec(sgu_b.shape, const3),
        ],
        out_specs=pl.BlockSpec((tl, 2 * D_HALF), lambda i: (i, 0)),
        scratch_shapes=[pltpu.VMEM((tl + 2 * HALO, D_HALF), F32), pltpu.VMEM((tl, D_HALF), BF16)],
        compiler_params=_params("parallel"),
        name="even_mix",
    )(proj, proj, proj, proj, proj, pool_w, pool_scale, sgu_norm, sgu_w, sgu_b)


def _rope_kernel(inv_ref, cos_ref, sin_ref, *, tl):
    pos = (pl.program_id(0) * tl + lax.broadcasted_iota(jnp.int32, (tl, 1), 0)).astype(F32)
    ang = pos * inv_ref[...]
    cos_ref[...] = jnp.cos(ang)
    sin_ref[...] = jnp.sin(ang)


def _rope_tables(max_len, tl):
    half = RET_D // 2
    inv = (1.0 / (ROPE_BASE ** jnp.linspace(0.0, 1.0, half, dtype=F32))).reshape(1, half)
    shp = jax.ShapeDtypeStruct((max_len, half), F32)
    return pl.pallas_call(
        functools.partial(_rope_kernel, tl=tl),
        out_shape=(shp, shp),
        grid=(max_len // tl,),
        in_specs=[pl.BlockSpec((1, half), lambda i: (0, 0))],
        out_specs=(pl.BlockSpec((tl, half), lambda i: (i, 0)), pl.BlockSpec((tl, half), lambda i: (i, 0))),
        compiler_params=_params("parallel"),
        name="rope_table",
    )(inv)


def _rope_apply(x, cos, sin):
    half = RET_D // 2
    x1 = x[:, :half]
    x2 = x[:, half:]
    return jnp.concatenate([x1 * cos - x2 * sin, x1 * sin + x2 * cos], axis=-1)


def _ret_kernel(*refs, tl, nt, groups, reverse):
    if reverse:
        q_ref, k_ref, v_ref, cos_ref, sin_ref, dec_ref, o_ref, s_ref = refs
    else:
        (q_ref, k_ref, v_ref, cos_ref, sin_ref, dec_ref, decb_ref, g_ref, ob_ref, ng_ref,
         o_ref, s_ref) = refs
    j = pl.program_id(1)
    it = nt - 1 - j if reverse else j
    row0 = it * tl
    _, sstart, slen = _seq_info(row0, groups)
    pos0 = row0 - sstart
    reset = (pos0 + tl == slen) if reverse else (pos0 == 0)

    @pl.when(reset)
    def _():
        s_ref[...] = jnp.zeros_like(s_ref)

    c = RET_CHUNK
    lg = jnp.log1p(-jnp.exp2(-dec_ref[...]))
    idx = lax.broadcasted_iota(jnp.int32, (c, 1), 0).astype(F32)
    if reverse:
        q_dec = jnp.exp(lg * (c - idx))
        k_dec = jnp.exp(lg * idx)
    else:
        q_dec = jnp.exp(lg * (idx + 1.0))
        k_dec = jnp.exp(lg * (c - 1.0 - idx))
        lgb = jnp.log1p(-jnp.exp2(-decb_ref[...]))
        ri = lax.broadcasted_iota(jnp.int32, (c, c), 0)
        ci = lax.broadcasted_iota(jnp.int32, (c, c), 1)
        rel = (ri - ci).astype(F32)
        dmat = (jnp.where(rel >= 0, jnp.exp(lg * jnp.maximum(rel, 0.0)), 0.0)
                + jnp.where(rel <= 0, jnp.exp(lgb * jnp.maximum(-rel, 0.0)), 0.0))
    chunk_dec = jnp.exp(lg * float(c))

    nc = tl // c
    order = range(nc - 1, -1, -1) if reverse else range(nc)
    for ch in order:
        rows = slice(ch * c, (ch + 1) * c)
        cos = cos_ref[rows, :]
        sin = sin_ref[rows, :]
        q = _rope_apply(q_ref[rows, :], cos, sin)
        k = _rope_apply(k_ref[rows, :], cos, sin) * (RET_D ** -0.5)
        v = v_ref[rows, :].astype(BF16)
        s = s_ref[...]
        inter = _dot((q * q_dec).astype(BF16), s.astype(BF16))
        s_ref[...] = s * chunk_dec + _dot_tn((k * k_dec).astype(BF16), v)
        if reverse:
            o_ref[rows, :] = inter
        else:
            scores = _dot_nt(q.astype(BF16), k.astype(BF16)) * dmat
            o = _dot(scores.astype(BF16), v) + inter + ob_ref[rows, :]
            mu = jnp.mean(o, axis=-1, keepdims=True)
            var = jnp.mean(jnp.square(o - mu), axis=-1, keepdims=True)
            on = ((o - mu) * lax.rsqrt(var + EPS)) * ng_ref[...]
            o_ref[rows, :] = (jax.nn.silu(g_ref[rows, :]) * on).astype(BF16)


def _retention(proj, cos, sin, decay_f, decay_b, norm_g, groups):
    t = proj.shape[0]
    tl = min(TL_RET, groups[0][1])
    nt = t // tl
    hb = D_HALF // RET_D
    half = RET_D // 2

    def tile(j, reverse):
        return nt - 1 - j if reverse else j

    def specs(reverse):
        def pos_block(h, j):
            row0 = tile(j, reverse) * tl
            return ((row0 - _seq_info(row0, groups)[1]) // tl, 0)
        sec = lambda s: pl.BlockSpec((tl, RET_D), lambda h, j: (tile(j, reverse), s * hb + h))
        tab = pl.BlockSpec((tl, half), pos_block)
        dec = pl.BlockSpec((None, 1, 1), lambda h, j: (h, 0, 0))
        return sec, tab, dec

    sec, tab, dec = specs(True)
    ob = pl.pallas_call(
        functools.partial(_ret_kernel, tl=tl, nt=nt, groups=groups, reverse=True),
        out_shape=jax.ShapeDtypeStruct((t, D_HALF), F32),
        grid=(RET_HEADS, nt),
        in_specs=[sec(0), sec(1), sec(2), tab, tab, dec],
        out_specs=pl.BlockSpec((tl, RET_D), lambda h, j: (nt - 1 - j, h)),
        scratch_shapes=[pltpu.VMEM((RET_D, RET_D), F32)],
        compiler_params=_params("parallel", "arbitrary"),
        name="ret_bwd",
    )(proj, proj, proj, cos, sin, decay_b.reshape(RET_HEADS, 1, 1))

    sec, tab, dec = specs(False)
    return pl.pallas_call(
        functools.partial(_ret_kernel, tl=tl, nt=nt, groups=groups, reverse=False),
        out_shape=jax.ShapeDtypeStruct((t, D_HALF), BF16),
        grid=(RET_HEADS, nt),
        in_specs=[sec(0), sec(1), sec(2), tab, tab, dec, dec, sec(3),
                  pl.BlockSpec((tl, RET_D), lambda h, j: (j, h)),
                  pl.BlockSpec((1, RET_D), lambda h, j: (0, h))],
        out_specs=pl.BlockSpec((tl, RET_D), lambda h, j: (j, h)),
        scratch_shapes=[pltpu.VMEM((RET_D, RET_D), F32)],
        compiler_params=_params("parallel", "arbitrary"),
        name="ret_fwd",
    )(proj, proj, proj, cos, sin, decay_f.reshape(RET_HEADS, 1, 1), decay_b.reshape(RET_HEADS, 1, 1),
      proj, ob, norm_g)


def _dnprep_kernel(x_ref, prev_ref, next_ref, w_ref, o_ref, ext_ref, *, tl, groups):
    row0 = pl.program_id(0) * tl
    part = pl.program_id(1)
    _, sstart, slen = _seq_info(row0, groups)
    pos0 = row0 - sstart
    _fill_ext(ext_ref, x_ref, prev_ref, next_ref, pos0 == 0, pos0 + tl == slen, tl)
    left = DN_CONV // 2
    q_scale = jnp.where(part == 0, DN_D ** -0.5, 1.0).astype(F32)
    for h in range(DN_HEADS):
        cols = slice(h * DN_D, (h + 1) * DN_D)
        conv = ext_ref[HALO - left:HALO - left + tl, cols] * w_ref[0:1, cols]
        for tap in range(1, DN_CONV):
            r0 = HALO - left + tap
            conv = conv + ext_ref[r0:r0 + tl, cols] * w_ref[tap:tap + 1, cols]
        y = jax.nn.silu(conv)
        inv_norm = lax.rsqrt(jnp.sum(y * y, axis=-1, keepdims=True) + EPS)
        o_ref[:, cols] = y * jnp.where(part < 2, inv_norm * q_scale, 1.0)


def _dn_prep(proj, conv_w, groups):
    t = proj.shape[0]
    tl = min(TL_PREP, groups[0][1])
    base = P_RET // D_HALF
    prev, nxt = _halo_specs(tl, D_HALF, t, lambda i, part: base + part)
    return pl.pallas_call(
        functools.partial(_dnprep_kernel, tl=tl, groups=groups),
        out_shape=jax.ShapeDtypeStruct((t, 3 * D_HALF), F32),
        grid=(t // tl, 3),
        in_specs=[pl.BlockSpec((tl, D_HALF), lambda i, part: (i, base + part)), prev, nxt,
                  pl.BlockSpec((DN_CONV, D_HALF), lambda i, part: (0, part))],
        out_specs=pl.BlockSpec((tl, D_HALF), lambda i, part: (i, part)),
        scratch_shapes=[pltpu.VMEM((tl + 2 * HALO, D_HALF), F32)],
        compiler_params=_params("parallel", "parallel"),
        name="dn_prep",
    )(proj, proj, proj, conv_w)


def _dot_hi_each(lhs, rhs):
    ls = [_split_bf16(a) for a in lhs]
    rs = [_split_bf16(b) for b in rhs]
    main = [_dot(ah, bh) for (ah, _), (bh, _) in zip(ls, rs)]
    cross = [_dot(ah, bl) + _dot(al, bh) for (ah, al), (bh, bl) in zip(ls, rs)]
    return [m + x for m, x in zip(main, cross)]


def _unit_triangular_inverses(mats, eye, ri, ci):
    size = SUBLANES
    same = (ri // size) == (ci // size)
    ps = [jnp.where(same, a, 0.0) for a in mats]
    invs = [eye - d for d in ps]
    n = 2
    while n < size:
        ps = _dot_hi_each(ps, ps)
        invs = [inv + x for inv, x in zip(invs, _dot_hi_each(invs, ps))]
        n *= 2
    while size < DN_CHUNK:
        size *= 2
        merged = (ri // size) == (ci // size)
        es = [jnp.where(merged & ~same, a, 0.0) for a in mats]
        invs = [inv - x for inv, x in zip(invs, _dot_hi_each(_dot_hi_each(invs, es), invs))]
        same = merged
    return invs


def _dn_kernel(*refs, tl, nt, groups, reverse):
    if reverse:
        q_ref, k_ref, v_ref, gates_ref, alog_ref, dtb_ref, o_ref, s_ref = refs
    else:
        (q_ref, k_ref, v_ref, gates_ref, alog_ref, dtb_ref, z_ref, ob_ref, ng_ref,
         o_ref, s_ref) = refs
    j = pl.program_id(0)
    it = nt - 1 - j if reverse else j
    row0 = it * tl
    _, sstart, slen = _seq_info(row0, groups)
    pos0 = row0 - sstart
    reset = (pos0 + tl == slen) if reverse else (pos0 == 0)

    @pl.when(reset)
    def _():
        s_ref[...] = jnp.zeros_like(s_ref)

    gates = gates_ref[...]
    beta_all = jax.nn.sigmoid(gates)
    la_all = -jnp.exp(alog_ref[...]) * jax.nn.softplus(gates + dtb_ref[...])
    cb0 = DN_HEADS if reverse else 0
    ca0 = cb0 + 2 * DN_HEADS

    c = DN_CHUNK
    ri = lax.broadcasted_iota(jnp.int32, (c, c), 0)
    ci = lax.broadcasted_iota(jnp.int32, (c, c), 1)
    eye = (ri == ci).astype(F32)
    incl = (ri <= ci) if reverse else (ri >= ci)
    strict = (ri < ci) if reverse else (ri > ci)
    incl_t = (ri >= ci) if reverse else (ri <= ci)
    last = 0 if reverse else c - 1

    nc = tl // c
    order = list(range(nc - 1, -1, -1) if reverse else range(nc))
    heads = range(DN_HEADS)
    units = [(ch, h) for ch in order for h in heads]

    def rows(ch):
        return slice(ch * c, (ch + 1) * c)

    def cols(h):
        return slice(h * DN_D, (h + 1) * DN_D)

    q = [q_ref[rows(ch), cols(h)] for ch, h in units]
    k = [k_ref[rows(ch), cols(h)] for ch, h in units]
    beta = [beta_all[rows(ch), cb0 + h:cb0 + h + 1] for ch, h in units]
    la = [la_all[rows(ch), ca0 + h:ca0 + h + 1] for ch, h in units]
    la_row = [jnp.sum(eye * x, axis=0, keepdims=True) for x in la]
    g_col = [jnp.sum(jnp.where(incl, x, 0.0), axis=1, keepdims=True) for x in la_row]
    g_row = [jnp.sum(jnp.where(incl_t, x, 0.0), axis=0, keepdims=True) for x in la]
    gam = [jnp.where(incl, jnp.exp(jnp.where(incl, gc - gr, 0.0)), 0.0) for gc, gr in zip(g_col, g_row)]
    eg = [jnp.exp(gc) for gc in g_col]
    g_last = [gc[last:last + 1, :] for gc in g_col]
    kb = [x * b for x, b in zip(k, beta)]
    kbf = [x.astype(BF16) for x in k]
    kq = [_dot_nt(jnp.concatenate([x, y], axis=0).astype(BF16), z) for x, y, z in zip(kb, q, kbf)]
    a = [jnp.where(strict, x[:c] * gm, 0.0) for x, gm in zip(kq, gam)]
    attn = [(x[c:] * gm).astype(BF16) for x, gm in zip(kq, gam)]
    tinv = _unit_triangular_inverses(a, eye, ri, ci)
    rhs = [jnp.concatenate([v_ref[rows(ch), cols(h)] * b, x * e], axis=1)
           for (ch, h), b, x, e in zip(units, beta, kb, eg)]
    uw = _dot_hi_each(tinv, rhs)
    qe = [(x * e).astype(BF16) for x, e in zip(q, eg)]
    kt = [(x * jnp.exp(gl - gc)).astype(BF16) for x, gl, gc in zip(k, g_last, g_col)]
    dec = [jnp.exp(gl) for gl in g_last]

    for ci_, ch in enumerate(order):
        idx = [ci_ * DN_HEADS + h for h in heads]
        s = [s_ref[h] for h in heads]
        sb = [x.astype(BF16) for x in s]
        ws = [_dot(jnp.concatenate([uw[i][:, DN_D:].astype(BF16), qe[i]], axis=0), sb[h])
              for h, i in zip(heads, idx)]
        vnb = [(uw[i][:, :DN_D] - x[:c]).astype(BF16) for i, x in zip(idx, ws)]
        o = [x[c:] + _dot(attn[i], vn) for i, x, vn in zip(idx, ws, vnb)]
        for h, i in zip(heads, idx):
            s_ref[h] = s[h] * dec[i] + _dot_tn(kt[i], vnb[h])
        for h in heads:
            if reverse:
                o_ref[rows(ch), cols(h)] = o[h]
            else:
                oo = o[h] + ob_ref[rows(ch), cols(h)]
                on = (oo * lax.rsqrt(jnp.mean(oo * oo, axis=-1, keepdims=True) + EPS)) * ng_ref[...]
                o_ref[rows(ch), cols(h)] = (on * jax.nn.silu(z_ref[rows(ch), cols(h)])).astype(BF16)


def _deltanet(proj, qkv, alog_row, dtb_row, norm_g, groups):
    t = proj.shape[0]
    tl = min(TL_DN, groups[0][1])
    nt = t // tl
    z_blk = (P_RET + 3 * D_HALF) // D_HALF
    gate_blk = (P_RET + 4 * D_HALF) // LANES
    row = pl.BlockSpec((1, LANES), lambda j: (0, 0))
    state = pltpu.VMEM((DN_HEADS, DN_D, DN_D), F32)

    def common(reverse):
        tile = (lambda j: nt - 1 - j) if reverse else (lambda j: j)
        sec = lambda s: pl.BlockSpec((tl, D_HALF), lambda j: (tile(j), s))
        gates = pl.BlockSpec((tl, LANES), lambda j: (tile(j), gate_blk))
        return sec, gates

    sec, gates = common(True)
    ob = pl.pallas_call(
        functools.partial(_dn_kernel, tl=tl, nt=nt, groups=groups, reverse=True),
        out_shape=jax.ShapeDtypeStruct((t, D_HALF), F32),
        grid=(nt,),
        in_specs=[sec(0), sec(1), sec(2), gates, row, row],
        out_specs=sec(0),
        scratch_shapes=[state],
        compiler_params=_params("arbitrary"),
        name="dn_bwd",
    )(qkv, qkv, qkv, proj, alog_row, dtb_row)

    sec, gates = common(False)
    return pl.pallas_call(
        functools.partial(_dn_kernel, tl=tl, nt=nt, groups=groups, reverse=False),
        out_shape=jax.ShapeDtypeStruct((t, D_HALF), BF16),
        grid=(nt,),
        in_specs=[sec(0), sec(1), sec(2), gates, row, row, sec(z_blk), sec(0), row],
        out_specs=sec(0),
        scratch_shapes=[state],
        compiler_params=_params("arbitrary"),
        name="dn_fwd",
    )(qkv, qkv, qkv, proj, alog_row, dtb_row, proj, ob, norm_g)


def _gate_row(f_vals, b_vals):
    row = jnp.zeros((LANES,), F32)
    row = row.at[2 * DN_HEADS:3 * DN_HEADS].set(f_vals.astype(F32))
    row = row.at[3 * DN_HEADS:4 * DN_HEADS].set(b_vals.astype(F32))
    return row.reshape(1, LANES)


def _trunk(x, c, p, groups):
    t, d = x.shape
    n_seq = c.shape[0]
    c_pad = jnp.zeros((SEQ_PAD, d), F32).at[:n_seq].set(c)
    mods = _ada(c_pad, p['w_ada'], p['b_ada'])
    mods = mods.reshape(DEPTH, SEQ_PAD, N_MOD, 1, d).transpose(0, 2, 1, 3, 4)
    fin = _ada(c_pad, p['w_ada_final'][None], p['b_ada_final'][None])
    fin = fin.reshape(SEQ_PAD, 2, 1, d).transpose(1, 0, 2, 3)

    max_len = max(ln for _, ln in groups)
    cos, sin = _rope_tables(max_len, min(TL_RET, groups[0][1]))

    row = lambda a: a.reshape(1, -1)
    for layer in range(DEPTH):
        sh1, sc1, g1, sh2, sc2, g2, sh3, sc3, g3 = [mods[layer, jm] for jm in range(N_MOD)]
        x = _ffn(x, row(p['norm_ffn1'][layer]), sh1, sc1, g1,
                 p['w_ffn1_in'][layer], p['w_ffn1_out'][layer], groups)
        idx = layer // 2
        gain = row(p['norm_mix'][layer])
        if layer % 2 == 0:
            proj = _proj(x, gain, sh2, sc2, p['w_in_even'][idx], TN_PROJ_EVEN, groups)
            y = _even_mix(proj, p['pool_w'][idx], row(p['pool_scale'][idx]), row(p['sgu_norm'][idx]),
                          p['sgu_w'][idx], p['sgu_b'][idx][..., None], groups)
            x = _outproj(x, y, y, 0, 1, g2, p['w_out_even'][idx], groups)
        else:
            proj = _proj(x, gain, sh2, sc2, p['w_in_odd'][idx], TN_PROJ_ODD, groups)
            yc = _retention(proj, cos, sin, p['ret_decay_f'][idx], p['ret_decay_b'][idx],
                            row(p['ret_norm'][idx]), groups)
            qkv = _dn_prep(proj, p['dn_conv'][idx], groups)
            yd = _deltanet(proj, qkv,
                           _gate_row(p['dn_a_log_f'][idx], p['dn_a_log_b'][idx]),
                           _gate_row(p['dn_dt_bias_f'][idx], p['dn_dt_bias_b'][idx]),
                           row(p['dn_norm'][idx]), groups)
            x = _outproj(x, yc, yd, 0, 0, g2, p['w_out_odd'][idx], groups)
        x = _ffn(x, row(p['norm_ffn2'][layer]), sh3, sc3, g3,
                 p['w_ffn2_in'][layer], p['w_ffn2_out'][layer], groups)
    return _final(x, row(p['norm_final']), fin[0], fin[1], groups)


def _prepare(p):
    q = dict(p)
    for name in ('w_ffn1_in', 'w_ffn1_out', 'w_ffn2_in', 'w_ffn2_out', 'w_in_even', 'w_out_even',
                 'pool_w', 'sgu_w', 'w_out_odd'):
        q[name] = p[name].astype(BF16)
    w = p['w_in_odd'].astype(BF16)
    q['w_in_odd'] = jnp.pad(w, ((0, 0), (0, 0), (0, P_ODD_PAD - P_ODD)))
    return q


def kernel(x_prompt, x_sample, c_prompt, c_sample, w_ada, b_ada, norm_ffn1, w_ffn1_in, w_ffn1_out, norm_mix, norm_ffn2, w_ffn2_in, w_ffn2_out, w_in_even, w_out_even, pool_w, pool_scale, sgu_norm, sgu_w, sgu_b, w_in_odd, w_out_odd, ret_decay_f, ret_decay_b, ret_norm, dn_conv, dn_a_log_f, dn_a_log_b, dn_dt_bias_f, dn_dt_bias_b, dn_norm, norm_final, w_ada_final, b_ada_final):
    p = _prepare({
        'w_ada': w_ada, 'b_ada': b_ada, 'norm_ffn1': norm_ffn1, 'w_ffn1_in': w_ffn1_in,
        'w_ffn1_out': w_ffn1_out, 'norm_mix': norm_mix, 'norm_ffn2': norm_ffn2,
        'w_ffn2_in': w_ffn2_in, 'w_ffn2_out': w_ffn2_out, 'w_in_even': w_in_even,
        'w_out_even': w_out_even, 'pool_w': pool_w, 'pool_scale': pool_scale,
        'sgu_norm': sgu_norm, 'sgu_w': sgu_w, 'sgu_b': sgu_b, 'w_in_odd': w_in_odd,
        'w_out_odd': w_out_odd, 'ret_decay_f': ret_decay_f, 'ret_decay_b': ret_decay_b,
        'ret_norm': ret_norm, 'dn_conv': dn_conv, 'dn_a_log_f': dn_a_log_f,
        'dn_a_log_b': dn_a_log_b, 'dn_dt_bias_f': dn_dt_bias_f, 'dn_dt_bias_b': dn_dt_bias_b,
        'dn_norm': dn_norm, 'norm_final': norm_final, 'w_ada_final': w_ada_final,
        'b_ada_final': b_ada_final,
    })
    bp, lp, d = x_prompt.shape
    bs, ls, _ = x_sample.shape
    groups = ((bp, lp), (bs, ls))
    x = jnp.concatenate([x_prompt.reshape(bp * lp, d), x_sample.reshape(bs * ls, d)], axis=0)
    c = jnp.concatenate([c_prompt, c_sample], axis=0)
    y = _trunk(x, c, p, groups)
    return (y[:bp * lp].reshape(bp, lp, d), y[bp * lp:].reshape(bs, ls, d))
```

```python
import functools

import jax
import jax.numpy as jnp
import numpy as np
from jax import lax
from jax.experimental import pallas as pl
from jax.experimental.pallas import tpu as pltpu

F32 = jnp.float32
BF16 = jnp.bfloat16

D_MODEL = 2048
DEPTH = 4
D_HALF = D_MODEL // 2
POOL_WINDOWS = (2, 4, 8, 16)
POOL_GROUP = D_HALF // len(POOL_WINDOWS)
SGU_CHUNK = 128
SGU_HEAD = 128
SGU_GROUPS = D_HALF // SGU_HEAD
RET_HEADS = 4
RET_D = D_HALF // RET_HEADS
RET_CHUNK = 128
ROPE_BASE = 10000.0
DN_HEADS = 8
DN_D = D_HALF // DN_HEADS
DN_CONV = 4
DN_CHUNK = 64
D_FF = 5632
N_MOD = 9
EPS = 1e-6
P_EVEN = 3 * D_HALF
P_RET = 4 * D_HALF
P_ODD_MAIN = P_RET + 4 * D_HALF
P_ODD = P_ODD_MAIN + 4 * DN_HEADS

LANES = 128
SUBLANES = 8
HALO = SUBLANES
VMEM_LIMIT = 56 * 1024 * 1024

TM_FFN = 512
TF_FFN = 512
TM_PROJ = 1024
TN_PROJ = 512
MOD_ROWS = 16
MOD_UNROLL = 8
TM_OUT = 512
TL_EVEN = 256
TL_RET = 512
TL_DN = 128
TL_PREP = 256
TM_FINAL = 512
TN_ADA = 1024
SEQ_PAD = 16


def _seq_info(row0, groups):
    seq = start = length = None
    t0 = s0 = 0
    for gi, (nb, ln) in enumerate(groups):
        rel = row0 - t0
        q = rel // ln
        if gi == 0:
            seq, start, length = q, q * ln, ln
        else:
            here = row0 >= t0
            seq = jnp.where(here, s0 + q, seq)
            start = jnp.where(here, t0 + q * ln, start)
            length = jnp.where(here, ln, length)
        t0 += nb * ln
        s0 += nb
    return seq, start, length


def _params(*sem):
    return pltpu.CompilerParams(dimension_semantics=sem, vmem_limit_bytes=VMEM_LIMIT)


def _modulated(x, gain, shift, scale):
    ms = jnp.mean(x * x, axis=-1, keepdims=True)
    y = x * lax.rsqrt(ms + EPS)
    return (y * gain) * (1.0 + scale) + shift


def _dot(a, b):
    return jnp.dot(a, b, preferred_element_type=F32)


def _dot_nt(a, b):
    return lax.dot_general(a, b, (((1,), (1,)), ((), ())), preferred_element_type=F32)


def _dot_tn(a, b):
    return lax.dot_general(a, b, (((0,), (0,)), ((), ())), preferred_element_type=F32)


def _split_bf16(a):
    hi = a.astype(BF16)
    lo = (a - hi.astype(F32)).astype(BF16)
    return hi, lo


def _dot_hi(a, b):
    ah, al = _split_bf16(a)
    bh, bl = _split_bf16(b)
    return _dot(ah, bh) + (_dot(ah, bl) + _dot(al, bh))


def _ada_kernel(c_ref, w_ref, b_ref, o_ref):
    c = c_ref[...]
    act = jax.nn.silu(c).astype(BF16)
    o_ref[...] = _dot(act, w_ref[...].astype(BF16)) + b_ref[...]


def _ada(c_pad, w, b):
    ly, d, n = w.shape
    s = c_pad.shape[0]
    tn = min(TN_ADA, n)
    return pl.pallas_call(
        _ada_kernel,
        out_shape=jax.ShapeDtypeStruct((ly, s, n), F32),
        grid=(ly, n // tn),
        in_specs=[
            pl.BlockSpec((s, d), lambda l, j: (0, 0)),
            pl.BlockSpec((None, d, tn), lambda l, j: (l, 0, j)),
            pl.BlockSpec((None, 1, tn), lambda l, j: (l, 0, j)),
        ],
        out_specs=pl.BlockSpec((None, s, tn), lambda l, j: (l, 0, j)),
        compiler_params=_params("parallel", "parallel"),
        name="ada_rows",
    )(c_pad, w, b.reshape(ly, 1, n))


def _modulate_into(h_ref, x_ref, gain_ref, sh_ref, sc_ref):
    tm = x_ref.shape[0]
    amp = gain_ref[...] * (1.0 + sc_ref[...])
    shift = sh_ref[...]

    def body(r, carry):
        rows = pl.ds(pl.multiple_of(r * MOD_ROWS, MOD_ROWS), MOD_ROWS)
        x = x_ref[rows, :]
        ms = jnp.mean(x * x, axis=-1, keepdims=True)
        h_ref[rows, :] = ((x * lax.rsqrt(ms + EPS)) * amp + shift).astype(BF16)
        return carry

    lax.fori_loop(0, tm // MOD_ROWS, body, 0, unroll=MOD_UNROLL)


def _ffn_kernel(x_ref, gain_ref, sh_ref, sc_ref, gt_ref, wg_ref, wu_ref, wo_ref, o_ref,
                h_ref, acc_ref, *, nf):
    f = pl.program_id(1)

    @pl.when(f == 0)
    def _():
        _modulate_into(h_ref, x_ref, gain_ref, sh_ref, sc_ref)
        acc_ref[...] = jnp.zeros_like(acc_ref)

    h = h_ref[...]
    g = _dot(h, wg_ref[...])
    u = _dot(h, wu_ref[...])
    a = (jax.nn.silu(g) * u).astype(BF16)
    acc_ref[...] += _dot(a, wo_ref[...])

    @pl.when(f == nf - 1)
    def _():
        o_ref[...] = x_ref[...] + (0.5 * gt_ref[...]) * acc_ref[...]


def _row_spec(groups, tm, d):
    return pl.BlockSpec((None, 1, d), lambda i, j: (_seq_info(i * tm, groups)[0], 0, 0))


def _ffn(x, gain, shift, scale, gate, w_in, w_out, layer, groups):
    t, d = x.shape
    ff = w_out.shape[1]
    tm = min(TM_FFN, groups[0][1])
    tf = min(TF_FFN, ff)
    nf = ff // tf
    row = _row_spec(groups, tm, d)
    return pl.pallas_call(
        functools.partial(_ffn_kernel, nf=nf),
        out_shape=jax.ShapeDtypeStruct((t, d), F32),
        grid=(t // tm, nf),
        in_specs=[
            pl.BlockSpec((tm, d), lambda i, f: (i, 0)),
            pl.BlockSpec((1, d), lambda i, f: (0, 0)),
            row, row, row,
            pl.BlockSpec((None, d, tf), lambda i, f: (layer, 0, f)),
            pl.BlockSpec((None, d, tf), lambda i, f: (layer, 0, nf + f)),
            pl.BlockSpec((None, tf, d), lambda i, f: (layer, f, 0)),
        ],
        out_specs=pl.BlockSpec((tm, d), lambda i, f: (i, 0)),
        scratch_shapes=[pltpu.VMEM((tm, d), BF16), pltpu.VMEM((tm, d), F32)],
        compiler_params=_params("parallel", "arbitrary"),
        name="ffn",
    )(x, gain, shift, scale, gate, w_in, w_in, w_out)


def _proj_kernel(*refs, narrow):
    if narrow:
        x_ref, gain_ref, sh_ref, sc_ref, w_ref, wn_ref, o_ref, on_ref, h_ref = refs
    else:
        x_ref, gain_ref, sh_ref, sc_ref, w_ref, o_ref, h_ref = refs

    @pl.when(pl.program_id(1) == 0)
    def _():
        _modulate_into(h_ref, x_ref, gain_ref, sh_ref, sc_ref)
        if narrow:
            on_ref[...] = _dot(h_ref[...], wn_ref[...])

    o_ref[...] = _dot(h_ref[...], w_ref[...])


def _proj(x, gain, shift, scale, w, idx, n, groups, w_narrow=None):
    t, d = x.shape
    tm = min(TM_PROJ, groups[0][1])
    tn = TN_PROJ
    row = _row_spec(groups, tm, d)
    narrow = w_narrow is not None
    in_specs = [
        pl.BlockSpec((tm, d), lambda i, j: (i, 0)),
        pl.BlockSpec((1, d), lambda i, j: (0, 0)),
        row, row,
        pl.BlockSpec((None, d, tn), lambda i, j: (idx, 0, j)),
    ]
    out_shape = jax.ShapeDtypeStruct((t, n), F32)
    out_specs = pl.BlockSpec((tm, tn), lambda i, j: (i, j))
    args = (x, gain, shift, scale, w)
    if narrow:
        in_specs.append(pl.BlockSpec((None, d, LANES), lambda i, j: (idx, 0, 0)))
        out_shape = (out_shape, jax.ShapeDtypeStruct((t, LANES), F32))
        out_specs = (out_specs, pl.BlockSpec((tm, LANES), lambda i, j: (i, 0)))
        args = args + (w_narrow,)
    return pl.pallas_call(
        functools.partial(_proj_kernel, narrow=narrow),
        out_shape=out_shape,
        grid=(t // tm, n // tn),
        in_specs=in_specs,
        out_specs=out_specs,
        scratch_shapes=[pltpu.VMEM((tm, d), BF16)],
        compiler_params=_params("parallel", "arbitrary"),
        name="mix_proj",
    )(*args)


def _outproj_kernel(x_ref, ya_ref, yb_ref, gt_ref, wa_ref, wb_ref, o_ref):
    y = _dot(ya_ref[...], wa_ref[...]) + _dot(yb_ref[...], wb_ref[...])
    o_ref[...] = x_ref[...] + gt_ref[...] * y


def _outproj(x, ya, yb, ca, cb, gate, w, idx, groups):
    t, d = x.shape
    dh = d // 2
    tm = min(TM_OUT, groups[0][1])
    row = _row_spec(groups, tm, d)
    return pl.pallas_call(
        _outproj_kernel,
        out_shape=jax.ShapeDtypeStruct((t, d), F32),
        grid=(t // tm, 1),
        in_specs=[
            pl.BlockSpec((tm, d), lambda i, j: (i, 0)),
            pl.BlockSpec((tm, dh), lambda i, j: (i, ca)),
            pl.BlockSpec((tm, dh), lambda i, j: (i, cb)),
            row,
            pl.BlockSpec((None, dh, d), lambda i, j: (idx, 0, 0)),
            pl.BlockSpec((None, dh, d), lambda i, j: (idx, 1, 0)),
        ],
        out_specs=pl.BlockSpec((tm, d), lambda i, j: (i, 0)),
        compiler_params=_params("parallel", "arbitrary"),
        name="mix_out",
    )(x, ya, yb, gate, w, w)


def _final_kernel(x_ref, gain_ref, sh_ref, sc_ref, o0_ref, o1_ref, *, n0):
    y = _modulated(x_ref[...], gain_ref[...], sh_ref[...], sc_ref[...])
    i = pl.program_id(0)

    @pl.when(i < n0)
    def _():
        o0_ref[...] = y

    @pl.when(i >= n0)
    def _():
        o1_ref[...] = y


def _final(x, gain, shift, scale, groups):
    t, d = x.shape
    tm = min(TM_FINAL, groups[0][1])
    (b0, l0), (b1, l1) = groups
    n0 = b0 * l0 // tm
    row = _row_spec(groups, tm, d)
    return pl.pallas_call(
        functools.partial(_final_kernel, n0=n0),
        out_shape=(jax.ShapeDtypeStruct((b0 * l0, d), F32), jax.ShapeDtypeStruct((b1 * l1, d), F32)),
        grid=(t // tm, 1),
        in_specs=[
            pl.BlockSpec((tm, d), lambda i, j: (i, 0)),
            pl.BlockSpec((1, d), lambda i, j: (0, 0)),
            row, row,
        ],
        out_specs=(pl.BlockSpec((tm, d), lambda i, j: (jnp.minimum(i, n0 - 1), 0)),
                   pl.BlockSpec((tm, d), lambda i, j: (jnp.maximum(i - n0, 0), 0))),
        compiler_params=_params("arbitrary", "arbitrary"),
        name="final_mod",
    )(x, gain, shift, scale)


def _halo_specs(tl, width, nrows, col_of):
    per = tl // HALO
    last = nrows // HALO - 1
    prev = pl.BlockSpec((HALO, width), lambda i, *r: (jnp.maximum(i * per - 1, 0), col_of(i, *r)))
    nxt = pl.BlockSpec((HALO, width), lambda i, *r: (jnp.minimum((i + 1) * per, last), col_of(i, *r)))
    return prev, nxt


def _fill_ext(ext_ref, x_ref, prev_ref, next_ref, first, last, tl):
    ext_ref[HALO:HALO + tl, :] = x_ref[...]
    ext_ref[0:HALO, :] = jnp.where(first, 0.0, prev_ref[...])
    ext_ref[HALO + tl:2 * HALO + tl, :] = jnp.where(last, 0.0, next_ref[...])


def _even_kernel(xa_ref, prev_ref, next_ref, u_ref, v_ref, pw_ref, ps_ref, ng_ref, sw_ref, sb_ref,
                 o_ref, ext_ref, vn_ref, *, tl, groups):
    row0 = pl.program_id(0) * tl
    _, sstart, slen = _seq_info(row0, groups)
    pos0 = row0 - sstart
    _fill_ext(ext_ref, xa_ref, prev_ref, next_ref, pos0 == 0, pos0 + tl == slen, tl)

    t = pos0 + lax.broadcasted_iota(jnp.int32, (tl, 1), 0)
    for gi, w in enumerate(POOL_WINDOWS):
        c0 = gi * POOL_GROUP
        cols = slice(c0, c0 + POOL_GROUP)
        base = HALO - w // 2
        s = ext_ref[base:base + tl, cols]
        for dlt in range(1, w):
            s = s + ext_ref[base + dlt:base + dlt + tl, cols]
        lo = jnp.clip(t - w // 2, 0, slen)
        hi = jnp.clip(t + (w - w // 2), 0, slen)
        cnt = (hi - lo).astype(F32)
        pooled = (s / cnt - xa_ref[:, cols]).astype(BF16)
        ya = _dot(pooled, pw_ref[gi]) * ps_ref[:, cols]
        o_ref[:, cols] = ya.astype(BF16)

    v = jax.nn.gelu(v_ref[...])
    vms = jnp.mean(v * v, axis=-1, keepdims=True)
    vn_ref[...] = ((v * lax.rsqrt(vms + EPS)) * ng_ref[...]).astype(BF16)
    for n in range(tl // SGU_CHUNK):
        rows = slice(n * SGU_CHUNK, (n + 1) * SGU_CHUNK)
        for g in range(SGU_GROUPS):
            cols = slice(g * SGU_HEAD, (g + 1) * SGU_HEAD)
            mixed = _dot(sw_ref[g], vn_ref[rows, cols]) + sb_ref[g]
            u = jax.nn.gelu(u_ref[rows, cols])
            o_ref[rows, D_HALF + g * SGU_HEAD:D_HALF + (g + 1) * SGU_HEAD] = (u * mixed).astype(BF16)


def _even_mix(proj, pool_w, pool_scale, sgu_norm, sgu_w, sgu_b, groups):
    t = proj.shape[0]
    tl = min(TL_EVEN, groups[0][1])
    prev, nxt = _halo_specs(tl, D_HALF, t, lambda i: 0)
    const2 = lambda i: (0, 0)
    const3 = lambda i: (0, 0, 0)
    return pl.pallas_call(
        functools.partial(_even_kernel, tl=tl, groups=groups),
        out_shape=jax.ShapeDtypeStruct((t, 2 * D_HALF), BF16),
        grid=(t // tl,),
        in_specs=[
            pl.BlockSpec((tl, D_HALF), lambda i: (i, 0)),
            prev, nxt,
            pl.BlockSpec((tl, D_HALF), lambda i: (i, 1)),
            pl.BlockSpec((tl, D_HALF), lambda i: (i, 2)),
            pl.BlockSpec(pool_w.shape, const3),
            pl.BlockSpec((1, D_HALF), const2),
            pl.BlockSpec((1, D_HALF), const2),
            pl.BlockSpec(sgu_w.shape, const3),
            pl.BlockSpec(sgu_b.shape, const3),
        ],
        out_specs=pl.BlockSpec((tl, 2 * D_HALF), lambda i: (i, 0)),
        scratch_shapes=[pltpu.VMEM((tl + 2 * HALO, D_HALF), F32), pltpu.VMEM((tl, D_HALF), BF16)],
        compiler_params=_params("parallel"),
        name="even_mix",
    )(proj, proj, proj, proj, proj, pool_w, pool_scale, sgu_norm, sgu_w, sgu_b)


def _rope_kernel(inv_ref, cos_ref, sin_ref, *, tl):
    pos = (pl.program_id(0) * tl + lax.broadcasted_iota(jnp.int32, (tl, 1), 0)).astype(F32)
    ang = pos * inv_ref[...]
    cos_ref[...] = jnp.cos(ang)
    sin_ref[...] = jnp.sin(ang)


def _rope_tables(max_len, tl):
    half = RET_D // 2
    inv = (1.0 / (ROPE_BASE ** jnp.linspace(0.0, 1.0, half, dtype=F32))).reshape(1, half)
    shp = jax.ShapeDtypeStruct((max_len, half), F32)
    return pl.pallas_call(
        functools.partial(_rope_kernel, tl=tl),
        out_shape=(shp, shp),
        grid=(max_len // tl,),
        in_specs=[pl.BlockSpec((1, half), lambda i: (0, 0))],
        out_specs=(pl.BlockSpec((tl, half), lambda i: (i, 0)), pl.BlockSpec((tl, half), lambda i: (i, 0))),
        compiler_params=_params("parallel"),
        name="rope_table",
    )(inv)


def _rope_apply(x, cos, sin):
    half = RET_D // 2
    x1 = x[:, :half]
    x2 = x[:, half:]
    return jnp.concatenate([x1 * cos - x2 * sin, x1 * sin + x2 * cos], axis=-1)


def _ret_kernel(*refs, tl, nt, groups, reverse):
    if reverse:
        q_ref, k_ref, v_ref, cos_ref, sin_ref, dec_ref, o_ref, s_ref = refs
    else:
        (q_ref, k_ref, v_ref, cos_ref, sin_ref, dec_ref, decb_ref, g_ref, ob_ref, ng_ref,
         o_ref, s_ref) = refs
    j = pl.program_id(1)
    it = nt - 1 - j if reverse else j
    row0 = it * tl
    _, sstart, slen = _seq_info(row0, groups)
    pos0 = row0 - sstart
    reset = (pos0 + tl == slen) if reverse else (pos0 == 0)

    @pl.when(reset)
    def _():
        s_ref[...] = jnp.zeros_like(s_ref)

    c = RET_CHUNK
    lg = jnp.log1p(-jnp.exp2(-dec_ref[...]))
    idx = lax.broadcasted_iota(jnp.int32, (c, 1), 0).astype(F32)
    if reverse:
        q_dec = jnp.exp(lg * (c - idx))
        k_dec = jnp.exp(lg * idx)
    else:
        q_dec = jnp.exp(lg * (idx + 1.0))
        k_dec = jnp.exp(lg * (c - 1.0 - idx))
        lgb = jnp.log1p(-jnp.exp2(-decb_ref[...]))
        ri = lax.broadcasted_iota(jnp.int32, (c, c), 0)
        ci = lax.broadcasted_iota(jnp.int32, (c, c), 1)
        rel = (ri - ci).astype(F32)
        dmat = (jnp.where(rel >= 0, jnp.exp(lg * jnp.maximum(rel, 0.0)), 0.0)
                + jnp.where(rel <= 0, jnp.exp(lgb * jnp.maximum(-rel, 0.0)), 0.0))
    chunk_dec = jnp.exp(lg * float(c))

    nc = tl // c
    order = range(nc - 1, -1, -1) if reverse else range(nc)
    for ch in order:
        rows = slice(ch * c, (ch + 1) * c)
        cos = cos_ref[rows, :]
        sin = sin_ref[rows, :]
        q = _rope_apply(q_ref[rows, :], cos, sin)
        k = _rope_apply(k_ref[rows, :], cos, sin) * (RET_D ** -0.5)
        v = v_ref[rows, :].astype(BF16)
        s = s_ref[...]
        inter = _dot((q * q_dec).astype(BF16), s.astype(BF16))
        s_ref[...] = s * chunk_dec + _dot_tn((k * k_dec).astype(BF16), v)
        if reverse:
            o_ref[rows, :] = inter
        else:
            scores = _dot_nt(q.astype(BF16), k.astype(BF16)) * dmat
            o = _dot(scores.astype(BF16), v) + inter + ob_ref[rows, :]
            mu = jnp.mean(o, axis=-1, keepdims=True)
            var = jnp.mean(jnp.square(o - mu), axis=-1, keepdims=True)
            on = ((o - mu) * lax.rsqrt(var + EPS)) * ng_ref[...]
            o_ref[rows, :] = (jax.nn.silu(g_ref[rows, :]) * on).astype(BF16)


def _retention(proj, cos, sin, decay_f, decay_b, norm_g, groups):
    t = proj.shape[0]
    tl = min(TL_RET, groups[0][1])
    nt = t // tl
    hb = D_HALF // RET_D
    half = RET_D // 2

    def tile(j, reverse):
        return nt - 1 - j if reverse else j

    def specs(reverse):
        def pos_block(h, j):
            row0 = tile(j, reverse) * tl
            return ((row0 - _seq_info(row0, groups)[1]) // tl, 0)
        sec = lambda s: pl.BlockSpec((tl, RET_D), lambda h, j: (tile(j, reverse), s * hb + h))
        tab = pl.BlockSpec((tl, half), pos_block)
        dec = pl.BlockSpec((None, 1, 1), lambda h, j: (h, 0, 0))
        return sec, tab, dec

    sec, tab, dec = specs(True)
    ob = pl.pallas_call(
        functools.partial(_ret_kernel, tl=tl, nt=nt, groups=groups, reverse=True),
        out_shape=jax.ShapeDtypeStruct((t, D_HALF), F32),
        grid=(RET_HEADS, nt),
        in_specs=[sec(0), sec(1), sec(2), tab, tab, dec],
        out_specs=pl.BlockSpec((tl, RET_D), lambda h, j: (nt - 1 - j, h)),
        scratch_shapes=[pltpu.VMEM((RET_D, RET_D), F32)],
        compiler_params=_params("parallel", "arbitrary"),
        name="ret_bwd",
    )(proj, proj, proj, cos, sin, decay_b.reshape(RET_HEADS, 1, 1))

    sec, tab, dec = specs(False)
    return pl.pallas_call(
        functools.partial(_ret_kernel, tl=tl, nt=nt, groups=groups, reverse=False),
        out_shape=jax.ShapeDtypeStruct((t, D_HALF), BF16),
        grid=(RET_HEADS, nt),
        in_specs=[sec(0), sec(1), sec(2), tab, tab, dec, dec, sec(3),
                  pl.BlockSpec((tl, RET_D), lambda h, j: (j, h)),
                  pl.BlockSpec((1, RET_D), lambda h, j: (0, h))],
        out_specs=pl.BlockSpec((tl, RET_D), lambda h, j: (j, h)),
        scratch_shapes=[pltpu.VMEM((RET_D, RET_D), F32)],
        compiler_params=_params("parallel", "arbitrary"),
        name="ret_fwd",
    )(proj, proj, proj, cos, sin, decay_f.reshape(RET_HEADS, 1, 1), decay_b.reshape(RET_HEADS, 1, 1),
      proj, ob, norm_g)


def _dnprep_kernel(x_ref, prev_ref, next_ref, w_ref, o_ref, ext_ref, *, tl, groups):
    row0 = pl.program_id(0) * tl
    part = pl.program_id(1)
    _, sstart, slen = _seq_info(row0, groups)
    pos0 = row0 - sstart
    _fill_ext(ext_ref, x_ref, prev_ref, next_ref, pos0 == 0, pos0 + tl == slen, tl)
    left = DN_CONV // 2
    q_scale = jnp.where(part == 0, DN_D ** -0.5, 1.0).astype(F32)
    for h in range(DN_HEADS):
        cols = slice(h * DN_D, (h + 1) * DN_D)
        conv = ext_ref[HALO - left:HALO - left + tl, cols] * w_ref[0:1, cols]
        for tap in range(1, DN_CONV):
            r0 = HALO - left + tap
            conv = conv + ext_ref[r0:r0 + tl, cols] * w_ref[tap:tap + 1, cols]
        y = jax.nn.silu(conv)
        inv_norm = lax.rsqrt(jnp.sum(y * y, axis=-1, keepdims=True) + EPS)
        o_ref[:, cols] = y * jnp.where(part < 2, inv_norm * q_scale, 1.0)


def _dn_prep(proj, conv_w, groups):
    t = proj.shape[0]
    tl = min(TL_PREP, groups[0][1])
    base = P_RET // D_HALF
    prev, nxt = _halo_specs(tl, D_HALF, t, lambda i, part: base + part)
    return pl.pallas_call(
        functools.partial(_dnprep_kernel, tl=tl, groups=groups),
        out_shape=jax.ShapeDtypeStruct((t, 3 * D_HALF), F32),
        grid=(t // tl, 3),
        in_specs=[pl.BlockSpec((tl, D_HALF), lambda i, part: (i, base + part)), prev, nxt,
                  pl.BlockSpec((DN_CONV, D_HALF), lambda i, part: (0, part))],
        out_specs=pl.BlockSpec((tl, D_HALF), lambda i, part: (i, part)),
        scratch_shapes=[pltpu.VMEM((tl + 2 * HALO, D_HALF), F32)],
        compiler_params=_params("parallel", "parallel"),
        name="dn_prep",
    )(proj, proj, proj, conv_w)


def _dot_hi_each(lhs, rhs):
    ls = [_split_bf16(a) for a in lhs]
    rs = [_split_bf16(b) for b in rhs]
    main = [_dot(ah, bh) for (ah, _), (bh, _) in zip(ls, rs)]
    cross = [_dot(ah, bl) + _dot(al, bh) for (ah, al), (bh, bl) in zip(ls, rs)]
    return [m + x for m, x in zip(main, cross)]


def _unit_triangular_inverses(mats, eye, ri, ci):
    size = SUBLANES
    same = (ri // size) == (ci // size)
    ps = [jnp.where(same, a, 0.0) for a in mats]
    invs = [eye - d for d in ps]
    n = 2
    while n < size:
        ps = _dot_hi_each(ps, ps)
        invs = [inv + x for inv, x in zip(invs, _dot_hi_each(invs, ps))]
        n *= 2
    while size < DN_CHUNK:
        size *= 2
        merged = (ri // size) == (ci // size)
        es = [jnp.where(merged & ~same, a, 0.0) for a in mats]
        invs = [inv - x for inv, x in zip(invs, _dot_hi_each(_dot_hi_each(invs, es), invs))]
        same = merged
    return invs


def _dn_kernel(*refs, tl, nt, groups, reverse):
    if reverse:
        q_ref, k_ref, v_ref, gates_ref, alog_ref, dtb_ref, o_ref, s_ref = refs
    else:
        (q_ref, k_ref, v_ref, gates_ref, alog_ref, dtb_ref, z_ref, ob_ref, ng_ref,
         o_ref, s_ref) = refs
    j = pl.program_id(0)
    it = nt - 1 - j if reverse else j
    row0 = it * tl
    _, sstart, slen = _seq_info(row0, groups)
    pos0 = row0 - sstart
    reset = (pos0 + tl == slen) if reverse else (pos0 == 0)

    @pl.when(reset)
    def _():
        s_ref[...] = jnp.zeros_like(s_ref)

    gates = gates_ref[...]
    beta_all = jax.nn.sigmoid(gates)
    la_all = -jnp.exp(alog_ref[...]) * jax.nn.softplus(gates + dtb_ref[...])
    cb0 = DN_HEADS if reverse else 0
    ca0 = cb0 + 2 * DN_HEADS

    c = DN_CHUNK
    ri = lax.broadcasted_iota(jnp.int32, (c, c), 0)
    ci = lax.broadcasted_iota(jnp.int32, (c, c), 1)
    eye = (ri == ci).astype(F32)
    incl = (ri <= ci) if reverse else (ri >= ci)
    strict = (ri < ci) if reverse else (ri > ci)
    incl_t = (ri >= ci) if reverse else (ri <= ci)
    last = 0 if reverse else c - 1

    nc = tl // c
    order = list(range(nc - 1, -1, -1) if reverse else range(nc))
    heads = range(DN_HEADS)
    units = [(ch, h) for ch in order for h in heads]

    def rows(ch):
        return slice(ch * c, (ch + 1) * c)

    def cols(h):
        return slice(h * DN_D, (h + 1) * DN_D)

    q = [q_ref[rows(ch), cols(h)] for ch, h in units]
    k = [k_ref[rows(ch), cols(h)] for ch, h in units]
    beta = [beta_all[rows(ch), cb0 + h:cb0 + h + 1] for ch, h in units]
    la = [la_all[rows(ch), ca0 + h:ca0 + h + 1] for ch, h in units]
    la_row = [jnp.sum(eye * x, axis=0, keepdims=True) for x in la]
    g_col = [jnp.sum(jnp.where(incl, x, 0.0), axis=1, keepdims=True) for x in la_row]
    g_row = [jnp.sum(jnp.where(incl_t, x, 0.0), axis=0, keepdims=True) for x in la]
    gam = [jnp.where(incl, jnp.exp(jnp.where(incl, gc - gr, 0.0)), 0.0) for gc, gr in zip(g_col, g_row)]
    eg = [jnp.exp(gc) for gc in g_col]
    g_last = [gc[last:last + 1, :] for gc in g_col]
    kb = [x * b for x, b in zip(k, beta)]
    kbf = [x.astype(BF16) for x in k]
    kq = [_dot_nt(jnp.concatenate([x, y], axis=0).astype(BF16), z) for x, y, z in zip(kb, q, kbf)]
    a = [jnp.where(strict, x[:c] * gm, 0.0) for x, gm in zip(kq, gam)]
    attn = [(x[c:] * gm).astype(BF16) for x, gm in zip(kq, gam)]
    tinv = _unit_triangular_inverses(a, eye, ri, ci)
    rhs = [jnp.concatenate([v_ref[rows(ch), cols(h)] * b, x * e], axis=1)
           for (ch, h), b, x, e in zip(units, beta, kb, eg)]
    uw = _dot_hi_each(tinv, rhs)
    qe = [(x * e).astype(BF16) for x, e in zip(q, eg)]
    kt = [(x * jnp.exp(gl - gc)).astype(BF16) for x, gl, gc in zip(k, g_last, g_col)]
    dec = [jnp.exp(gl) for gl in g_last]

    for ci_, ch in enumerate(order):
        idx = [ci_ * DN_HEADS + h for h in heads]
        s = [s_ref[h] for h in heads]
        sb = [x.astype(BF16) for x in s]
        ws = [_dot(jnp.concatenate([uw[i][:, DN_D:].astype(BF16), qe[i]], axis=0), sb[h])
              for h, i in zip(heads, idx)]
        vnb = [(uw[i][:, :DN_D] - x[:c]).astype(BF16) for i, x in zip(idx, ws)]
        o = [x[c:] + _dot(attn[i], vn) for i, x, vn in zip(idx, ws, vnb)]
        for h, i in zip(heads, idx):
            s_ref[h] = s[h] * dec[i] + _dot_tn(kt[i], vnb[h])
        for h in heads:
            if reverse:
                o_ref[rows(ch), cols(h)] = o[h]
            else:
                oo = o[h] + ob_ref[rows(ch), cols(h)]
                on = (oo * lax.rsqrt(jnp.mean(oo * oo, axis=-1, keepdims=True) + EPS)) * ng_ref[...]
                o_ref[rows(ch), cols(h)] = (on * jax.nn.silu(z_ref[rows(ch), cols(h)])).astype(BF16)


def _deltanet(proj, gate_cols, qkv, alog_row, dtb_row, norm_g, groups):
    t = proj.shape[0]
    tl = min(TL_DN, groups[0][1])
    nt = t // tl
    z_blk = (P_RET + 3 * D_HALF) // D_HALF
    row = pl.BlockSpec((1, LANES), lambda j: (0, 0))
    state = pltpu.VMEM((DN_HEADS, DN_D, DN_D), F32)

    def common(reverse):
        tile = (lambda j: nt - 1 - j) if reverse else (lambda j: j)
        sec = lambda s: pl.BlockSpec((tl, D_HALF), lambda j: (tile(j), s))
        gates = pl.BlockSpec((tl, LANES), lambda j: (tile(j), 0))
        return sec, gates

    sec, gates = common(True)
    ob = pl.pallas_call(
        functools.partial(_dn_kernel, tl=tl, nt=nt, groups=groups, reverse=True),
        out_shape=jax.ShapeDtypeStruct((t, D_HALF), F32),
        grid=(nt,),
        in_specs=[sec(0), sec(1), sec(2), gates, row, row],
        out_specs=sec(0),
        scratch_shapes=[state],
        compiler_params=_params("arbitrary"),
        name="dn_bwd",
    )(qkv, qkv, qkv, gate_cols, alog_row, dtb_row)

    sec, gates = common(False)
    return pl.pallas_call(
        functools.partial(_dn_kernel, tl=tl, nt=nt, groups=groups, reverse=False),
        out_shape=jax.ShapeDtypeStruct((t, D_HALF), BF16),
        grid=(nt,),
        in_specs=[sec(0), sec(1), sec(2), gates, row, row, sec(z_blk), sec(0), row],
        out_specs=sec(0),
        scratch_shapes=[state],
        compiler_params=_params("arbitrary"),
        name="dn_fwd",
    )(qkv, qkv, qkv, gate_cols, alog_row, dtb_row, proj, ob, norm_g)


def _gate_row(f_vals, b_vals):
    row = jnp.zeros((LANES,), F32)
    row = row.at[2 * DN_HEADS:3 * DN_HEADS].set(f_vals.astype(F32))
    row = row.at[3 * DN_HEADS:4 * DN_HEADS].set(b_vals.astype(F32))
    return row.reshape(1, LANES)


def _trunk(x, c, p, groups):
    t, d = x.shape
    n_seq = c.shape[0]
    c_pad = jnp.zeros((SEQ_PAD, d), F32).at[:n_seq].set(c)
    mods = _ada(c_pad, p['w_ada'], p['b_ada'])
    mods = mods.reshape(DEPTH, SEQ_PAD, N_MOD, 1, d).transpose(0, 2, 1, 3, 4)
    fin = _ada(c_pad, p['w_ada_final'][None], p['b_ada_final'][None])
    fin = fin.reshape(SEQ_PAD, 2, 1, d).transpose(1, 0, 2, 3)

    max_len = max(ln for _, ln in groups)
    cos, sin = _rope_tables(max_len, min(TL_RET, groups[0][1]))

    row = lambda a: a.reshape(1, -1)
    for layer in range(DEPTH):
        sh1, sc1, g1, sh2, sc2, g2, sh3, sc3, g3 = [mods[layer, jm] for jm in range(N_MOD)]
        x = _ffn(x, row(p['norm_ffn1'][layer]), sh1, sc1, g1, p['w_ffn1_in'], p['w_ffn1_out'], layer, groups)
        idx = layer // 2
        gain = row(p['norm_mix'][layer])
        if layer % 2 == 0:
            proj = _proj(x, gain, sh2, sc2, p['w_in_even'], idx, P_EVEN, groups)
            y = _even_mix(proj, p['pool_w'][idx], row(p['pool_scale'][idx]), row(p['sgu_norm'][idx]),
                          p['sgu_w'][idx], p['sgu_b'][idx][..., None], groups)
            x = _outproj(x, y, y, 0, 1, g2, p['w_out_even'], idx, groups)
        else:
            proj, gate_cols = _proj(x, gain, sh2, sc2, p['w_in_odd'], idx, P_ODD_MAIN, groups,
                                    w_narrow=p['w_in_odd_gates'])
            yc = _retention(proj, cos, sin, p['ret_decay_f'][idx], p['ret_decay_b'][idx],
                            row(p['ret_norm'][idx]), groups)
            qkv = _dn_prep(proj, p['dn_conv'][idx], groups)
            yd = _deltanet(proj, gate_cols, qkv,
                           _gate_row(p['dn_a_log_f'][idx], p['dn_a_log_b'][idx]),
                           _gate_row(p['dn_dt_bias_f'][idx], p['dn_dt_bias_b'][idx]),
                           row(p['dn_norm'][idx]), groups)
            x = _outproj(x, yc, yd, 0, 0, g2, p['w_out_odd'], idx, groups)
        x = _ffn(x, row(p['norm_ffn2'][layer]), sh3, sc3, g3, p['w_ffn2_in'], p['w_ffn2_out'], layer, groups)
    return _final(x, row(p['norm_final']), fin[0], fin[1], groups)


def _prepare(p):
    q = dict(p)
    for name in ('w_ffn1_in', 'w_ffn1_out', 'w_ffn2_in', 'w_ffn2_out', 'w_in_even', 'w_out_even',
                 'pool_w', 'sgu_w', 'w_in_odd', 'w_out_odd'):
        q[name] = p[name].astype(BF16)
    gates = q['w_in_odd'][:, :, P_ODD_MAIN:]
    q['w_in_odd_gates'] = jnp.pad(gates, ((0, 0), (0, 0), (0, LANES - gates.shape[-1])))
    return q


def kernel(x_prompt, x_sample, c_prompt, c_sample, w_ada, b_ada, norm_ffn1, w_ffn1_in, w_ffn1_out, norm_mix, norm_ffn2, w_ffn2_in, w_ffn2_out, w_in_even, w_out_even, pool_w, pool_scale, sgu_norm, sgu_w, sgu_b, w_in_odd, w_out_odd, ret_decay_f, ret_decay_b, ret_norm, dn_conv, dn_a_log_f, dn_a_log_b, dn_dt_bias_f, dn_dt_bias_b, dn_norm, norm_final, w_ada_final, b_ada_final):
    p = _prepare({
        'w_ada': w_ada, 'b_ada': b_ada, 'norm_ffn1': norm_ffn1, 'w_ffn1_in': w_ffn1_in,
        'w_ffn1_out': w_ffn1_out, 'norm_mix': norm_mix, 'norm_ffn2': norm_ffn2,
        'w_ffn2_in': w_ffn2_in, 'w_ffn2_out': w_ffn2_out, 'w_in_even': w_in_even,
        'w_out_even': w_out_even, 'pool_w': pool_w, 'pool_scale': pool_scale,
        'sgu_norm': sgu_norm, 'sgu_w': sgu_w, 'sgu_b': sgu_b, 'w_in_odd': w_in_odd,
        'w_out_odd': w_out_odd, 'ret_decay_f': ret_decay_f, 'ret_decay_b': ret_decay_b,
        'ret_norm': ret_norm, 'dn_conv': dn_conv, 'dn_a_log_f': dn_a_log_f,
        'dn_a_log_b': dn_a_log_b, 'dn_dt_bias_f': dn_dt_bias_f, 'dn_dt_bias_b': dn_dt_bias_b,
        'dn_norm': dn_norm, 'norm_final': norm_final, 'w_ada_final': w_ada_final,
        'b_ada_final': b_ada_final,
    })
    bp, lp, d = x_prompt.shape
    bs, ls, _ = x_sample.shape
    groups = ((bp, lp), (bs, ls))
    x = jnp.concatenate([x_prompt.reshape(bp * lp, d), x_sample.reshape(bs * ls, d)], axis=0)
    c = jnp.concatenate([c_prompt, c_sample], axis=0)
    y_prompt, y_sample = _trunk(x, c, p, groups)
    return (y_prompt.reshape(bp, lp, d), y_sample.reshape(bs, ls, d))
```

```python
import functools

import jax
import jax.numpy as jnp
import numpy as np
from jax import lax
from jax.experimental import pallas as pl
from jax.experimental.pallas import tpu as pltpu

F32 = jnp.float32
BF16 = jnp.bfloat16

D_MODEL = 2048
DEPTH = 4
D_HALF = D_MODEL // 2
POOL_WINDOWS = (2, 4, 8, 16)
POOL_GROUP = D_HALF // len(POOL_WINDOWS)
SGU_CHUNK = 128
SGU_HEAD = 128
SGU_GROUPS = D_HALF // SGU_HEAD
RET_HEADS = 4
RET_D = D_HALF // RET_HEADS
RET_CHUNK = 128
ROPE_BASE = 10000.0
DN_HEADS = 8
DN_D = D_HALF // DN_HEADS
DN_CONV = 4
DN_CHUNK = 64
D_FF = 5632
N_MOD = 9
EPS = 1e-6
P_EVEN = 3 * D_HALF
P_RET = 4 * D_HALF
P_ODD_MAIN = P_RET + 4 * D_HALF
P_ODD = P_ODD_MAIN + 4 * DN_HEADS

LANES = 128
SUBLANES = 8
HALO = 2 * SUBLANES
VMEM_LIMIT = 56 * 1024 * 1024

TM_FFN = 1024
TF_FFN = 512
TM_PROJ = 1024
TN_PROJ = 512
MOD_ROWS = 16
MOD_UNROLL = 8
TM_OUT = 512
TL_EVEN = 256
TL_RET = 512
TL_DN = 128
TL_PREP = 256
TM_FINAL = 512
TN_ADA = 1024
SEQ_PAD = 16


def _seq_info(row0, groups):
    seq = start = length = None
    t0 = s0 = 0
    for gi, (nb, ln) in enumerate(groups):
        rel = row0 - t0
        q = rel // ln
        if gi == 0:
            seq, start, length = q, q * ln, ln
        else:
            here = row0 >= t0
            seq = jnp.where(here, s0 + q, seq)
            start = jnp.where(here, t0 + q * ln, start)
            length = jnp.where(here, ln, length)
        t0 += nb * ln
        s0 += nb
    return seq, start, length


def _params(*sem):
    return pltpu.CompilerParams(dimension_semantics=sem, vmem_limit_bytes=VMEM_LIMIT)


def _modulated(x, gain, shift, scale):
    ms = jnp.mean(x * x, axis=-1, keepdims=True)
    y = x * lax.rsqrt(ms + EPS)
    return (y * gain) * (1.0 + scale) + shift


def _dot(a, b):
    return jnp.dot(a, b, preferred_element_type=F32)


def _dot_nt(a, b):
    return lax.dot_general(a, b, (((1,), (1,)), ((), ())), preferred_element_type=F32)


def _dot_tn(a, b):
    return lax.dot_general(a, b, (((0,), (0,)), ((), ())), preferred_element_type=F32)


def _split_bf16(a):
    hi = a.astype(BF16)
    lo = (a - hi.astype(F32)).astype(BF16)
    return hi, lo


def _dot_hi(a, b):
    ah, al = _split_bf16(a)
    bh, bl = _split_bf16(b)
    return _dot(ah, bh) + (_dot(ah, bl) + _dot(al, bh))


def _ada_kernel(c_ref, w_ref, b_ref, o_ref):
    c = c_ref[...]
    act = jax.nn.silu(c).astype(BF16)
    o_ref[...] = _dot(act, w_ref[...].astype(BF16)) + b_ref[...]


def _ada(c_pad, w, b):
    ly, d, n = w.shape
    s = c_pad.shape[0]
    tn = min(TN_ADA, n)
    return pl.pallas_call(
        _ada_kernel,
        out_shape=jax.ShapeDtypeStruct((ly, s, n), F32),
        grid=(ly, n // tn),
        in_specs=[
            pl.BlockSpec((s, d), lambda l, j: (0, 0)),
            pl.BlockSpec((None, d, tn), lambda l, j: (l, 0, j)),
            pl.BlockSpec((None, 1, tn), lambda l, j: (l, 0, j)),
        ],
        out_specs=pl.BlockSpec((None, s, tn), lambda l, j: (l, 0, j)),
        compiler_params=_params("parallel", "parallel"),
        name="ada_rows",
    )(c_pad, w, b.reshape(ly, 1, n))


def _modulate_into(h_ref, x_ref, gain_ref, sh_ref, sc_ref):
    tm = x_ref.shape[0]
    amp = gain_ref[...] * (1.0 + sc_ref[...])
    shift = sh_ref[...]

    def body(r, carry):
        rows = pl.ds(pl.multiple_of(r * MOD_ROWS, MOD_ROWS), MOD_ROWS)
        x = x_ref[rows, :]
        ms = jnp.mean(x * x, axis=-1, keepdims=True)
        h_ref[rows, :] = ((x * lax.rsqrt(ms + EPS)) * amp + shift).astype(BF16)
        return carry

    lax.fori_loop(0, tm // MOD_ROWS, body, 0, unroll=MOD_UNROLL)


def _ffn_kernel(x_ref, gain_ref, sh_ref, sc_ref, gt_ref, wg_ref, wu_ref, wo_ref, o_ref, h_ref, *, nf):
    f = pl.program_id(1)

    @pl.when(f == 0)
    def _():
        _modulate_into(h_ref, x_ref, gain_ref, sh_ref, sc_ref)
        o_ref[...] = jnp.zeros_like(o_ref)

    h = h_ref[...]
    g = _dot(h, wg_ref[...])
    u = _dot(h, wu_ref[...])
    a = (jax.nn.silu(g) * u).astype(BF16)
    o_ref[...] += _dot(a, wo_ref[...])

    @pl.when(f == nf - 1)
    def _():
        o_ref[...] = x_ref[...] + (0.5 * gt_ref[...]) * o_ref[...]


def _row_spec(groups, tm, d):
    return pl.BlockSpec((None, 1, d), lambda i, j: (_seq_info(i * tm, groups)[0], 0, 0))


def _ffn(x, gain, shift, scale, gate, w_in, w_out, layer, groups):
    t, d = x.shape
    ff = w_out.shape[1]
    tm = min(TM_FFN, groups[0][1])
    tf = min(TF_FFN, ff)
    nf = ff // tf
    row = _row_spec(groups, tm, d)
    return pl.pallas_call(
        functools.partial(_ffn_kernel, nf=nf),
        out_shape=jax.ShapeDtypeStruct((t, d), F32),
        grid=(t // tm, nf),
        in_specs=[
            pl.BlockSpec((tm, d), lambda i, f: (i, 0)),
            pl.BlockSpec((1, d), lambda i, f: (0, 0)),
            row, row, row,
            pl.BlockSpec((None, d, tf), lambda i, f: (layer, 0, f)),
            pl.BlockSpec((None, d, tf), lambda i, f: (layer, 0, nf + f)),
            pl.BlockSpec((None, tf, d), lambda i, f: (layer, f, 0)),
        ],
        out_specs=pl.BlockSpec((tm, d), lambda i, f: (i, 0)),
        scratch_shapes=[pltpu.VMEM((tm, d), BF16)],
        compiler_params=_params("parallel", "arbitrary"),
        name="ffn",
    )(x, gain, shift, scale, gate, w_in, w_in, w_out)


def _proj_kernel(*refs, odd):
    if odd:
        x_ref, gain_ref, sh_ref, sc_ref, w_ref, wn_ref, cos_ref, sin_ref, o_ref, on_ref, h_ref = refs
    else:
        x_ref, gain_ref, sh_ref, sc_ref, w_ref, o_ref, h_ref = refs
    j = pl.program_id(1)

    @pl.when(j == 0)
    def _():
        _modulate_into(h_ref, x_ref, gain_ref, sh_ref, sc_ref)
        if odd:
            on_ref[...] = _dot(h_ref[...], wn_ref[...])

    y = _dot(h_ref[...], w_ref[...])
    if not odd:
        o_ref[...] = y.astype(BF16)
        return

    tn = o_ref.shape[1]
    rope_tiles = 2 * D_HALF // tn

    @pl.when(j < rope_tiles)
    def _():
        half = RET_D // 2
        cos = cos_ref[...]
        sin = sin_ref[...]
        k_scale = jnp.where(j >= rope_tiles // 2, RET_D ** -0.5, 1.0).astype(F32)
        for hd in range(tn // RET_D):
            x1 = y[:, hd * RET_D:hd * RET_D + half]
            x2 = y[:, hd * RET_D + half:(hd + 1) * RET_D]
            o_ref[:, hd * RET_D:hd * RET_D + half] = ((x1 * cos - x2 * sin) * k_scale).astype(BF16)
            o_ref[:, hd * RET_D + half:(hd + 1) * RET_D] = ((x1 * sin + x2 * cos) * k_scale).astype(BF16)

    @pl.when(j >= rope_tiles)
    def _():
        o_ref[...] = y.astype(BF16)


def _proj(x, gain, shift, scale, w, idx, n, groups, odd_extras=None):
    t, d = x.shape
    tm = min(TM_PROJ, groups[0][1])
    tn = TN_PROJ
    row = _row_spec(groups, tm, d)
    odd = odd_extras is not None
    in_specs = [
        pl.BlockSpec((tm, d), lambda i, j: (i, 0)),
        pl.BlockSpec((1, d), lambda i, j: (0, 0)),
        row, row,
        pl.BlockSpec((None, d, tn), lambda i, j: (idx, 0, j)),
    ]
    out_shape = jax.ShapeDtypeStruct((t, n), BF16)
    out_specs = pl.BlockSpec((tm, tn), lambda i, j: (i, j))
    args = (x, gain, shift, scale, w)
    if odd:
        half = RET_D // 2

        def pos_block(i, j):
            row0 = i * tm
            return ((row0 - _seq_info(row0, groups)[1]) // tm, 0)

        tab = pl.BlockSpec((tm, half), pos_block)
        in_specs += [pl.BlockSpec((None, d, LANES), lambda i, j: (idx, 0, 0)), tab, tab]
        out_shape = (out_shape, jax.ShapeDtypeStruct((t, LANES), F32))
        out_specs = (out_specs, pl.BlockSpec((tm, LANES), lambda i, j: (i, 0)))
        args = args + tuple(odd_extras)
    return pl.pallas_call(
        functools.partial(_proj_kernel, odd=odd),
        out_shape=out_shape,
        grid=(t // tm, n // tn),
        in_specs=in_specs,
        out_specs=out_specs,
        scratch_shapes=[pltpu.VMEM((tm, d), BF16)],
        compiler_params=_params("parallel", "arbitrary"),
        name="mix_proj",
    )(*args)


def _outproj_kernel(x_ref, ya_ref, yb_ref, gt_ref, wa_ref, wb_ref, o_ref):
    y = _dot(ya_ref[...], wa_ref[...]) + _dot(yb_ref[...], wb_ref[...])
    o_ref[...] = x_ref[...] + gt_ref[...] * y


def _outproj(x, ya, yb, ca, cb, gate, w, idx, groups):
    t, d = x.shape
    dh = d // 2
    tm = min(TM_OUT, groups[0][1])
    row = _row_spec(groups, tm, d)
    return pl.pallas_call(
        _outproj_kernel,
        out_shape=jax.ShapeDtypeStruct((t, d), F32),
        grid=(t // tm, 1),
        in_specs=[
            pl.BlockSpec((tm, d), lambda i, j: (i, 0)),
            pl.BlockSpec((tm, dh), lambda i, j: (i, ca)),
            pl.BlockSpec((tm, dh), lambda i, j: (i, cb)),
            row,
            pl.BlockSpec((None, dh, d), lambda i, j: (idx, 0, 0)),
            pl.BlockSpec((None, dh, d), lambda i, j: (idx, 1, 0)),
        ],
        out_specs=pl.BlockSpec((tm, d), lambda i, j: (i, 0)),
        compiler_params=_params("parallel", "arbitrary"),
        name="mix_out",
    )(x, ya, yb, gate, w, w)


def _final_kernel(x_ref, gain_ref, sh_ref, sc_ref, o0_ref, o1_ref, *, n0):
    y = _modulated(x_ref[...], gain_ref[...], sh_ref[...], sc_ref[...])
    i = pl.program_id(0)

    @pl.when(i < n0)
    def _():
        o0_ref[...] = y

    @pl.when(i >= n0)
    def _():
        o1_ref[...] = y


def _final(x, gain, shift, scale, groups):
    t, d = x.shape
    tm = min(TM_FINAL, groups[0][1])
    (b0, l0), (b1, l1) = groups
    n0 = b0 * l0 // tm
    row = _row_spec(groups, tm, d)
    return pl.pallas_call(
        functools.partial(_final_kernel, n0=n0),
        out_shape=(jax.ShapeDtypeStruct((b0 * l0, d), F32), jax.ShapeDtypeStruct((b1 * l1, d), F32)),
        grid=(t // tm, 1),
        in_specs=[
            pl.BlockSpec((tm, d), lambda i, j: (i, 0)),
            pl.BlockSpec((1, d), lambda i, j: (0, 0)),
            row, row,
        ],
        out_specs=(pl.BlockSpec((tm, d), lambda i, j: (jnp.minimum(i, n0 - 1), 0)),
                   pl.BlockSpec((tm, d), lambda i, j: (jnp.maximum(i - n0, 0), 0))),
        compiler_params=_params("arbitrary", "arbitrary"),
        name="final_mod",
    )(x, gain, shift, scale)


def _halo_specs(tl, width, nrows, col_of):
    per = tl // HALO
    last = nrows // HALO - 1
    prev = pl.BlockSpec((HALO, width), lambda i, *r: (jnp.maximum(i * per - 1, 0), col_of(i, *r)))
    nxt = pl.BlockSpec((HALO, width), lambda i, *r: (jnp.minimum((i + 1) * per, last), col_of(i, *r)))
    return prev, nxt


def _fill_ext(ext_ref, x_ref, prev_ref, next_ref, first, last, tl):
    ext_ref[HALO:HALO + tl, :] = x_ref[...].astype(F32)
    ext_ref[0:HALO, :] = jnp.where(first, 0.0, prev_ref[...].astype(F32))
    ext_ref[HALO + tl:2 * HALO + tl, :] = jnp.where(last, 0.0, next_ref[...].astype(F32))


def _even_kernel(xa_ref, prev_ref, next_ref, u_ref, v_ref, pw_ref, ps_ref, ng_ref, sw_ref, sb_ref,
                 o_ref, ext_ref, vn_ref, *, tl, groups):
    row0 = pl.program_id(0) * tl
    _, sstart, slen = _seq_info(row0, groups)
    pos0 = row0 - sstart
    _fill_ext(ext_ref, xa_ref, prev_ref, next_ref, pos0 == 0, pos0 + tl == slen, tl)

    t = pos0 + lax.broadcasted_iota(jnp.int32, (tl, 1), 0)
    for gi, w in enumerate(POOL_WINDOWS):
        c0 = gi * POOL_GROUP
        cols = slice(c0, c0 + POOL_GROUP)
        base = HALO - w // 2
        s = ext_ref[base:base + tl, cols]
        for dlt in range(1, w):
            s = s + ext_ref[base + dlt:base + dlt + tl, cols]
        lo = jnp.clip(t - w // 2, 0, slen)
        hi = jnp.clip(t + (w - w // 2), 0, slen)
        cnt = (hi - lo).astype(F32)
        pooled = (s / cnt - ext_ref[HALO:HALO + tl, cols]).astype(BF16)
        ya = _dot(pooled, pw_ref[gi]) * ps_ref[:, cols]
        o_ref[:, cols] = ya.astype(BF16)

    v = jax.nn.gelu(v_ref[...].astype(F32))
    vms = jnp.mean(v * v, axis=-1, keepdims=True)
    vn_ref[...] = ((v * lax.rsqrt(vms + EPS)) * ng_ref[...]).astype(BF16)
    for n in range(tl // SGU_CHUNK):
        rows = slice(n * SGU_CHUNK, (n + 1) * SGU_CHUNK)
        for g in range(SGU_GROUPS):
            cols = slice(g * SGU_HEAD, (g + 1) * SGU_HEAD)
            mixed = _dot(sw_ref[g], vn_ref[rows, cols]) + sb_ref[g]
            u = jax.nn.gelu(u_ref[rows, cols].astype(F32))
            o_ref[rows, D_HALF + g * SGU_HEAD:D_HALF + (g + 1) * SGU_HEAD] = (u * mixed).astype(BF16)


def _even_mix(proj, pool_w, pool_scale, sgu_norm, sgu_w, sgu_b, groups):
    t = proj.shape[0]
    tl = min(TL_EVEN, groups[0][1])
    prev, nxt = _halo_specs(tl, D_HALF, t, lambda i: 0)
    const2 = lambda i: (0, 0)
    const3 = lambda i: (0, 0, 0)
    return pl.pallas_call(
        functools.partial(_even_kernel, tl=tl, groups=groups),
        out_shape=jax.ShapeDtypeStruct((t, 2 * D_HALF), BF16),
        grid=(t // tl,),
        in_specs=[
            pl.BlockSpec((tl, D_HALF), lambda i: (i, 0)),
            prev, nxt,
            pl.BlockSpec((tl, D_HALF), lambda i: (i, 1)),
            pl.BlockSpec((tl, D_HALF), lambda i: (i, 2)),
            pl.BlockSpec(pool_w.shape, const3),
            pl.BlockSpec((1, D_HALF), const2),
            pl.BlockSpec((1, D_HALF), const2),
            pl.BlockSpec(sgu_w.shape, const3),
            pl.BlockSpec(sgu_b.shape, const3),
        ],
        out_specs=pl.BlockSpec((tl, 2 * D_HALF), lambda i: (i, 0)),
        scratch_shapes=[pltpu.VMEM((tl + 2 * HALO, D_HALF), F32), pltpu.VMEM((tl, D_HALF), BF16)],
        compiler_params=_params("parallel"),
        name="even_mix",
    )(proj, proj, proj, proj, proj, pool_w, pool_scale, sgu_norm, sgu_w, sgu_b)


def _rope_kernel(inv_ref, cos_ref, sin_ref, *, tl):
    pos = (pl.program_id(0) * tl + lax.broadcasted_iota(jnp.int32, (tl, 1), 0)).astype(F32)
    ang = pos * inv_ref[...]
    cos_ref[...] = jnp.cos(ang)
    sin_ref[...] = jnp.sin(ang)


def _rope_tables(max_len, tl):
    half = RET_D // 2
    inv = (1.0 / (ROPE_BASE ** jnp.linspace(0.0, 1.0, half, dtype=F32))).reshape(1, half)
    shp = jax.ShapeDtypeStruct((max_len, half), F32)
    return pl.pallas_call(
        functools.partial(_rope_kernel, tl=tl),
        out_shape=(shp, shp),
        grid=(max_len // tl,),
        in_specs=[pl.BlockSpec((1, half), lambda i: (0, 0))],
        out_specs=(pl.BlockSpec((tl, half), lambda i: (i, 0)), pl.BlockSpec((tl, half), lambda i: (i, 0))),
        compiler_params=_params("parallel"),
        name="rope_table",
    )(inv)


def _ret_kernel(*refs, tl, nt, groups, reverse):
    if reverse:
        q_ref, k_ref, v_ref, dec_ref, o_ref, s_ref = refs
    else:
        q_ref, k_ref, v_ref, dec_ref, decb_ref, g_ref, ob_ref, ng_ref, o_ref, s_ref = refs
    j = pl.program_id(1)
    it = nt - 1 - j if reverse else j
    row0 = it * tl
    _, sstart, slen = _seq_info(row0, groups)
    pos0 = row0 - sstart
    reset = (pos0 + tl == slen) if reverse else (pos0 == 0)

    @pl.when(reset)
    def _():
        s_ref[...] = jnp.zeros_like(s_ref)

    c = RET_CHUNK
    lg = jnp.log1p(-jnp.exp2(-dec_ref[...]))
    idx = lax.broadcasted_iota(jnp.int32, (c, 1), 0).astype(F32)
    if reverse:
        q_dec = jnp.exp(lg * (c - idx))
        k_dec = jnp.exp(lg * idx)
    else:
        q_dec = jnp.exp(lg * (idx + 1.0))
        k_dec = jnp.exp(lg * (c - 1.0 - idx))
        lgb = jnp.log1p(-jnp.exp2(-decb_ref[...]))
        ri = lax.broadcasted_iota(jnp.int32, (c, c), 0)
        ci = lax.broadcasted_iota(jnp.int32, (c, c), 1)
        rel = (ri - ci).astype(F32)
        dmat = (jnp.where(rel >= 0, jnp.exp(lg * jnp.maximum(rel, 0.0)), 0.0)
                + jnp.where(rel <= 0, jnp.exp(lgb * jnp.maximum(-rel, 0.0)), 0.0))
    chunk_dec = jnp.exp(lg * float(c))

    nc = tl // c
    order = range(nc - 1, -1, -1) if reverse else range(nc)
    for ch in order:
        rows = slice(ch * c, (ch + 1) * c)
        q = q_ref[rows, :]
        k = k_ref[rows, :]
        v = v_ref[rows, :]
        s = s_ref[...]
        inter = _dot(q, s.astype(BF16)) * q_dec
        s_ref[...] = s * chunk_dec + _dot_tn(k, (v.astype(F32) * k_dec).astype(BF16))
        if reverse:
            o_ref[rows, :] = inter
        else:
            scores = _dot_nt(q, k) * dmat
            o = _dot(scores.astype(BF16), v) + inter + ob_ref[rows, :]
            mu = jnp.mean(o, axis=-1, keepdims=True)
            var = jnp.mean(jnp.square(o - mu), axis=-1, keepdims=True)
            on = ((o - mu) * lax.rsqrt(var + EPS)) * ng_ref[...]
            o_ref[rows, :] = (jax.nn.silu(g_ref[rows, :].astype(F32)) * on).astype(BF16)


def _retention(proj, decay_f, decay_b, norm_g, groups):
    t = proj.shape[0]
    tl = min(TL_RET, groups[0][1])
    nt = t // tl
    hb = D_HALF // RET_D

    def specs(reverse):
        tile = (lambda j: nt - 1 - j) if reverse else (lambda j: j)
        sec = lambda s: pl.BlockSpec((tl, RET_D), lambda h, j: (tile(j), s * hb + h))
        dec = pl.BlockSpec((None, 1, 1), lambda h, j: (h, 0, 0))
        return sec, dec

    sec, dec = specs(True)
    ob = pl.pallas_call(
        functools.partial(_ret_kernel, tl=tl, nt=nt, groups=groups, reverse=True),
        out_shape=jax.ShapeDtypeStruct((t, D_HALF), F32),
        grid=(RET_HEADS, nt),
        in_specs=[sec(0), sec(1), sec(2), dec],
        out_specs=pl.BlockSpec((tl, RET_D), lambda h, j: (nt - 1 - j, h)),
        scratch_shapes=[pltpu.VMEM((RET_D, RET_D), F32)],
        compiler_params=_params("parallel", "arbitrary"),
        name="ret_bwd",
    )(proj, proj, proj, decay_b.reshape(RET_HEADS, 1, 1))

    sec, dec = specs(False)
    return pl.pallas_call(
        functools.partial(_ret_kernel, tl=tl, nt=nt, groups=groups, reverse=False),
        out_shape=jax.ShapeDtypeStruct((t, D_HALF), BF16),
        grid=(RET_HEADS, nt),
        in_specs=[sec(0), sec(1), sec(2), dec, dec, sec(3),
                  pl.BlockSpec((tl, RET_D), lambda h, j: (j, h)),
                  pl.BlockSpec((1, RET_D), lambda h, j: (0, h))],
        out_specs=pl.BlockSpec((tl, RET_D), lambda h, j: (j, h)),
        scratch_shapes=[pltpu.VMEM((RET_D, RET_D), F32)],
        compiler_params=_params("parallel", "arbitrary"),
        name="ret_fwd",
    )(proj, proj, proj, decay_f.reshape(RET_HEADS, 1, 1), decay_b.reshape(RET_HEADS, 1, 1),
      proj, ob, norm_g)


def _dnprep_kernel(x_ref, prev_ref, next_ref, w_ref, o_ref, ext_ref, *, tl, groups):
    row0 = pl.program_id(0) * tl
    part = pl.program_id(1)
    _, sstart, slen = _seq_info(row0, groups)
    pos0 = row0 - sstart
    _fill_ext(ext_ref, x_ref, prev_ref, next_ref, pos0 == 0, pos0 + tl == slen, tl)
    left = DN_CONV // 2
    q_scale = jnp.where(part == 0, DN_D ** -0.5, 1.0).astype(F32)
    for h in range(DN_HEADS):
        cols = slice(h * DN_D, (h + 1) * DN_D)
        conv = ext_ref[HALO - left:HALO - left + tl, cols] * w_ref[0:1, cols]
        for tap in range(1, DN_CONV):
            r0 = HALO - left + tap
            conv = conv + ext_ref[r0:r0 + tl, cols] * w_ref[tap:tap + 1, cols]
        y = jax.nn.silu(conv)
        inv_norm = lax.rsqrt(jnp.sum(y * y, axis=-1, keepdims=True) + EPS)
        o_ref[:, cols] = (y * jnp.where(part < 2, inv_norm * q_scale, 1.0)).astype(BF16)


def _dn_prep(proj, conv_w, groups):
    t = proj.shape[0]
    tl = min(TL_PREP, groups[0][1])
    base = P_RET // D_HALF
    prev, nxt = _halo_specs(tl, D_HALF, t, lambda i, part: base + part)
    return pl.pallas_call(
        functools.partial(_dnprep_kernel, tl=tl, groups=groups),
        out_shape=jax.ShapeDtypeStruct((t, 3 * D_HALF), BF16),
        grid=(t // tl, 3),
        in_specs=[pl.BlockSpec((tl, D_HALF), lambda i, part: (i, base + part)), prev, nxt,
                  pl.BlockSpec((DN_CONV, D_HALF), lambda i, part: (0, part))],
        out_specs=pl.BlockSpec((tl, D_HALF), lambda i, part: (i, part)),
        scratch_shapes=[pltpu.VMEM((tl + 2 * HALO, D_HALF), F32)],
        compiler_params=_params("parallel", "parallel"),
        name="dn_prep",
    )(proj, proj, proj, conv_w)


def _dot_hi_each(lhs, rhs):
    ls = [_split_bf16(a) for a in lhs]
    rs = [_split_bf16(b) for b in rhs]
    main = [_dot(ah, bh) for (ah, _), (bh, _) in zip(ls, rs)]
    cross = [_dot(ah, bl) + _dot(al, bh) for (ah, al), (bh, bl) in zip(ls, rs)]
    return [m + x for m, x in zip(main, cross)]


def _unit_triangular_inverses(mats, eye, ri, ci):
    size = SUBLANES
    same = (ri // size) == (ci // size)
    ps = [jnp.where(same, a, 0.0) for a in mats]
    invs = [eye - d for d in ps]
    n = 2
    while n < size:
        ps = _dot_hi_each(ps, ps)
        invs = [inv + x for inv, x in zip(invs, _dot_hi_each(invs, ps))]
        n *= 2
    while size < DN_CHUNK:
        size *= 2
        merged = (ri // size) == (ci // size)
        es = [jnp.where(merged & ~same, a, 0.0) for a in mats]
        invs = [inv - x for inv, x in zip(invs, _dot_hi_each(_dot_hi_each(invs, es), invs))]
        same = merged
    return invs


def _dn_kernel(*refs, tl, nt, groups, reverse):
    if reverse:
        q_ref, k_ref, v_ref, gates_ref, alog_ref, dtb_ref, o_ref, s_ref = refs
    else:
        (q_ref, k_ref, v_ref, gates_ref, alog_ref, dtb_ref, z_ref, ob_ref, ng_ref,
         o_ref, s_ref) = refs
    j = pl.program_id(0)
    it = nt - 1 - j if reverse else j
    row0 = it * tl
    _, sstart, slen = _seq_info(row0, groups)
    pos0 = row0 - sstart
    reset = (pos0 + tl == slen) if reverse else (pos0 == 0)

    @pl.when(reset)
    def _():
        s_ref[...] = jnp.zeros_like(s_ref)

    gates = gates_ref[...]
    beta_all = jax.nn.sigmoid(gates)
    la_all = -jnp.exp(alog_ref[...]) * jax.nn.softplus(gates + dtb_ref[...])
    cb0 = DN_HEADS if reverse else 0
    ca0 = cb0 + 2 * DN_HEADS

    c = DN_CHUNK
    ri = lax.broadcasted_iota(jnp.int32, (c, c), 0)
    ci = lax.broadcasted_iota(jnp.int32, (c, c), 1)
    eye = (ri == ci).astype(F32)
    incl = (ri <= ci) if reverse else (ri >= ci)
    strict = (ri < ci) if reverse else (ri > ci)
    incl_t = (ri >= ci) if reverse else (ri <= ci)
    last = 0 if reverse else c - 1

    nc = tl // c
    order = list(range(nc - 1, -1, -1) if reverse else range(nc))
    heads = range(DN_HEADS)
    units = [(ch, h) for ch in order for h in heads]

    def rows(ch):
        return slice(ch * c, (ch + 1) * c)

    def cols(h):
        return slice(h * DN_D, (h + 1) * DN_D)

    q = [q_ref[rows(ch), cols(h)] for ch, h in units]
    k = [k_ref[rows(ch), cols(h)] for ch, h in units]
    beta = [beta_all[rows(ch), cb0 + h:cb0 + h + 1] for ch, h in units]
    la = [la_all[rows(ch), ca0 + h:ca0 + h + 1] for ch, h in units]
    la_row = [jnp.sum(eye * x, axis=0, keepdims=True) for x in la]
    g_col = [jnp.sum(jnp.where(incl, x, 0.0), axis=1, keepdims=True) for x in la_row]
    g_row = [jnp.sum(jnp.where(incl_t, x, 0.0), axis=0, keepdims=True) for x in la]
    gam = [jnp.where(incl, jnp.exp(jnp.where(incl, gc - gr, 0.0)), 0.0) for gc, gr in zip(g_col, g_row)]
    eg = [jnp.exp(gc) for gc in g_col]
    g_last = [gc[last:last + 1, :] for gc in g_col]
    kb = [x.astype(F32) * b for x, b in zip(k, beta)]
    kq = [_dot_nt(jnp.concatenate([x.astype(BF16), y], axis=0), z) for x, y, z in zip(kb, q, k)]
    a = [jnp.where(strict, x[:c] * gm, 0.0) for x, gm in zip(kq, gam)]
    attn = [(x[c:] * gm).astype(BF16) for x, gm in zip(kq, gam)]
    tinv = _unit_triangular_inverses(a, eye, ri, ci)
    rhs = [jnp.concatenate([v_ref[rows(ch), cols(h)].astype(F32) * b, x * e], axis=1)
           for (ch, h), b, x, e in zip(units, beta, kb, eg)]
    uw = _dot_hi_each(tinv, rhs)
    tail = [jnp.exp(gl - gc) for gl, gc in zip(g_last, g_col)]
    dec = [jnp.exp(gl) for gl in g_last]

    for ci_, ch in enumerate(order):
        idx = [ci_ * DN_HEADS + h for h in heads]
        s = [s_ref[h] for h in heads]
        sb = [x.astype(BF16) for x in s]
        ws = [_dot(jnp.concatenate([uw[i][:, DN_D:].astype(BF16), q[i]], axis=0), sb[h])
              for h, i in zip(heads, idx)]
        v_new = [uw[i][:, :DN_D] - x[:c] for i, x in zip(idx, ws)]
        o = [x[c:] * eg[i] + _dot(attn[i], vn.astype(BF16)) for i, x, vn in zip(idx, ws, v_new)]
        for h, i in zip(heads, idx):
            s_ref[h] = s[h] * dec[i] + _dot_tn(k[i], (v_new[h] * tail[i]).astype(BF16))
        for h in heads:
            if reverse:
                o_ref[rows(ch), cols(h)] = o[h]
            else:
                oo = o[h] + ob_ref[rows(ch), cols(h)]
                on = (oo * lax.rsqrt(jnp.mean(oo * oo, axis=-1, keepdims=True) + EPS)) * ng_ref[...]
                z = z_ref[rows(ch), cols(h)].astype(F32)
                o_ref[rows(ch), cols(h)] = (on * jax.nn.silu(z)).astype(BF16)


def _deltanet(proj, gate_cols, qkv, alog_row, dtb_row, norm_g, groups):
    t = proj.shape[0]
    tl = min(TL_DN, groups[0][1])
    nt = t // tl
    z_blk = (P_RET + 3 * D_HALF) // D_HALF
    row = pl.BlockSpec((1, LANES), lambda j: (0, 0))
    state = pltpu.VMEM((DN_HEADS, DN_D, DN_D), F32)

    def common(reverse):
        tile = (lambda j: nt - 1 - j) if reverse else (lambda j: j)
        sec = lambda s: pl.BlockSpec((tl, D_HALF), lambda j: (tile(j), s))
        gates = pl.BlockSpec((tl, LANES), lambda j: (tile(j), 0))
        return sec, gates

    sec, gates = common(True)
    ob = pl.pallas_call(
        functools.partial(_dn_kernel, tl=tl, nt=nt, groups=groups, reverse=True),
        out_shape=jax.ShapeDtypeStruct((t, D_HALF), F32),
        grid=(nt,),
        in_specs=[sec(0), sec(1), sec(2), gates, row, row],
        out_specs=sec(0),
        scratch_shapes=[state],
        compiler_params=_params("arbitrary"),
        name="dn_bwd",
    )(qkv, qkv, qkv, gate_cols, alog_row, dtb_row)

    sec, gates = common(False)
    return pl.pallas_call(
        functools.partial(_dn_kernel, tl=tl, nt=nt, groups=groups, reverse=False),
        out_shape=jax.ShapeDtypeStruct((t, D_HALF), BF16),
        grid=(nt,),
        in_specs=[sec(0), sec(1), sec(2), gates, row, row, sec(z_blk), sec(0), row],
        out_specs=sec(0),
        scratch_shapes=[state],
        compiler_params=_params("arbitrary"),
        name="dn_fwd",
    )(qkv, qkv, qkv, gate_cols, alog_row, dtb_row, proj, ob, norm_g)


def _gate_row(f_vals, b_vals):
    row = jnp.zeros((LANES,), F32)
    row = row.at[2 * DN_HEADS:3 * DN_HEADS].set(f_vals.astype(F32))
    row = row.at[3 * DN_HEADS:4 * DN_HEADS].set(b_vals.astype(F32))
    return row.reshape(1, LANES)


def _trunk(x, c, p, groups):
    t, d = x.shape
    n_seq = c.shape[0]
    c_pad = jnp.zeros((SEQ_PAD, d), F32).at[:n_seq].set(c)
    mods = _ada(c_pad, p['w_ada'], p['b_ada'])
    mods = mods.reshape(DEPTH, SEQ_PAD, N_MOD, 1, d).transpose(0, 2, 1, 3, 4)
    fin = _ada(c_pad, p['w_ada_final'][None], p['b_ada_final'][None])
    fin = fin.reshape(SEQ_PAD, 2, 1, d).transpose(1, 0, 2, 3)

    max_len = max(ln for _, ln in groups)
    cos, sin = _rope_tables(max_len, min(TL_RET, groups[0][1]))

    row = lambda a: a.reshape(1, -1)
    for layer in range(DEPTH):
        sh1, sc1, g1, sh2, sc2, g2, sh3, sc3, g3 = [mods[layer, jm] for jm in range(N_MOD)]
        x = _ffn(x, row(p['norm_ffn1'][layer]), sh1, sc1, g1, p['w_ffn1_in'], p['w_ffn1_out'], layer, groups)
        idx = layer // 2
        gain = row(p['norm_mix'][layer])
        if layer % 2 == 0:
            proj = _proj(x, gain, sh2, sc2, p['w_in_even'], idx, P_EVEN, groups)
            y = _even_mix(proj, p['pool_w'][idx], row(p['pool_scale'][idx]), row(p['sgu_norm'][idx]),
                          p['sgu_w'][idx], p['sgu_b'][idx][..., None], groups)
            x = _outproj(x, y, y, 0, 1, g2, p['w_out_even'], idx, groups)
        else:
            proj, gate_cols = _proj(x, gain, sh2, sc2, p['w_in_odd'], idx, P_ODD_MAIN, groups,
                                    odd_extras=(p['w_in_odd_gates'], cos, sin))
            yc = _retention(proj, p['ret_decay_f'][idx], p['ret_decay_b'][idx],
                            row(p['ret_norm'][idx]), groups)
            qkv = _dn_prep(proj, p['dn_conv'][idx], groups)
            yd = _deltanet(proj, gate_cols, qkv,
                           _gate_row(p['dn_a_log_f'][idx], p['dn_a_log_b'][idx]),
                           _gate_row(p['dn_dt_bias_f'][idx], p['dn_dt_bias_b'][idx]),
                           row(p['dn_norm'][idx]), groups)
            x = _outproj(x, yc, yd, 0, 0, g2, p['w_out_odd'], idx, groups)
        x = _ffn(x, row(p['norm_ffn2'][layer]), sh3, sc3, g3, p['w_ffn2_in'], p['w_ffn2_out'], layer, groups)
    return _final(x, row(p['norm_final']), fin[0], fin[1], groups)


def _prepare(p):
    q = dict(p)
    for name in ('w_ffn1_in', 'w_ffn1_out', 'w_ffn2_in', 'w_ffn2_out', 'w_in_even', 'w_out_even',
                 'pool_w', 'sgu_w', 'w_in_odd', 'w_out_odd'):
        q[name] = p[name].astype(BF16)
    gates = q['w_in_odd'][:, :, P_ODD_MAIN:]
    q['w_in_odd_gates'] = jnp.pad(gates, ((0, 0), (0, 0), (0, LANES - gates.shape[-1])))
    return q


def kernel(x_prompt, x_sample, c_prompt, c_sample, w_ada, b_ada, norm_ffn1, w_ffn1_in, w_ffn1_out, norm_mix, norm_ffn2, w_ffn2_in, w_ffn2_out, w_in_even, w_out_even, pool_w, pool_scale, sgu_norm, sgu_w, sgu_b, w_in_odd, w_out_odd, ret_decay_f, ret_decay_b, ret_norm, dn_conv, dn_a_log_f, dn_a_log_b, dn_dt_bias_f, dn_dt_bias_b, dn_norm, norm_final, w_ada_final, b_ada_final):
    p = _prepare({
        'w_ada': w_ada, 'b_ada': b_ada, 'norm_ffn1': norm_ffn1, 'w_ffn1_in': w_ffn1_in,
        'w_ffn1_out': w_ffn1_out, 'norm_mix': norm_mix, 'norm_ffn2': norm_ffn2,
        'w_ffn2_in': w_ffn2_in, 'w_ffn2_out': w_ffn2_out, 'w_in_even': w_in_even,
        'w_out_even': w_out_even, 'pool_w': pool_w, 'pool_scale': pool_scale,
        'sgu_norm': sgu_norm, 'sgu_w': sgu_w, 'sgu_b': sgu_b, 'w_in_odd': w_in_odd,
        'w_out_odd': w_out_odd, 'ret_decay_f': ret_decay_f, 'ret_decay_b': ret_decay_b,
        'ret_norm': ret_norm, 'dn_conv': dn_conv, 'dn_a_log_f': dn_a_log_f,
        'dn_a_log_b': dn_a_log_b, 'dn_dt_bias_f': dn_dt_bias_f, 'dn_dt_bias_b': dn_dt_bias_b,
        'dn_norm': dn_norm, 'norm_final': norm_final, 'w_ada_final': w_ada_final,
        'b_ada_final': b_ada_final,
    })
    bp, lp, d = x_prompt.shape
    bs, ls, _ = x_sample.shape
    groups = ((bp, lp), (bs, ls))
    x = jnp.concatenate([x_prompt.reshape(bp * lp, d), x_sample.reshape(bs * ls, d)], axis=0)
    c = jnp.concatenate([c_prompt, c_sample], axis=0)
    y_prompt, y_sample = _trunk(x, c, p, groups)
    return (y_prompt.reshape(bp, lp, d), y_sample.reshape(bs, ls, d))
```

```python
import functools

import jax
import jax.numpy as jnp
import numpy as np
from jax import lax
from jax.experimental import pallas as pl
from jax.experimental.pallas import tpu as pltpu

F32 = jnp.float32
BF16 = jnp.bfloat16

D_MODEL = 2048
DEPTH = 4
D_HALF = D_MODEL // 2
POOL_WINDOWS = (2, 4, 8, 16)
POOL_GROUP = D_HALF // len(POOL_WINDOWS)
SGU_CHUNK = 128
SGU_HEAD = 128
SGU_GROUPS = D_HALF // SGU_HEAD
RET_HEADS = 4
RET_D = D_HALF // RET_HEADS
RET_CHUNK = 128
ROPE_BASE = 10000.0
DN_HEADS = 8
DN_D = D_HALF // DN_HEADS
DN_CONV = 4
DN_CHUNK = 64
D_FF = 5632
N_MOD = 9
EPS = 1e-6
P_EVEN = 3 * D_HALF
P_RET = 4 * D_HALF
P_ODD_MAIN = P_RET + 4 * D_HALF
P_ODD = P_ODD_MAIN + 4 * DN_HEADS

LANES = 128
SUBLANES = 8
HALO = 2 * SUBLANES
VMEM_LIMIT = 56 * 1024 * 1024

TM_FFN = 1024
TF_FFN = 512
TM_PROJ = 1024
TN_PROJ = 1024
MOD_ROWS = 16
MOD_UNROLL = 8
TM_OUT = 512
TL_EVEN = 256
TL_RET = 512
TL_DN = 256
TL_PREP = 256
TM_FINAL = 512
TN_ADA = 1024
SEQ_PAD = 16


def _seq_info(row0, groups):
    seq = start = length = None
    t0 = s0 = 0
    for gi, (nb, ln) in enumerate(groups):
        rel = row0 - t0
        q = rel // ln
        if gi == 0:
            seq, start, length = q, q * ln, ln
        else:
            here = row0 >= t0
            seq = jnp.where(here, s0 + q, seq)
            start = jnp.where(here, t0 + q * ln, start)
            length = jnp.where(here, ln, length)
        t0 += nb * ln
        s0 += nb
    return seq, start, length


def _params(*sem):
    return pltpu.CompilerParams(dimension_semantics=sem, vmem_limit_bytes=VMEM_LIMIT)


def _modulated(x, gain, shift, scale):
    ms = jnp.mean(x * x, axis=-1, keepdims=True)
    y = x * lax.rsqrt(ms + EPS)
    return (y * gain) * (1.0 + scale) + shift


def _dot(a, b):
    return jnp.dot(a, b, preferred_element_type=F32)


def _dot_nt(a, b):
    return lax.dot_general(a, b, (((1,), (1,)), ((), ())), preferred_element_type=F32)


def _dot_tn(a, b):
    return lax.dot_general(a, b, (((0,), (0,)), ((), ())), preferred_element_type=F32)


def _split_bf16(a):
    hi = a.astype(BF16)
    lo = (a - hi.astype(F32)).astype(BF16)
    return hi, lo


def _dot_hi(a, b):
    ah, al = _split_bf16(a)
    bh, bl = _split_bf16(b)
    return _dot(ah, bh) + (_dot(ah, bl) + _dot(al, bh))


def _ada_kernel(c_ref, w_ref, b_ref, o_ref):
    c = c_ref[...]
    act = jax.nn.silu(c).astype(BF16)
    o_ref[...] = _dot(act, w_ref[...].astype(BF16)) + b_ref[...]


def _ada(c_pad, w, b):
    ly, d, n = w.shape
    s = c_pad.shape[0]
    tn = min(TN_ADA, n)
    return pl.pallas_call(
        _ada_kernel,
        out_shape=jax.ShapeDtypeStruct((ly, s, n), F32),
        grid=(ly, n // tn),
        in_specs=[
            pl.BlockSpec((s, d), lambda l, j: (0, 0)),
            pl.BlockSpec((None, d, tn), lambda l, j: (l, 0, j)),
            pl.BlockSpec((None, 1, tn), lambda l, j: (l, 0, j)),
        ],
        out_specs=pl.BlockSpec((None, s, tn), lambda l, j: (l, 0, j)),
        compiler_params=_params("parallel", "parallel"),
        name="ada_rows",
    )(c_pad, w, b.reshape(ly, 1, n))


def _modulate_into(h_ref, x_ref, gain_ref, sh_ref, sc_ref):
    tm = x_ref.shape[0]
    amp = gain_ref[...] * (1.0 + sc_ref[...])
    shift = sh_ref[...]

    def body(r, carry):
        rows = pl.ds(pl.multiple_of(r * MOD_ROWS, MOD_ROWS), MOD_ROWS)
        x = x_ref[rows, :]
        ms = jnp.mean(x * x, axis=-1, keepdims=True)
        h_ref[rows, :] = ((x * lax.rsqrt(ms + EPS)) * amp + shift).astype(BF16)
        return carry

    lax.fori_loop(0, tm // MOD_ROWS, body, 0, unroll=MOD_UNROLL)


def _ffn_kernel(x_ref, gain_ref, sh_ref, sc_ref, gt_ref, wg_ref, wu_ref, wo_ref, o_ref, h_ref, *, nf):
    f = pl.program_id(1)

    @pl.when(f == 0)
    def _():
        _modulate_into(h_ref, x_ref, gain_ref, sh_ref, sc_ref)
        o_ref[...] = jnp.zeros_like(o_ref)

    h = h_ref[...]
    g = _dot(h, wg_ref[...])
    u = _dot(h, wu_ref[...])
    a = (jax.nn.silu(g) * u).astype(BF16)
    o_ref[...] += _dot(a, wo_ref[...])

    @pl.when(f == nf - 1)
    def _():
        o_ref[...] = x_ref[...] + (0.5 * gt_ref[...]) * o_ref[...]


def _row_spec(groups, tm, d):
    return pl.BlockSpec((None, 1, d), lambda i, j: (_seq_info(i * tm, groups)[0], 0, 0))


def _ffn(x, gain, shift, scale, gate, w_in, w_out, layer, groups):
    t, d = x.shape
    ff = w_out.shape[1]
    tm = min(TM_FFN, groups[0][1])
    tf = min(TF_FFN, ff)
    nf = ff // tf
    row = _row_spec(groups, tm, d)
    return pl.pallas_call(
        functools.partial(_ffn_kernel, nf=nf),
        out_shape=jax.ShapeDtypeStruct((t, d), F32),
        grid=(t // tm, nf),
        in_specs=[
            pl.BlockSpec((tm, d), lambda i, f: (i, 0)),
            pl.BlockSpec((1, d), lambda i, f: (0, 0)),
            row, row, row,
            pl.BlockSpec((None, d, tf), lambda i, f: (layer, 0, f)),
            pl.BlockSpec((None, d, tf), lambda i, f: (layer, 0, nf + f)),
            pl.BlockSpec((None, tf, d), lambda i, f: (layer, f, 0)),
        ],
        out_specs=pl.BlockSpec((tm, d), lambda i, f: (i, 0)),
        scratch_shapes=[pltpu.VMEM((tm, d), BF16)],
        compiler_params=_params("parallel", "arbitrary"),
        name="ffn",
    )(x, gain, shift, scale, gate, w_in, w_in, w_out)


def _proj_kernel(*refs, odd):
    if odd:
        x_ref, gain_ref, sh_ref, sc_ref, w_ref, wn_ref, cos_ref, sin_ref, o_ref, on_ref, h_ref = refs
    else:
        x_ref, gain_ref, sh_ref, sc_ref, w_ref, o_ref, h_ref = refs
    j = pl.program_id(1)

    @pl.when(j == 0)
    def _():
        _modulate_into(h_ref, x_ref, gain_ref, sh_ref, sc_ref)
        if odd:
            on_ref[...] = _dot(h_ref[...], wn_ref[...])

    y = _dot(h_ref[...], w_ref[...])
    if not odd:
        o_ref[...] = y.astype(BF16)
        return

    tn = o_ref.shape[1]
    rope_tiles = 2 * D_HALF // tn

    @pl.when(j < rope_tiles)
    def _():
        half = RET_D // 2
        cos = cos_ref[...]
        sin = sin_ref[...]
        k_scale = jnp.where(j >= rope_tiles // 2, RET_D ** -0.5, 1.0).astype(F32)
        for hd in range(tn // RET_D):
            x1 = y[:, hd * RET_D:hd * RET_D + half]
            x2 = y[:, hd * RET_D + half:(hd + 1) * RET_D]
            o_ref[:, hd * RET_D:hd * RET_D + half] = ((x1 * cos - x2 * sin) * k_scale).astype(BF16)
            o_ref[:, hd * RET_D + half:(hd + 1) * RET_D] = ((x1 * sin + x2 * cos) * k_scale).astype(BF16)

    @pl.when(j >= rope_tiles)
    def _():
        o_ref[...] = y.astype(BF16)


def _proj(x, gain, shift, scale, w, idx, n, groups, odd_extras=None):
    t, d = x.shape
    tm = min(TM_PROJ, groups[0][1])
    tn = TN_PROJ
    row = _row_spec(groups, tm, d)
    odd = odd_extras is not None
    in_specs = [
        pl.BlockSpec((tm, d), lambda i, j: (i, 0)),
        pl.BlockSpec((1, d), lambda i, j: (0, 0)),
        row, row,
        pl.BlockSpec((None, d, tn), lambda i, j: (idx, 0, j)),
    ]
    out_shape = jax.ShapeDtypeStruct((t, n), BF16)
    out_specs = pl.BlockSpec((tm, tn), lambda i, j: (i, j))
    args = (x, gain, shift, scale, w)
    if odd:
        half = RET_D // 2

        def pos_block(i, j):
            row0 = i * tm
            return ((row0 - _seq_info(row0, groups)[1]) // tm, 0)

        tab = pl.BlockSpec((tm, half), pos_block)
        in_specs += [pl.BlockSpec((None, d, LANES), lambda i, j: (idx, 0, 0)), tab, tab]
        out_shape = (out_shape, jax.ShapeDtypeStruct((t, LANES), F32))
        out_specs = (out_specs, pl.BlockSpec((tm, LANES), lambda i, j: (i, 0)))
        args = args + tuple(odd_extras)
    return pl.pallas_call(
        functools.partial(_proj_kernel, odd=odd),
        out_shape=out_shape,
        grid=(t // tm, n // tn),
        in_specs=in_specs,
        out_specs=out_specs,
        scratch_shapes=[pltpu.VMEM((tm, d), BF16)],
        compiler_params=_params("parallel", "arbitrary"),
        name="mix_proj",
    )(*args)


def _outproj_kernel(x_ref, ya_ref, yb_ref, gt_ref, wa_ref, wb_ref, o_ref):
    y = _dot(ya_ref[...], wa_ref[...]) + _dot(yb_ref[...], wb_ref[...])
    o_ref[...] = x_ref[...] + gt_ref[...] * y


def _outproj(x, ya, yb, ca, cb, gate, w, idx, groups):
    t, d = x.shape
    dh = d // 2
    tm = min(TM_OUT, groups[0][1])
    row = _row_spec(groups, tm, d)
    return pl.pallas_call(
        _outproj_kernel,
        out_shape=jax.ShapeDtypeStruct((t, d), F32),
        grid=(t // tm, 1),
        in_specs=[
            pl.BlockSpec((tm, d), lambda i, j: (i, 0)),
            pl.BlockSpec((tm, dh), lambda i, j: (i, ca)),
            pl.BlockSpec((tm, dh), lambda i, j: (i, cb)),
            row,
            pl.BlockSpec((None, dh, d), lambda i, j: (idx, 0, 0)),
            pl.BlockSpec((None, dh, d), lambda i, j: (idx, 1, 0)),
        ],
        out_specs=pl.BlockSpec((tm, d), lambda i, j: (i, 0)),
        compiler_params=_params("parallel", "arbitrary"),
        name="mix_out",
    )(x, ya, yb, gate, w, w)


def _final_kernel(x_ref, gain_ref, sh_ref, sc_ref, o0_ref, o1_ref, *, n0):
    y = _modulated(x_ref[...], gain_ref[...], sh_ref[...], sc_ref[...])
    i = pl.program_id(0)

    @pl.when(i < n0)
    def _():
        o0_ref[...] = y

    @pl.when(i >= n0)
    def _():
        o1_ref[...] = y


def _final(x, gain, shift, scale, groups):
    t, d = x.shape
    tm = min(TM_FINAL, groups[0][1])
    (b0, l0), (b1, l1) = groups
    n0 = b0 * l0 // tm
    row = _row_spec(groups, tm, d)
    return pl.pallas_call(
        functools.partial(_final_kernel, n0=n0),
        out_shape=(jax.ShapeDtypeStruct((b0 * l0, d), F32), jax.ShapeDtypeStruct((b1 * l1, d), F32)),
        grid=(t // tm, 1),
        in_specs=[
            pl.BlockSpec((tm, d), lambda i, j: (i, 0)),
            pl.BlockSpec((1, d), lambda i, j: (0, 0)),
            row, row,
        ],
        out_specs=(pl.BlockSpec((tm, d), lambda i, j: (jnp.minimum(i, n0 - 1), 0)),
                   pl.BlockSpec((tm, d), lambda i, j: (jnp.maximum(i - n0, 0), 0))),
        compiler_params=_params("arbitrary", "arbitrary"),
        name="final_mod",
    )(x, gain, shift, scale)


def _halo_specs(tl, width, nrows, col_of):
    per = tl // HALO
    last = nrows // HALO - 1
    prev = pl.BlockSpec((HALO, width), lambda i, *r: (jnp.maximum(i * per - 1, 0), col_of(i, *r)))
    nxt = pl.BlockSpec((HALO, width), lambda i, *r: (jnp.minimum((i + 1) * per, last), col_of(i, *r)))
    return prev, nxt


def _fill_ext(ext_ref, x_ref, prev_ref, next_ref, first, last, tl):
    ext_ref[HALO:HALO + tl, :] = x_ref[...].astype(F32)
    ext_ref[0:HALO, :] = jnp.where(first, 0.0, prev_ref[...].astype(F32))
    ext_ref[HALO + tl:2 * HALO + tl, :] = jnp.where(last, 0.0, next_ref[...].astype(F32))


def _even_kernel(xa_ref, prev_ref, next_ref, u_ref, v_ref, pw_ref, ps_ref, ng_ref, sw_ref, sb_ref,
                 o_ref, ext_ref, vn_ref, *, tl, groups):
    row0 = pl.program_id(0) * tl
    _, sstart, slen = _seq_info(row0, groups)
    pos0 = row0 - sstart
    _fill_ext(ext_ref, xa_ref, prev_ref, next_ref, pos0 == 0, pos0 + tl == slen, tl)

    t = pos0 + lax.broadcasted_iota(jnp.int32, (tl, 1), 0)
    for gi, w in enumerate(POOL_WINDOWS):
        c0 = gi * POOL_GROUP
        cols = slice(c0, c0 + POOL_GROUP)
        base = HALO - w // 2
        s = ext_ref[base:base + tl, cols]
        for dlt in range(1, w):
            s = s + ext_ref[base + dlt:base + dlt + tl, cols]
        lo = jnp.clip(t - w // 2, 0, slen)
        hi = jnp.clip(t + (w - w // 2), 0, slen)
        cnt = (hi - lo).astype(F32)
        pooled = (s / cnt - ext_ref[HALO:HALO + tl, cols]).astype(BF16)
        ya = _dot(pooled, pw_ref[gi]) * ps_ref[:, cols]
        o_ref[:, cols] = ya.astype(BF16)

    v = jax.nn.gelu(v_ref[...].astype(F32))
    vms = jnp.mean(v * v, axis=-1, keepdims=True)
    vn_ref[...] = ((v * lax.rsqrt(vms + EPS)) * ng_ref[...]).astype(BF16)
    for n in range(tl // SGU_CHUNK):
        rows = slice(n * SGU_CHUNK, (n + 1) * SGU_CHUNK)
        for g in range(SGU_GROUPS):
            cols = slice(g * SGU_HEAD, (g + 1) * SGU_HEAD)
            mixed = _dot(sw_ref[g], vn_ref[rows, cols]) + sb_ref[g]
            u = jax.nn.gelu(u_ref[rows, cols].astype(F32))
            o_ref[rows, D_HALF + g * SGU_HEAD:D_HALF + (g + 1) * SGU_HEAD] = (u * mixed).astype(BF16)


def _even_mix(proj, pool_w, pool_scale, sgu_norm, sgu_w, sgu_b, groups):
    t = proj.shape[0]
    tl = min(TL_EVEN, groups[0][1])
    prev, nxt = _halo_specs(tl, D_HALF, t, lambda i: 0)
    const2 = lambda i: (0, 0)
    const3 = lambda i: (0, 0, 0)
    return pl.pallas_call(
        functools.partial(_even_kernel, tl=tl, groups=groups),
        out_shape=jax.ShapeDtypeStruct((t, 2 * D_HALF), BF16),
        grid=(t // tl,),
        in_specs=[
            pl.BlockSpec((tl, D_HALF), lambda i: (i, 0)),
            prev, nxt,
            pl.BlockSpec((tl, D_HALF), lambda i: (i, 1)),
            pl.BlockSpec((tl, D_HALF), lambda i: (i, 2)),
            pl.BlockSpec(pool_w.shape, const3),
            pl.BlockSpec((1, D_HALF), const2),
            pl.BlockSpec((1, D_HALF), const2),
            pl.BlockSpec(sgu_w.shape, const3),
            pl.BlockSpec(sgu_b.shape, const3),
        ],
        out_specs=pl.BlockSpec((tl, 2 * D_HALF), lambda i: (i, 0)),
        scratch_shapes=[pltpu.VMEM((tl + 2 * HALO, D_HALF), F32), pltpu.VMEM((tl, D_HALF), BF16)],
        compiler_params=_params("parallel"),
        name="even_mix",
    )(proj, proj, proj, proj, proj, pool_w, pool_scale, sgu_norm, sgu_w, sgu_b)


def _rope_kernel(inv_ref, cos_ref, sin_ref, *, tl):
    pos = (pl.program_id(0) * tl + lax.broadcasted_iota(jnp.int32, (tl, 1), 0)).astype(F32)
    ang = pos * inv_ref[...]
    cos_ref[...] = jnp.cos(ang)
    sin_ref[...] = jnp.sin(ang)


def _rope_tables(max_len, tl):
    half = RET_D // 2
    inv = (1.0 / (ROPE_BASE ** jnp.linspace(0.0, 1.0, half, dtype=F32))).reshape(1, half)
    shp = jax.ShapeDtypeStruct((max_len, half), F32)
    return pl.pallas_call(
        functools.partial(_rope_kernel, tl=tl),
        out_shape=(shp, shp),
        grid=(max_len // tl,),
        in_specs=[pl.BlockSpec((1, half), lambda i: (0, 0))],
        out_specs=(pl.BlockSpec((tl, half), lambda i: (i, 0)), pl.BlockSpec((tl, half), lambda i: (i, 0))),
        compiler_params=_params("parallel"),
        name="rope_table",
    )(inv)


def _ret_kernel(*refs, tl, nt, groups, reverse):
    if reverse:
        q_ref, k_ref, v_ref, dec_ref, o_ref, s_ref = refs
    else:
        q_ref, k_ref, v_ref, dec_ref, decb_ref, g_ref, ob_ref, ng_ref, o_ref, s_ref = refs
    j = pl.program_id(1)
    it = nt - 1 - j if reverse else j
    row0 = it * tl
    _, sstart, slen = _seq_info(row0, groups)
    pos0 = row0 - sstart
    reset = (pos0 + tl == slen) if reverse else (pos0 == 0)

    @pl.when(reset)
    def _():
        s_ref[...] = jnp.zeros_like(s_ref)

    c = RET_CHUNK
    lg = jnp.log1p(-jnp.exp2(-dec_ref[...]))
    idx = lax.broadcasted_iota(jnp.int32, (c, 1), 0).astype(F32)
    if reverse:
        q_dec = jnp.exp(lg * (c - idx))
        k_dec = jnp.exp(lg * idx)
    else:
        q_dec = jnp.exp(lg * (idx + 1.0))
        k_dec = jnp.exp(lg * (c - 1.0 - idx))
        lgb = jnp.log1p(-jnp.exp2(-decb_ref[...]))
        ri = lax.broadcasted_iota(jnp.int32, (c, c), 0)
        ci = lax.broadcasted_iota(jnp.int32, (c, c), 1)
        rel = (ri - ci).astype(F32)
        dmat = (jnp.where(rel >= 0, jnp.exp(lg * jnp.maximum(rel, 0.0)), 0.0)
                + jnp.where(rel <= 0, jnp.exp(lgb * jnp.maximum(-rel, 0.0)), 0.0))
    chunk_dec = jnp.exp(lg * float(c))

    nc = tl // c
    order = range(nc - 1, -1, -1) if reverse else range(nc)
    for ch in order:
        rows = slice(ch * c, (ch + 1) * c)
        q = q_ref[rows, :]
        k = k_ref[rows, :]
        v = v_ref[rows, :]
        s = s_ref[...]
        inter = _dot(q, s.astype(BF16)) * q_dec
        s_ref[...] = s * chunk_dec + _dot_tn(k, (v.astype(F32) * k_dec).astype(BF16))
        if reverse:
            o_ref[rows, :] = inter
        else:
            scores = _dot_nt(q, k) * dmat
            o = _dot(scores.astype(BF16), v) + inter + ob_ref[rows, :]
            mu = jnp.mean(o, axis=-1, keepdims=True)
            var = jnp.mean(jnp.square(o - mu), axis=-1, keepdims=True)
            on = ((o - mu) * lax.rsqrt(var + EPS)) * ng_ref[...]
            o_ref[rows, :] = (jax.nn.silu(g_ref[rows, :].astype(F32)) * on).astype(BF16)


def _retention(proj, decay_f, decay_b, norm_g, groups):
    t = proj.shape[0]
    tl = min(TL_RET, groups[0][1])
    nt = t // tl
    hb = D_HALF // RET_D

    def specs(reverse):
        tile = (lambda j: nt - 1 - j) if reverse else (lambda j: j)
        sec = lambda s: pl.BlockSpec((tl, RET_D), lambda h, j: (tile(j), s * hb + h))
        dec = pl.BlockSpec((None, 1, 1), lambda h, j: (h, 0, 0))
        return sec, dec

    sec, dec = specs(True)
    ob = pl.pallas_call(
        functools.partial(_ret_kernel, tl=tl, nt=nt, groups=groups, reverse=True),
        out_shape=jax.ShapeDtypeStruct((t, D_HALF), F32),
        grid=(RET_HEADS, nt),
        in_specs=[sec(0), sec(1), sec(2), dec],
        out_specs=pl.BlockSpec((tl, RET_D), lambda h, j: (nt - 1 - j, h)),
        scratch_shapes=[pltpu.VMEM((RET_D, RET_D), F32)],
        compiler_params=_params("parallel", "arbitrary"),
        name="ret_bwd",
    )(proj, proj, proj, decay_b.reshape(RET_HEADS, 1, 1))

    sec, dec = specs(False)
    return pl.pallas_call(
        functools.partial(_ret_kernel, tl=tl, nt=nt, groups=groups, reverse=False),
        out_shape=jax.ShapeDtypeStruct((t, D_HALF), BF16),
        grid=(RET_HEADS, nt),
        in_specs=[sec(0), sec(1), sec(2), dec, dec, sec(3),
                  pl.BlockSpec((tl, RET_D), lambda h, j: (j, h)),
                  pl.BlockSpec((1, RET_D), lambda h, j: (0, h))],
        out_specs=pl.BlockSpec((tl, RET_D), lambda h, j: (j, h)),
        scratch_shapes=[pltpu.VMEM((RET_D, RET_D), F32)],
        compiler_params=_params("parallel", "arbitrary"),
        name="ret_fwd",
    )(proj, proj, proj, decay_f.reshape(RET_HEADS, 1, 1), decay_b.reshape(RET_HEADS, 1, 1),
      proj, ob, norm_g)


def _dnprep_kernel(x_ref, prev_ref, next_ref, w_ref, o_ref, ext_ref, *, tl, groups):
    row0 = pl.program_id(0) * tl
    part = pl.program_id(1)
    _, sstart, slen = _seq_info(row0, groups)
    pos0 = row0 - sstart
    _fill_ext(ext_ref, x_ref, prev_ref, next_ref, pos0 == 0, pos0 + tl == slen, tl)
    left = DN_CONV // 2
    q_scale = jnp.where(part == 0, DN_D ** -0.5, 1.0).astype(F32)
    for h in range(DN_HEADS):
        cols = slice(h * DN_D, (h + 1) * DN_D)
        conv = ext_ref[HALO - left:HALO - left + tl, cols] * w_ref[0:1, cols]
        for tap in range(1, DN_CONV):
            r0 = HALO - left + tap
            conv = conv + ext_ref[r0:r0 + tl, cols] * w_ref[tap:tap + 1, cols]
        y = jax.nn.silu(conv)
        inv_norm = lax.rsqrt(jnp.sum(y * y, axis=-1, keepdims=True) + EPS)
        o_ref[:, cols] = (y * jnp.where(part < 2, inv_norm * q_scale, 1.0)).astype(BF16)


def _dn_prep(proj, conv_w, groups):
    t = proj.shape[0]
    tl = min(TL_PREP, groups[0][1])
    base = P_RET // D_HALF
    prev, nxt = _halo_specs(tl, D_HALF, t, lambda i, part: base + part)
    return pl.pallas_call(
        functools.partial(_dnprep_kernel, tl=tl, groups=groups),
        out_shape=jax.ShapeDtypeStruct((t, 3 * D_HALF), BF16),
        grid=(t // tl, 3),
        in_specs=[pl.BlockSpec((tl, D_HALF), lambda i, part: (i, base + part)), prev, nxt,
                  pl.BlockSpec((DN_CONV, D_HALF), lambda i, part: (0, part))],
        out_specs=pl.BlockSpec((tl, D_HALF), lambda i, part: (i, part)),
        scratch_shapes=[pltpu.VMEM((tl + 2 * HALO, D_HALF), F32)],
        compiler_params=_params("parallel", "parallel"),
        name="dn_prep",
    )(proj, proj, proj, conv_w)


def _dot_hi_each(lhs_parts, rhs_parts):
    main = [_dot(ah, bh) for (ah, _), (bh, _) in zip(lhs_parts, rhs_parts)]
    cross = [_dot(ah, bl) + _dot(al, bh) for (ah, al), (bh, bl) in zip(lhs_parts, rhs_parts)]
    return [m + x for m, x in zip(main, cross)]


def _block_diag(y, left):
    zero = jnp.zeros_like(y)
    return jnp.concatenate([jnp.where(left, y, zero), jnp.where(left, zero, y)], axis=0)


def _pair_products(lhs, rhs, left):
    ls = [_split_bf16(x) for x in lhs]
    rs = [tuple(_block_diag(part, left) for part in _split_bf16(y)) for y in rhs]
    return _dot_hi_each(ls, rs)


def _unit_triangular_inverses(mats, eye, ri, ci, left):
    size = SUBLANES
    same = (ri // size) == (ci // size)
    ps = [jnp.where(same, a, 0.0) for a in mats]
    invs = [eye - d for d in ps]
    n = 2
    while n < size:
        ps = _pair_products(ps, ps, left)
        invs = [inv + x for inv, x in zip(invs, _pair_products(invs, ps, left))]
        n *= 2
    while size < DN_CHUNK:
        size *= 2
        merged = (ri // size) == (ci // size)
        es = [jnp.where(merged & ~same, a, 0.0) for a in mats]
        invs = [inv - x for inv, x in zip(invs, _pair_products(_pair_products(invs, es, left), invs, left))]
        same = merged
    return invs


def _dn_kernel(*refs, tl, nt, groups, reverse):
    if reverse:
        q_ref, k_ref, v_ref, gates_ref, alog_ref, dtb_ref, o_ref, s_ref = refs
    else:
        (q_ref, k_ref, v_ref, gates_ref, alog_ref, dtb_ref, z_ref, ob_ref, ng_ref,
         o_ref, s_ref) = refs
    j = pl.program_id(0)
    it = nt - 1 - j if reverse else j
    row0 = it * tl
    _, sstart, slen = _seq_info(row0, groups)
    pos0 = row0 - sstart
    reset = (pos0 + tl == slen) if reverse else (pos0 == 0)

    @pl.when(reset)
    def _():
        s_ref[...] = jnp.zeros_like(s_ref)

    gates = gates_ref[...]
    beta_all = jax.nn.sigmoid(gates)
    la_all = -jnp.exp(alog_ref[...]) * jax.nn.softplus(gates + dtb_ref[...])
    cb0 = DN_HEADS if reverse else 0
    ca0 = cb0 + 2 * DN_HEADS

    c = DN_CHUNK
    left = lax.broadcasted_iota(jnp.int32, (1, 2 * c), 1) < c
    ri = lax.broadcasted_iota(jnp.int32, (c, 2 * c), 0)
    ci = jnp.bitwise_and(lax.broadcasted_iota(jnp.int32, (c, 2 * c), 1), c - 1)
    eye = (ri == ci).astype(F32)
    incl = (ri <= ci) if reverse else (ri >= ci)
    strict = (ri < ci) if reverse else (ri > ci)
    incl_t = (ri >= ci) if reverse else (ri <= ci)
    last = 0 if reverse else c - 1

    nc = tl // c
    order = list(range(nc - 1, -1, -1) if reverse else range(nc))
    heads = range(DN_HEADS)
    half_heads = range(DN_HEADS // 2)
    units = [(ch, h) for ch in order for h in heads]
    pairs = range(len(units) // 2)

    def rows(ch):
        return slice(ch * c, (ch + 1) * c)

    def cols(h):
        return slice(h * DN_D, (h + 1) * DN_D)

    def paired(xs):
        return [jnp.where(left, xs[2 * p], xs[2 * p + 1]) for p in pairs]

    q = [q_ref[rows(ch), cols(h)] for ch, h in units]
    k = [k_ref[rows(ch), cols(h)] for ch, h in units]
    beta = [beta_all[rows(ch), cb0 + h:cb0 + h + 1] for ch, h in units]
    la = [la_all[rows(ch), ca0 + h:ca0 + h + 1] for ch, h in units]
    la2 = paired(la)
    la_row = [jnp.sum(eye * x, axis=0, keepdims=True) for x in la2]
    g_row = [jnp.sum(jnp.where(incl_t, x, 0.0), axis=0, keepdims=True) for x in la2]
    windowed = [jnp.where(incl, x, 0.0) for x in la_row]
    g_col = [jnp.sum(jnp.where(left == (u % 2 == 0), windowed[u // 2], 0.0), axis=1, keepdims=True)
             for u in range(len(units))]
    gam = [jnp.where(incl, jnp.exp(jnp.where(incl, gc - gr, 0.0)), 0.0) for gc, gr in zip(paired(g_col), g_row)]
    eg = [jnp.exp(gc) for gc in g_col]
    g_last = [gc[last:last + 1, :] for gc in g_col]
    kb = [x.astype(F32) * b for x, b in zip(k, beta)]
    kq = [_dot_nt(jnp.concatenate([kb[2 * p].astype(BF16), q[2 * p], kb[2 * p + 1].astype(BF16), q[2 * p + 1]],
                                  axis=0),
                  jnp.concatenate([k[2 * p], k[2 * p + 1]], axis=0)) for p in pairs]
    a = [jnp.where(strict, jnp.where(left, x[:c], x[2 * c:3 * c]) * gm, 0.0) for x, gm in zip(kq, gam)]
    attn = [_block_diag((jnp.where(left, x[c:2 * c], x[3 * c:]) * gm).astype(BF16), left)
            for x, gm in zip(kq, gam)]
    tinv = _unit_triangular_inverses(a, eye, ri, ci, left)
    rhs = [jnp.concatenate([v_ref[rows(ch), cols(h)].astype(F32) * b, x * e], axis=1)
           for (ch, h), b, x, e in zip(units, beta, kb, eg)]
    uw2 = _dot_hi_each([tuple(_block_diag(part, left) for part in _split_bf16(t)) for t in tinv],
                       [_split_bf16(jnp.concatenate([rhs[2 * p], rhs[2 * p + 1]], axis=0)) for p in pairs])
    uw = [uw2[u // 2][(u % 2) * c:(u % 2 + 1) * c] for u in range(len(units))]
    tail = [jnp.exp(gl - gc) for gl, gc in zip(g_last, g_col)]
    dec = [jnp.exp(gl) for gl in g_last]

    for ci_, ch in enumerate(order):
        idx = [ci_ * DN_HEADS + h for h in heads]
        s = [s_ref[h] for h in heads]
        sb = [x.astype(BF16) for x in s]
        ws = [_dot(jnp.concatenate([uw[i][:, DN_D:].astype(BF16), q[i]], axis=0), sb[h])
              for h, i in zip(heads, idx)]
        v_new = [uw[i][:, :DN_D] - x[:c] for i, x in zip(idx, ws)]
        av = [_dot(attn[idx[2 * x] // 2],
                   jnp.concatenate([v_new[2 * x].astype(BF16), v_new[2 * x + 1].astype(BF16)], axis=0))
              for x in half_heads]
        o = [ws[h][c:] * eg[idx[h]] + av[h // 2][(h % 2) * c:(h % 2 + 1) * c] for h in heads]
        for h, i in zip(heads, idx):
            s_ref[h] = s[h] * dec[i] + _dot_tn(k[i], (v_new[h] * tail[i]).astype(BF16))
        for h in heads:
            if reverse:
                o_ref[rows(ch), cols(h)] = o[h]
            else:
                oo = o[h] + ob_ref[rows(ch), cols(h)]
                on = (oo * lax.rsqrt(jnp.mean(oo * oo, axis=-1, keepdims=True) + EPS)) * ng_ref[...]
                z = z_ref[rows(ch), cols(h)].astype(F32)
                o_ref[rows(ch), cols(h)] = (on * jax.nn.silu(z)).astype(BF16)


def _deltanet(proj, gate_cols, qkv, alog_row, dtb_row, norm_g, groups):
    t = proj.shape[0]
    tl = min(TL_DN, groups[0][1])
    nt = t // tl
    z_blk = (P_RET + 3 * D_HALF) // D_HALF
    row = pl.BlockSpec((1, LANES), lambda j: (0, 0))
    state = pltpu.VMEM((DN_HEADS, DN_D, DN_D), F32)

    def common(reverse):
        tile = (lambda j: nt - 1 - j) if reverse else (lambda j: j)
        sec = lambda s: pl.BlockSpec((tl, D_HALF), lambda j: (tile(j), s))
        gates = pl.BlockSpec((tl, LANES), lambda j: (tile(j), 0))
        return sec, gates

    sec, gates = common(True)
    ob = pl.pallas_call(
        functools.partial(_dn_kernel, tl=tl, nt=nt, groups=groups, reverse=True),
        out_shape=jax.ShapeDtypeStruct((t, D_HALF), F32),
        grid=(nt,),
        in_specs=[sec(0), sec(1), sec(2), gates, row, row],
        out_specs=sec(0),
        scratch_shapes=[state],
        compiler_params=_params("arbitrary"),
        name="dn_bwd",
    )(qkv, qkv, qkv, gate_cols, alog_row, dtb_row)

    sec, gates = common(False)
    return pl.pallas_call(
        functools.partial(_dn_kernel, tl=tl, nt=nt, groups=groups, reverse=False),
        out_shape=jax.ShapeDtypeStruct((t, D_HALF), BF16),
        grid=(nt,),
        in_specs=[sec(0), sec(1), sec(2), gates, row, row, sec(z_blk), sec(0), row],
        out_specs=sec(0),
        scratch_shapes=[state],
        compiler_params=_params("arbitrary"),
        name="dn_fwd",
    )(qkv, qkv, qkv, gate_cols, alog_row, dtb_row, proj, ob, norm_g)


def _gate_row(f_vals, b_vals):
    row = jnp.zeros((LANES,), F32)
    row = row.at[2 * DN_HEADS:3 * DN_HEADS].set(f_vals.astype(F32))
    row = row.at[3 * DN_HEADS:4 * DN_HEADS].set(b_vals.astype(F32))
    return row.reshape(1, LANES)


def _trunk(x, c, p, groups):
    t, d = x.shape
    n_seq = c.shape[0]
    c_pad = jnp.zeros((SEQ_PAD, d), F32).at[:n_seq].set(c)
    mods = _ada(c_pad, p['w_ada'], p['b_ada'])
    mods = mods.reshape(DEPTH, SEQ_PAD, N_MOD, 1, d).transpose(0, 2, 1, 3, 4)
    fin = _ada(c_pad, p['w_ada_final'][None], p['b_ada_final'][None])
    fin = fin.reshape(SEQ_PAD, 2, 1, d).transpose(1, 0, 2, 3)

    max_len = max(ln for _, ln in groups)
    cos, sin = _rope_tables(max_len, min(TL_RET, groups[0][1]))

    row = lambda a: a.reshape(1, -1)
    for layer in range(DEPTH):
        sh1, sc1, g1, sh2, sc2, g2, sh3, sc3, g3 = [mods[layer, jm] for jm in range(N_MOD)]
        x = _ffn(x, row(p['norm_ffn1'][layer]), sh1, sc1, g1, p['w_ffn1_in'], p['w_ffn1_out'], layer, groups)
        idx = layer // 2
        gain = row(p['norm_mix'][layer])
        if layer % 2 == 0:
            proj = _proj(x, gain, sh2, sc2, p['w_in_even'], idx, P_EVEN, groups)
            y = _even_mix(proj, p['pool_w'][idx], row(p['pool_scale'][idx]), row(p['sgu_norm'][idx]),
                          p['sgu_w'][idx], p['sgu_b'][idx][..., None], groups)
            x = _outproj(x, y, y, 0, 1, g2, p['w_out_even'], idx, groups)
        else:
            proj, gate_cols = _proj(x, gain, sh2, sc2, p['w_in_odd'], idx, P_ODD_MAIN, groups,
                                    odd_extras=(p['w_in_odd_gates'], cos, sin))
            yc = _retention(proj, p['ret_decay_f'][idx], p['ret_decay_b'][idx],
                            row(p['ret_norm'][idx]), groups)
            qkv = _dn_prep(proj, p['dn_conv'][idx], groups)
            yd = _deltanet(proj, gate_cols, qkv,
                           _gate_row(p['dn_a_log_f'][idx], p['dn_a_log_b'][idx]),
                           _gate_row(p['dn_dt_bias_f'][idx], p['dn_dt_bias_b'][idx]),
                           row(p['dn_norm'][idx]), groups)
            x = _outproj(x, yc, yd, 0, 0, g2, p['w_out_odd'], idx, groups)
        x = _ffn(x, row(p['norm_ffn2'][layer]), sh3, sc3, g3, p['w_ffn2_in'], p['w_ffn2_out'], layer, groups)
    return _final(x, row(p['norm_final']), fin[0], fin[1], groups)


def _prepare(p):
    q = dict(p)
    for name in ('w_ffn1_in', 'w_ffn1_out', 'w_ffn2_in', 'w_ffn2_out', 'w_in_even', 'w_out_even',
                 'pool_w', 'sgu_w', 'w_in_odd', 'w_out_odd'):
        q[name] = p[name].astype(BF16)
    gates = q['w_in_odd'][:, :, P_ODD_MAIN:]
    q['w_in_odd_gates'] = jnp.pad(gates, ((0, 0), (0, 0), (0, LANES - gates.shape[-1])))
    return q


def kernel(x_prompt, x_sample, c_prompt, c_sample, w_ada, b_ada, norm_ffn1, w_ffn1_in, w_ffn1_out, norm_mix, norm_ffn2, w_ffn2_in, w_ffn2_out, w_in_even, w_out_even, pool_w, pool_scale, sgu_norm, sgu_w, sgu_b, w_in_odd, w_out_odd, ret_decay_f, ret_decay_b, ret_norm, dn_conv, dn_a_log_f, dn_a_log_b, dn_dt_bias_f, dn_dt_bias_b, dn_norm, norm_final, w_ada_final, b_ada_final):
    p = _prepare({
        'w_ada': w_ada, 'b_ada': b_ada, 'norm_ffn1': norm_ffn1, 'w_ffn1_in': w_ffn1_in,
        'w_ffn1_out': w_ffn1_out, 'norm_mix': norm_mix, 'norm_ffn2': norm_ffn2,
        'w_ffn2_in': w_ffn2_in, 'w_ffn2_out': w_ffn2_out, 'w_in_even': w_in_even,
        'w_out_even': w_out_even, 'pool_w': pool_w, 'pool_scale': pool_scale,
        'sgu_norm': sgu_norm, 'sgu_w': sgu_w, 'sgu_b': sgu_b, 'w_in_odd': w_in_odd,
        'w_out_odd': w_out_odd, 'ret_decay_f': ret_decay_f, 'ret_decay_b': ret_decay_b,
        'ret_norm': ret_norm, 'dn_conv': dn_conv, 'dn_a_log_f': dn_a_log_f,
        'dn_a_log_b': dn_a_log_b, 'dn_dt_bias_f': dn_dt_bias_f, 'dn_dt_bias_b': dn_dt_bias_b,
        'dn_norm': dn_norm, 'norm_final': norm_final, 'w_ada_final': w_ada_final,
        'b_ada_final': b_ada_final,
    })
    bp, lp, d = x_prompt.shape
    bs, ls, _ = x_sample.shape
    groups = ((bp, lp), (bs, ls))
    x = jnp.concatenate([x_prompt.reshape(bp * lp, d), x_sample.reshape(bs * ls, d)], axis=0)
    c = jnp.concatenate([c_prompt, c_sample], axis=0)
    y_prompt, y_sample = _trunk(x, c, p, groups)
    return (y_prompt.reshape(bp, lp, d), y_sample.reshape(bs, ls, d))
```

```python
import functools

import jax
import jax.numpy as jnp
import numpy as np
from jax import lax
from jax.experimental import pallas as pl
from jax.experimental.pallas import tpu as pltpu

F32 = jnp.float32
BF16 = jnp.bfloat16

D_MODEL = 2048
DEPTH = 4
D_HALF = D_MODEL // 2
POOL_WINDOWS = (2, 4, 8, 16)
POOL_GROUP = D_HALF // len(POOL_WINDOWS)
SGU_CHUNK = 128
SGU_HEAD = 128
SGU_GROUPS = D_HALF // SGU_HEAD
RET_HEADS = 4
RET_D = D_HALF // RET_HEADS
RET_CHUNK = 128
ROPE_BASE = 10000.0
DN_HEADS = 8
DN_D = D_HALF // DN_HEADS
DN_CONV = 4
DN_CHUNK = 64
D_FF = 5632
N_MOD = 9
EPS = 1e-6
P_EVEN = 3 * D_HALF
P_RET = 4 * D_HALF
P_ODD_MAIN = P_RET + 4 * D_HALF
P_ODD = P_ODD_MAIN + 4 * DN_HEADS

LANES = 128
SUBLANES = 8
HALO = 2 * SUBLANES
VMEM_LIMIT = 56 * 1024 * 1024

TM_FFN = 1024
TF_FFN = 512
TM_PROJ = 1024
TN_PROJ = 1024
MOD_ROWS = 16
MOD_UNROLL = 8
TM_OUT = 512
TL_EVEN = 256
TL_RET = 512
TL_DN = 256
TL_PREP = 256
TM_FINAL = 512
TN_ADA = 1024
SEQ_PAD = 16


def _seq_info(row0, groups):
    seq = start = length = None
    t0 = s0 = 0
    for gi, (nb, ln) in enumerate(groups):
        rel = row0 - t0
        q = rel // ln
        if gi == 0:
            seq, start, length = q, q * ln, ln
        else:
            here = row0 >= t0
            seq = jnp.where(here, s0 + q, seq)
            start = jnp.where(here, t0 + q * ln, start)
            length = jnp.where(here, ln, length)
        t0 += nb * ln
        s0 += nb
    return seq, start, length


def _params(*sem):
    return pltpu.CompilerParams(dimension_semantics=sem, vmem_limit_bytes=VMEM_LIMIT)


def _modulated(x, gain, shift, scale):
    ms = jnp.mean(x * x, axis=-1, keepdims=True)
    y = x * lax.rsqrt(ms + EPS)
    return (y * gain) * (1.0 + scale) + shift


def _dot(a, b):
    return jnp.dot(a, b, preferred_element_type=F32)


def _dot_nt(a, b):
    return lax.dot_general(a, b, (((1,), (1,)), ((), ())), preferred_element_type=F32)


def _dot_tn(a, b):
    return lax.dot_general(a, b, (((0,), (0,)), ((), ())), preferred_element_type=F32)


def _split_bf16(a):
    hi = a.astype(BF16)
    lo = (a - hi.astype(F32)).astype(BF16)
    return hi, lo


def _dot_hi(a, b):
    ah, al = _split_bf16(a)
    bh, bl = _split_bf16(b)
    return _dot(ah, bh) + (_dot(ah, bl) + _dot(al, bh))


def _ada_kernel(c_ref, w_ref, b_ref, o_ref):
    c = c_ref[...]
    act = jax.nn.silu(c).astype(BF16)
    o_ref[...] = _dot(act, w_ref[...].astype(BF16)) + b_ref[...]


def _ada(c_pad, w, b):
    ly, d, n = w.shape
    s = c_pad.shape[0]
    tn = min(TN_ADA, n)
    return pl.pallas_call(
        _ada_kernel,
        out_shape=jax.ShapeDtypeStruct((ly, s, n), F32),
        grid=(ly, n // tn),
        in_specs=[
            pl.BlockSpec((s, d), lambda l, j: (0, 0)),
            pl.BlockSpec((None, d, tn), lambda l, j: (l, 0, j)),
            pl.BlockSpec((None, 1, tn), lambda l, j: (l, 0, j)),
        ],
        out_specs=pl.BlockSpec((None, s, tn), lambda l, j: (l, 0, j)),
        compiler_params=_params("parallel", "parallel"),
        name="ada_rows",
    )(c_pad, w, b.reshape(ly, 1, n))


def _modulate_into(h_ref, x_ref, gain_ref, sh_ref, sc_ref):
    tm = x_ref.shape[0]
    amp = gain_ref[...] * (1.0 + sc_ref[...])
    shift = sh_ref[...]

    def body(r, carry):
        rows = pl.ds(pl.multiple_of(r * MOD_ROWS, MOD_ROWS), MOD_ROWS)
        x = x_ref[rows, :]
        ms = jnp.mean(x * x, axis=-1, keepdims=True)
        h_ref[rows, :] = ((x * lax.rsqrt(ms + EPS)) * amp + shift).astype(BF16)
        return carry

    lax.fori_loop(0, tm // MOD_ROWS, body, 0, unroll=MOD_UNROLL)


def _ffn_kernel(x_ref, gain_ref, sh_ref, sc_ref, gt_ref, wg_ref, wu_ref, wo_ref, o_ref, h_ref, *, nf):
    f = pl.program_id(1)

    @pl.when(f == 0)
    def _():
        _modulate_into(h_ref, x_ref, gain_ref, sh_ref, sc_ref)
        o_ref[...] = jnp.zeros_like(o_ref)

    h = h_ref[...]
    g = _dot(h, wg_ref[...])
    u = _dot(h, wu_ref[...])
    a = (jax.nn.silu(g) * u).astype(BF16)
    o_ref[...] += _dot(a, wo_ref[...])

    @pl.when(f == nf - 1)
    def _():
        o_ref[...] = x_ref[...] + (0.5 * gt_ref[...]) * o_ref[...]


def _row_spec(groups, tm, d):
    return pl.BlockSpec((None, 1, d), lambda i, j: (_seq_info(i * tm, groups)[0], 0, 0))


def _ffn(x, gain, shift, scale, gate, w_in, w_out, layer, groups):
    t, d = x.shape
    ff = w_out.shape[1]
    tm = min(TM_FFN, groups[0][1])
    tf = min(TF_FFN, ff)
    nf = ff // tf
    row = _row_spec(groups, tm, d)
    return pl.pallas_call(
        functools.partial(_ffn_kernel, nf=nf),
        out_shape=jax.ShapeDtypeStruct((t, d), F32),
        grid=(t // tm, nf),
        in_specs=[
            pl.BlockSpec((tm, d), lambda i, f: (i, 0)),
            pl.BlockSpec((1, d), lambda i, f: (0, 0)),
            row, row, row,
            pl.BlockSpec((None, d, tf), lambda i, f: (layer, 0, f)),
            pl.BlockSpec((None, d, tf), lambda i, f: (layer, 0, nf + f)),
            pl.BlockSpec((None, tf, d), lambda i, f: (layer, f, 0)),
        ],
        out_specs=pl.BlockSpec((tm, d), lambda i, f: (i, 0)),
        scratch_shapes=[pltpu.VMEM((tm, d), BF16)],
        compiler_params=_params("parallel", "arbitrary"),
        name="ffn",
    )(x, gain, shift, scale, gate, w_in, w_in, w_out)


def _proj_kernel(*refs, odd):
    if odd:
        x_ref, gain_ref, sh_ref, sc_ref, w_ref, wn_ref, cos_ref, sin_ref, o_ref, on_ref, h_ref = refs
    else:
        x_ref, gain_ref, sh_ref, sc_ref, w_ref, o_ref, h_ref = refs
    j = pl.program_id(1)

    @pl.when(j == 0)
    def _():
        _modulate_into(h_ref, x_ref, gain_ref, sh_ref, sc_ref)
        if odd:
            on_ref[...] = _dot(h_ref[...], wn_ref[...])

    y = _dot(h_ref[...], w_ref[...])
    if not odd:
        o_ref[...] = y.astype(BF16)
        return

    tn = o_ref.shape[1]
    rope_tiles = 2 * D_HALF // tn
    half = RET_D // 2
    rotate = j < rope_tiles
    cos = jnp.where(rotate, cos_ref[...], 1.0)
    sin = jnp.where(rotate, sin_ref[...], 0.0)
    k_scale = jnp.where(rotate & (j >= rope_tiles // 2), RET_D ** -0.5, 1.0).astype(F32)
    for hd in range(tn // RET_D):
        x1 = y[:, hd * RET_D:hd * RET_D + half]
        x2 = y[:, hd * RET_D + half:(hd + 1) * RET_D]
        o_ref[:, hd * RET_D:hd * RET_D + half] = ((x1 * cos - x2 * sin) * k_scale).astype(BF16)
        o_ref[:, hd * RET_D + half:(hd + 1) * RET_D] = ((x1 * sin + x2 * cos) * k_scale).astype(BF16)


def _proj(x, gain, shift, scale, w, idx, n, groups, odd_extras=None):
    t, d = x.shape
    tm = min(TM_PROJ, groups[0][1])
    tn = TN_PROJ
    row = _row_spec(groups, tm, d)
    odd = odd_extras is not None
    in_specs = [
        pl.BlockSpec((tm, d), lambda i, j: (i, 0)),
        pl.BlockSpec((1, d), lambda i, j: (0, 0)),
        row, row,
        pl.BlockSpec((None, d, tn), lambda i, j: (idx, 0, j)),
    ]
    out_shape = jax.ShapeDtypeStruct((t, n), BF16)
    out_specs = pl.BlockSpec((tm, tn), lambda i, j: (i, j))
    args = (x, gain, shift, scale, w)
    if odd:
        half = RET_D // 2

        def pos_block(i, j):
            row0 = i * tm
            return ((row0 - _seq_info(row0, groups)[1]) // tm, 0)

        tab = pl.BlockSpec((tm, half), pos_block)
        in_specs += [pl.BlockSpec((None, d, LANES), lambda i, j: (idx, 0, 0)), tab, tab]
        out_shape = (out_shape, jax.ShapeDtypeStruct((t, LANES), F32))
        out_specs = (out_specs, pl.BlockSpec((tm, LANES), lambda i, j: (i, 0)))
        args = args + tuple(odd_extras)
    return pl.pallas_call(
        functools.partial(_proj_kernel, odd=odd),
        out_shape=out_shape,
        grid=(t // tm, n // tn),
        in_specs=in_specs,
        out_specs=out_specs,
        scratch_shapes=[pltpu.VMEM((tm, d), BF16)],
        compiler_params=_params("parallel", "arbitrary"),
        name="mix_proj",
    )(*args)


def _outproj_kernel(x_ref, ya_ref, yb_ref, gt_ref, wa_ref, wb_ref, o_ref):
    y = _dot(ya_ref[...], wa_ref[...]) + _dot(yb_ref[...], wb_ref[...])
    o_ref[...] = x_ref[...] + gt_ref[...] * y


def _outproj(x, ya, yb, ca, cb, gate, w, idx, groups):
    t, d = x.shape
    dh = d // 2
    tm = min(TM_OUT, groups[0][1])
    row = _row_spec(groups, tm, d)
    return pl.pallas_call(
        _outproj_kernel,
        out_shape=jax.ShapeDtypeStruct((t, d), F32),
        grid=(t // tm, 1),
        in_specs=[
            pl.BlockSpec((tm, d), lambda i, j: (i, 0)),
            pl.BlockSpec((tm, dh), lambda i, j: (i, ca)),
            pl.BlockSpec((tm, dh), lambda i, j: (i, cb)),
            row,
            pl.BlockSpec((None, dh, d), lambda i, j: (idx, 0, 0)),
            pl.BlockSpec((None, dh, d), lambda i, j: (idx, 1, 0)),
        ],
        out_specs=pl.BlockSpec((tm, d), lambda i, j: (i, 0)),
        compiler_params=_params("parallel", "arbitrary"),
        name="mix_out",
    )(x, ya, yb, gate, w, w)


def _final_kernel(x_ref, gain_ref, sh_ref, sc_ref, o0_ref, o1_ref, *, n0):
    y = _modulated(x_ref[...], gain_ref[...], sh_ref[...], sc_ref[...])
    i = pl.program_id(0)

    @pl.when(i < n0)
    def _():
        o0_ref[...] = y

    @pl.when(i >= n0)
    def _():
        o1_ref[...] = y


def _final(x, gain, shift, scale, groups):
    t, d = x.shape
    tm = min(TM_FINAL, groups[0][1])
    (b0, l0), (b1, l1) = groups
    n0 = b0 * l0 // tm
    row = _row_spec(groups, tm, d)
    return pl.pallas_call(
        functools.partial(_final_kernel, n0=n0),
        out_shape=(jax.ShapeDtypeStruct((b0 * l0, d), F32), jax.ShapeDtypeStruct((b1 * l1, d), F32)),
        grid=(t // tm, 1),
        in_specs=[
            pl.BlockSpec((tm, d), lambda i, j: (i, 0)),
            pl.BlockSpec((1, d), lambda i, j: (0, 0)),
            row, row,
        ],
        out_specs=(pl.BlockSpec((tm, d), lambda i, j: (jnp.minimum(i, n0 - 1), 0)),
                   pl.BlockSpec((tm, d), lambda i, j: (jnp.maximum(i - n0, 0), 0))),
        compiler_params=_params("arbitrary", "arbitrary"),
        name="final_mod",
    )(x, gain, shift, scale)


def _halo_specs(tl, width, nrows, col_of):
    per = tl // HALO
    last = nrows // HALO - 1
    prev = pl.BlockSpec((HALO, width), lambda i, *r: (jnp.maximum(i * per - 1, 0), col_of(i, *r)))
    nxt = pl.BlockSpec((HALO, width), lambda i, *r: (jnp.minimum((i + 1) * per, last), col_of(i, *r)))
    return prev, nxt


def _fill_ext(ext_ref, x_ref, prev_ref, next_ref, first, last, tl):
    ext_ref[HALO:HALO + tl, :] = x_ref[...].astype(F32)
    ext_ref[0:HALO, :] = jnp.where(first, 0.0, prev_ref[...].astype(F32))
    ext_ref[HALO + tl:2 * HALO + tl, :] = jnp.where(last, 0.0, next_ref[...].astype(F32))


def _even_kernel(xa_ref, prev_ref, next_ref, u_ref, v_ref, pw_ref, ps_ref, ng_ref, sw_ref, sb_ref,
                 o_ref, ext_ref, vn_ref, *, tl, groups):
    row0 = pl.program_id(0) * tl
    _, sstart, slen = _seq_info(row0, groups)
    pos0 = row0 - sstart
    _fill_ext(ext_ref, xa_ref, prev_ref, next_ref, pos0 == 0, pos0 + tl == slen, tl)

    t = pos0 + lax.broadcasted_iota(jnp.int32, (tl, 1), 0)
    for gi, w in enumerate(POOL_WINDOWS):
        c0 = gi * POOL_GROUP
        cols = slice(c0, c0 + POOL_GROUP)
        base = HALO - w // 2
        s = ext_ref[base:base + tl, cols]
        for dlt in range(1, w):
            s = s + ext_ref[base + dlt:base + dlt + tl, cols]
        lo = jnp.clip(t - w // 2, 0, slen)
        hi = jnp.clip(t + (w - w // 2), 0, slen)
        cnt = (hi - lo).astype(F32)
        pooled = (s / cnt - ext_ref[HALO:HALO + tl, cols]).astype(BF16)
        ya = _dot(pooled, pw_ref[gi]) * ps_ref[:, cols]
        o_ref[:, cols] = ya.astype(BF16)

    v = jax.nn.gelu(v_ref[...].astype(F32))
    vms = jnp.mean(v * v, axis=-1, keepdims=True)
    vn_ref[...] = ((v * lax.rsqrt(vms + EPS)) * ng_ref[...]).astype(BF16)
    for n in range(tl // SGU_CHUNK):
        rows = slice(n * SGU_CHUNK, (n + 1) * SGU_CHUNK)
        for g in range(SGU_GROUPS):
            cols = slice(g * SGU_HEAD, (g + 1) * SGU_HEAD)
            mixed = _dot(sw_ref[g], vn_ref[rows, cols]) + sb_ref[g]
            u = jax.nn.gelu(u_ref[rows, cols].astype(F32))
            o_ref[rows, D_HALF + g * SGU_HEAD:D_HALF + (g + 1) * SGU_HEAD] = (u * mixed).astype(BF16)


def _even_mix(proj, pool_w, pool_scale, sgu_norm, sgu_w, sgu_b, groups):
    t = proj.shape[0]
    tl = min(TL_EVEN, groups[0][1])
    prev, nxt = _halo_specs(tl, D_HALF, t, lambda i: 0)
    const2 = lambda i: (0, 0)
    const3 = lambda i: (0, 0, 0)
    return pl.pallas_call(
        functools.partial(_even_kernel, tl=tl, groups=groups),
        out_shape=jax.ShapeDtypeStruct((t, 2 * D_HALF), BF16),
        grid=(t // tl,),
        in_specs=[
            pl.BlockSpec((tl, D_HALF), lambda i: (i, 0)),
            prev, nxt,
            pl.BlockSpec((tl, D_HALF), lambda i: (i, 1)),
            pl.BlockSpec((tl, D_HALF), lambda i: (i, 2)),
            pl.BlockSpec(pool_w.shape, const3),
            pl.BlockSpec((1, D_HALF), const2),
            pl.BlockSpec((1, D_HALF), const2),
            pl.BlockSpec(sgu_w.shape, const3),
            pl.BlockSpec(sgu_b.shape, const3),
        ],
        out_specs=pl.BlockSpec((tl, 2 * D_HALF), lambda i: (i, 0)),
        scratch_shapes=[pltpu.VMEM((tl + 2 * HALO, D_HALF), F32), pltpu.VMEM((tl, D_HALF), BF16)],
        compiler_params=_params("parallel"),
        name="even_mix",
    )(proj, proj, proj, proj, proj, pool_w, pool_scale, sgu_norm, sgu_w, sgu_b)


def _rope_kernel(inv_ref, cos_ref, sin_ref, *, tl):
    pos = (pl.program_id(0) * tl + lax.broadcasted_iota(jnp.int32, (tl, 1), 0)).astype(F32)
    ang = pos * inv_ref[...]
    cos_ref[...] = jnp.cos(ang)
    sin_ref[...] = jnp.sin(ang)


def _rope_tables(max_len, tl):
    half = RET_D // 2
    inv = (1.0 / (ROPE_BASE ** jnp.linspace(0.0, 1.0, half, dtype=F32))).reshape(1, half)
    shp = jax.ShapeDtypeStruct((max_len, half), F32)
    return pl.pallas_call(
        functools.partial(_rope_kernel, tl=tl),
        out_shape=(shp, shp),
        grid=(max_len // tl,),
        in_specs=[pl.BlockSpec((1, half), lambda i: (0, 0))],
        out_specs=(pl.BlockSpec((tl, half), lambda i: (i, 0)), pl.BlockSpec((tl, half), lambda i: (i, 0))),
        compiler_params=_params("parallel"),
        name="rope_table",
    )(inv)


def _ret_kernel(*refs, tl, nt, groups, reverse):
    if reverse:
        q_ref, k_ref, v_ref, dec_ref, o_ref, s_ref = refs
    else:
        q_ref, k_ref, v_ref, dec_ref, decb_ref, g_ref, ob_ref, ng_ref, o_ref, s_ref = refs
    j = pl.program_id(0)
    it = nt - 1 - j if reverse else j
    row0 = it * tl
    _, sstart, slen = _seq_info(row0, groups)
    pos0 = row0 - sstart
    reset = (pos0 + tl == slen) if reverse else (pos0 == 0)

    @pl.when(reset)
    def _():
        s_ref[...] = jnp.zeros_like(s_ref)

    c = RET_CHUNK
    heads = range(RET_HEADS)
    idx = lax.broadcasted_iota(jnp.int32, (c, 1), 0).astype(F32)
    lg = [jnp.log1p(-jnp.exp2(-dec_ref[h])) for h in heads]
    if reverse:
        q_dec = [jnp.exp(x * (c - idx)) for x in lg]
        k_dec = [jnp.exp(x * idx) for x in lg]
    else:
        q_dec = [jnp.exp(x * (idx + 1.0)) for x in lg]
        k_dec = [jnp.exp(x * (c - 1.0 - idx)) for x in lg]
        lgb = [jnp.log1p(-jnp.exp2(-decb_ref[h])) for h in heads]
        ri = lax.broadcasted_iota(jnp.int32, (c, c), 0)
        ci = lax.broadcasted_iota(jnp.int32, (c, c), 1)
        rel = (ri - ci).astype(F32)
        dmat = [jnp.where(rel >= 0, jnp.exp(x * jnp.maximum(rel, 0.0)), 0.0)
                + jnp.where(rel <= 0, jnp.exp(y * jnp.maximum(-rel, 0.0)), 0.0) for x, y in zip(lg, lgb)]
    chunk_dec = [jnp.exp(x * float(c)) for x in lg]

    nc = tl // c
    order = list(range(nc - 1, -1, -1) if reverse else range(nc))
    units = [(ch, h) for ch in order for h in heads]

    def rows(ch):
        return slice(ch * c, (ch + 1) * c)

    def cols(h):
        return slice(h * RET_D, (h + 1) * RET_D)

    q = [q_ref[rows(ch), cols(h)] for ch, h in units]
    v = [v_ref[rows(ch), cols(h)] for ch, h in units]
    kv = [_dot_tn(k_ref[rows(ch), cols(h)], (x.astype(F32) * k_dec[h]).astype(BF16))
          for (ch, h), x in zip(units, v)]
    if not reverse:
        scores = [(_dot_nt(x, k_ref[rows(ch), cols(h)]) * dmat[h]).astype(BF16) for (ch, h), x in zip(units, q)]
        intra = [_dot(x, y) for x, y in zip(scores, v)]

    for ci_, ch in enumerate(order):
        s = [s_ref[h] for h in heads]
        inter = [_dot(q[ci_ * RET_HEADS + h], s[h].astype(BF16)) * q_dec[h] for h in heads]
        for h in heads:
            s_ref[h] = s[h] * chunk_dec[h] + kv[ci_ * RET_HEADS + h]
        for h in heads:
            if reverse:
                o_ref[rows(ch), cols(h)] = inter[h]
            else:
                o = intra[ci_ * RET_HEADS + h] + inter[h] + ob_ref[rows(ch), cols(h)]
                mu = jnp.mean(o, axis=-1, keepdims=True)
                var = jnp.mean(jnp.square(o - mu), axis=-1, keepdims=True)
                on = ((o - mu) * lax.rsqrt(var + EPS)) * ng_ref[:, cols(h)]
                gate = jax.nn.silu(g_ref[rows(ch), cols(h)].astype(F32))
                o_ref[rows(ch), cols(h)] = (gate * on).astype(BF16)


def _retention(proj, decay_f, decay_b, norm_g, groups):
    t = proj.shape[0]
    tl = min(TL_RET, groups[0][1])
    nt = t // tl
    dec = pl.BlockSpec((RET_HEADS, 1, 1), lambda j: (0, 0, 0))
    state = pltpu.VMEM((RET_HEADS, RET_D, RET_D), F32)

    def section(reverse):
        tile = (lambda j: nt - 1 - j) if reverse else (lambda j: j)
        return lambda s: pl.BlockSpec((tl, D_HALF), lambda j: (tile(j), s))

    sec = section(True)
    ob = pl.pallas_call(
        functools.partial(_ret_kernel, tl=tl, nt=nt, groups=groups, reverse=True),
        out_shape=jax.ShapeDtypeStruct((t, D_HALF), F32),
        grid=(nt,),
        in_specs=[sec(0), sec(1), sec(2), dec],
        out_specs=sec(0),
        scratch_shapes=[state],
        compiler_params=_params("arbitrary"),
        name="ret_bwd",
    )(proj, proj, proj, decay_b.reshape(RET_HEADS, 1, 1))

    sec = section(False)
    return pl.pallas_call(
        functools.partial(_ret_kernel, tl=tl, nt=nt, groups=groups, reverse=False),
        out_shape=jax.ShapeDtypeStruct((t, D_HALF), BF16),
        grid=(nt,),
        in_specs=[sec(0), sec(1), sec(2), dec, dec, sec(3), sec(0),
                  pl.BlockSpec((1, D_HALF), lambda j: (0, 0))],
        out_specs=sec(0),
        scratch_shapes=[state],
        compiler_params=_params("arbitrary"),
        name="ret_fwd",
    )(proj, proj, proj, decay_f.reshape(RET_HEADS, 1, 1), decay_b.reshape(RET_HEADS, 1, 1),
      proj, ob, norm_g)


def _dnprep_kernel(x_ref, prev_ref, next_ref, w_ref, o_ref, ext_ref, *, tl, groups):
    row0 = pl.program_id(0) * tl
    part = pl.program_id(1)
    _, sstart, slen = _seq_info(row0, groups)
    pos0 = row0 - sstart
    _fill_ext(ext_ref, x_ref, prev_ref, next_ref, pos0 == 0, pos0 + tl == slen, tl)
    left = DN_CONV // 2
    q_scale = jnp.where(part == 0, DN_D ** -0.5, 1.0).astype(F32)
    for h in range(DN_HEADS):
        cols = slice(h * DN_D, (h + 1) * DN_D)
        conv = ext_ref[HALO - left:HALO - left + tl, cols] * w_ref[0:1, cols]
        for tap in range(1, DN_CONV):
            r0 = HALO - left + tap
            conv = conv + ext_ref[r0:r0 + tl, cols] * w_ref[tap:tap + 1, cols]
        y = jax.nn.silu(conv)
        inv_norm = lax.rsqrt(jnp.sum(y * y, axis=-1, keepdims=True) + EPS)
        o_ref[:, cols] = (y * jnp.where(part < 2, inv_norm * q_scale, 1.0)).astype(BF16)


def _dn_prep(proj, conv_w, groups):
    t = proj.shape[0]
    tl = min(TL_PREP, groups[0][1])
    base = P_RET // D_HALF
    prev, nxt = _halo_specs(tl, D_HALF, t, lambda i, part: base + part)
    return pl.pallas_call(
        functools.partial(_dnprep_kernel, tl=tl, groups=groups),
        out_shape=jax.ShapeDtypeStruct((t, 3 * D_HALF), BF16),
        grid=(t // tl, 3),
        in_specs=[pl.BlockSpec((tl, D_HALF), lambda i, part: (i, base + part)), prev, nxt,
                  pl.BlockSpec((DN_CONV, D_HALF), lambda i, part: (0, part))],
        out_specs=pl.BlockSpec((tl, D_HALF), lambda i, part: (i, part)),
        scratch_shapes=[pltpu.VMEM((tl + 2 * HALO, D_HALF), F32)],
        compiler_params=_params("parallel", "parallel"),
        name="dn_prep",
    )(proj, proj, proj, conv_w)


def _dot_hi_each(lhs_parts, rhs_parts):
    main = [_dot(ah, bh) for (ah, _), (bh, _) in zip(lhs_parts, rhs_parts)]
    cross = [_dot(ah, bl) + _dot(al, bh) for (ah, al), (bh, bl) in zip(lhs_parts, rhs_parts)]
    return [m + x for m, x in zip(main, cross)]


def _block_diag(y, left):
    zero = jnp.zeros_like(y)
    return jnp.concatenate([jnp.where(left, y, zero), jnp.where(left, zero, y)], axis=0)


def _pair_products(lhs, rhs, left):
    ls = [_split_bf16(x) for x in lhs]
    rs = [tuple(_block_diag(part, left) for part in _split_bf16(y)) for y in rhs]
    return _dot_hi_each(ls, rs)


def _unit_triangular_inverses(mats, eye, ri, ci, left):
    size = SUBLANES
    same = (ri // size) == (ci // size)
    ps = [jnp.where(same, a, 0.0) for a in mats]
    invs = [eye - d for d in ps]
    n = 2
    while n < size:
        ps = _pair_products(ps, ps, left)
        invs = [inv + x for inv, x in zip(invs, _pair_products(invs, ps, left))]
        n *= 2
    while size < DN_CHUNK:
        size *= 2
        merged = (ri // size) == (ci // size)
        es = [jnp.where(merged & ~same, a, 0.0) for a in mats]
        invs = [inv - x for inv, x in zip(invs, _pair_products(_pair_products(invs, es, left), invs, left))]
        same = merged
    return invs


def _dn_kernel(*refs, tl, nt, groups, reverse):
    if reverse:
        q_ref, k_ref, v_ref, gates_ref, alog_ref, dtb_ref, o_ref, s_ref = refs
    else:
        (q_ref, k_ref, v_ref, gates_ref, alog_ref, dtb_ref, z_ref, ob_ref, ng_ref,
         o_ref, s_ref) = refs
    j = pl.program_id(0)
    it = nt - 1 - j if reverse else j
    row0 = it * tl
    _, sstart, slen = _seq_info(row0, groups)
    pos0 = row0 - sstart
    reset = (pos0 + tl == slen) if reverse else (pos0 == 0)

    @pl.when(reset)
    def _():
        s_ref[...] = jnp.zeros_like(s_ref)

    gates = gates_ref[...]
    beta_all = jax.nn.sigmoid(gates)
    la_all = -jnp.exp(alog_ref[...]) * jax.nn.softplus(gates + dtb_ref[...])
    cb0 = DN_HEADS if reverse else 0
    ca0 = cb0 + 2 * DN_HEADS

    c = DN_CHUNK
    left = lax.broadcasted_iota(jnp.int32, (1, 2 * c), 1) < c
    ri = lax.broadcasted_iota(jnp.int32, (c, 2 * c), 0)
    ci = jnp.bitwise_and(lax.broadcasted_iota(jnp.int32, (c, 2 * c), 1), c - 1)
    eye = (ri == ci).astype(F32)
    incl = (ri <= ci) if reverse else (ri >= ci)
    strict = (ri < ci) if reverse else (ri > ci)
    incl_t = (ri >= ci) if reverse else (ri <= ci)
    last = 0 if reverse else c - 1

    nc = tl // c
    order = list(range(nc - 1, -1, -1) if reverse else range(nc))
    heads = range(DN_HEADS)
    half_heads = range(DN_HEADS // 2)
    units = [(ch, h) for ch in order for h in heads]
    pairs = range(len(units) // 2)

    def rows(ch):
        return slice(ch * c, (ch + 1) * c)

    def cols(h):
        return slice(h * DN_D, (h + 1) * DN_D)

    def paired(xs):
        return [jnp.where(left, xs[2 * p], xs[2 * p + 1]) for p in pairs]

    q = [q_ref[rows(ch), cols(h)] for ch, h in units]
    k = [k_ref[rows(ch), cols(h)] for ch, h in units]
    beta = [beta_all[rows(ch), cb0 + h:cb0 + h + 1] for ch, h in units]
    la = [la_all[rows(ch), ca0 + h:ca0 + h + 1] for ch, h in units]
    la2 = paired(la)
    la_row = [jnp.sum(eye * x, axis=0, keepdims=True) for x in la2]
    g_row = [jnp.sum(jnp.where(incl_t, x, 0.0), axis=0, keepdims=True) for x in la2]
    windowed = [jnp.where(incl, x, 0.0) for x in la_row]
    g_col = [jnp.sum(jnp.where(left == (u % 2 == 0), windowed[u // 2], 0.0), axis=1, keepdims=True)
             for u in range(len(units))]
    gam = [jnp.where(incl, jnp.exp(jnp.where(incl, gc - gr, 0.0)), 0.0) for gc, gr in zip(paired(g_col), g_row)]
    eg = [jnp.exp(gc) for gc in g_col]
    g_last = [gc[last:last + 1, :] for gc in g_col]
    kb = [x.astype(F32) * b for x, b in zip(k, beta)]
    kq = [_dot_nt(jnp.concatenate([kb[2 * p].astype(BF16), q[2 * p], kb[2 * p + 1].astype(BF16), q[2 * p + 1]],
                                  axis=0),
                  jnp.concatenate([k[2 * p], k[2 * p + 1]], axis=0)) for p in pairs]
    a = [jnp.where(strict, jnp.where(left, x[:c], x[2 * c:3 * c]) * gm, 0.0) for x, gm in zip(kq, gam)]
    attn = [_block_diag((jnp.where(left, x[c:2 * c], x[3 * c:]) * gm).astype(BF16), left)
            for x, gm in zip(kq, gam)]
    tinv = _unit_triangular_inverses(a, eye, ri, ci, left)
    rhs = [jnp.concatenate([v_ref[rows(ch), cols(h)].astype(F32) * b, x * e], axis=1)
           for (ch, h), b, x, e in zip(units, beta, kb, eg)]
    uw2 = _dot_hi_each([tuple(_block_diag(part, left) for part in _split_bf16(t)) for t in tinv],
                       [_split_bf16(jnp.concatenate([rhs[2 * p], rhs[2 * p + 1]], axis=0)) for p in pairs])
    uw = [uw2[u // 2][(u % 2) * c:(u % 2 + 1) * c] for u in range(len(units))]
    tail = [jnp.exp(gl - gc) for gl, gc in zip(g_last, g_col)]
    dec = [jnp.exp(gl) for gl in g_last]

    for ci_, ch in enumerate(order):
        idx = [ci_ * DN_HEADS + h for h in heads]
        s = [s_ref[h] for h in heads]
        sb = [x.astype(BF16) for x in s]
        ws = [_dot(jnp.concatenate([uw[i][:, DN_D:].astype(BF16), q[i]], axis=0), sb[h])
              for h, i in zip(heads, idx)]
        v_new = [uw[i][:, :DN_D] - x[:c] for i, x in zip(idx, ws)]
        av = [_dot(attn[idx[2 * x] // 2],
                   jnp.concatenate([v_new[2 * x].astype(BF16), v_new[2 * x + 1].astype(BF16)], axis=0))
              for x in half_heads]
        o = [ws[h][c:] * eg[idx[h]] + av[h // 2][(h % 2) * c:(h % 2 + 1) * c] for h in heads]
        for h, i in zip(heads, idx):
            s_ref[h] = s[h] * dec[i] + _dot_tn(k[i], (v_new[h] * tail[i]).astype(BF16))
        for h in heads:
            if reverse:
                o_ref[rows(ch), cols(h)] = o[h]
            else:
                oo = o[h] + ob_ref[rows(ch), cols(h)]
                on = (oo * lax.rsqrt(jnp.mean(oo * oo, axis=-1, keepdims=True) + EPS)) * ng_ref[...]
                z = z_ref[rows(ch), cols(h)].astype(F32)
                o_ref[rows(ch), cols(h)] = (on * jax.nn.silu(z)).astype(BF16)


def _deltanet(proj, gate_cols, qkv, alog_row, dtb_row, norm_g, groups):
    t = proj.shape[0]
    tl = min(TL_DN, groups[0][1])
    nt = t // tl
    z_blk = (P_RET + 3 * D_HALF) // D_HALF
    row = pl.BlockSpec((1, LANES), lambda j: (0, 0))
    state = pltpu.VMEM((DN_HEADS, DN_D, DN_D), F32)

    def common(reverse):
        tile = (lambda j: nt - 1 - j) if reverse else (lambda j: j)
        sec = lambda s: pl.BlockSpec((tl, D_HALF), lambda j: (tile(j), s))
        gates = pl.BlockSpec((tl, LANES), lambda j: (tile(j), 0))
        return sec, gates

    sec, gates = common(True)
    ob = pl.pallas_call(
        functools.partial(_dn_kernel, tl=tl, nt=nt, groups=groups, reverse=True),
        out_shape=jax.ShapeDtypeStruct((t, D_HALF), F32),
        grid=(nt,),
        in_specs=[sec(0), sec(1), sec(2), gates, row, row],
        out_specs=sec(0),
        scratch_shapes=[state],
        compiler_params=_params("arbitrary"),
        name="dn_bwd",
    )(qkv, qkv, qkv, gate_cols, alog_row, dtb_row)

    sec, gates = common(False)
    return pl.pallas_call(
        functools.partial(_dn_kernel, tl=tl, nt=nt, groups=groups, reverse=False),
        out_shape=jax.ShapeDtypeStruct((t, D_HALF), BF16),
        grid=(nt,),
        in_specs=[sec(0), sec(1), sec(2), gates, row, row, sec(z_blk), sec(0), row],
        out_specs=sec(0),
        scratch_shapes=[state],
        compiler_params=_params("arbitrary"),
        name="dn_fwd",
    )(qkv, qkv, qkv, gate_cols, alog_row, dtb_row, proj, ob, norm_g)


def _gate_row(f_vals, b_vals):
    row = jnp.zeros((LANES,), F32)
    row = row.at[2 * DN_HEADS:3 * DN_HEADS].set(f_vals.astype(F32))
    row = row.at[3 * DN_HEADS:4 * DN_HEADS].set(b_vals.astype(F32))
    return row.reshape(1, LANES)


def _trunk(x, c, p, groups):
    t, d = x.shape
    n_seq = c.shape[0]
    c_pad = jnp.zeros((SEQ_PAD, d), F32).at[:n_seq].set(c)
    mods = _ada(c_pad, p['w_ada'], p['b_ada'])
    mods = mods.reshape(DEPTH, SEQ_PAD, N_MOD, 1, d).transpose(0, 2, 1, 3, 4)
    fin = _ada(c_pad, p['w_ada_final'][None], p['b_ada_final'][None])
    fin = fin.reshape(SEQ_PAD, 2, 1, d).transpose(1, 0, 2, 3)

    max_len = max(ln for _, ln in groups)
    cos, sin = _rope_tables(max_len, min(TL_RET, groups[0][1]))

    row = lambda a: a.reshape(1, -1)
    for layer in range(DEPTH):
        sh1, sc1, g1, sh2, sc2, g2, sh3, sc3, g3 = [mods[layer, jm] for jm in range(N_MOD)]
        x = _ffn(x, row(p['norm_ffn1'][layer]), sh1, sc1, g1, p['w_ffn1_in'], p['w_ffn1_out'], layer, groups)
        idx = layer // 2
        gain = row(p['norm_mix'][layer])
        if layer % 2 == 0:
            proj = _proj(x, gain, sh2, sc2, p['w_in_even'], idx, P_EVEN, groups)
            y = _even_mix(proj, p['pool_w'][idx], row(p['pool_scale'][idx]), row(p['sgu_norm'][idx]),
                          p['sgu_w'][idx], p['sgu_b'][idx][..., None], groups)
            x = _outproj(x, y, y, 0, 1, g2, p['w_out_even'], idx, groups)
        else:
            proj, gate_cols = _proj(x, gain, sh2, sc2, p['w_in_odd'], idx, P_ODD_MAIN, groups,
                                    odd_extras=(p['w_in_odd_gates'], cos, sin))
            yc = _retention(proj, p['ret_decay_f'][idx], p['ret_decay_b'][idx],
                            row(p['ret_norm'][idx]), groups)
            qkv = _dn_prep(proj, p['dn_conv'][idx], groups)
            yd = _deltanet(proj, gate_cols, qkv,
                           _gate_row(p['dn_a_log_f'][idx], p['dn_a_log_b'][idx]),
                           _gate_row(p['dn_dt_bias_f'][idx], p['dn_dt_bias_b'][idx]),
                           row(p['dn_norm'][idx]), groups)
            x = _outproj(x, yc, yd, 0, 0, g2, p['w_out_odd'], idx, groups)
        x = _ffn(x, row(p['norm_ffn2'][layer]), sh3, sc3, g3, p['w_ffn2_in'], p['w_ffn2_out'], layer, groups)
    return _final(x, row(p['norm_final']), fin[0], fin[1], groups)


def _prepare(p):
    q = dict(p)
    for name in ('w_ffn1_in', 'w_ffn1_out', 'w_ffn2_in', 'w_ffn2_out', 'w_in_even', 'w_out_even',
                 'pool_w', 'sgu_w', 'w_in_odd', 'w_out_odd'):
        q[name] = p[name].astype(BF16)
    gates = q['w_in_odd'][:, :, P_ODD_MAIN:]
    q['w_in_odd_gates'] = jnp.pad(gates, ((0, 0), (0, 0), (0, LANES - gates.shape[-1])))
    return q


def kernel(x_prompt, x_sample, c_prompt, c_sample, w_ada, b_ada, norm_ffn1, w_ffn1_in, w_ffn1_out, norm_mix, norm_ffn2, w_ffn2_in, w_ffn2_out, w_in_even, w_out_even, pool_w, pool_scale, sgu_norm, sgu_w, sgu_b, w_in_odd, w_out_odd, ret_decay_f, ret_decay_b, ret_norm, dn_conv, dn_a_log_f, dn_a_log_b, dn_dt_bias_f, dn_dt_bias_b, dn_norm, norm_final, w_ada_final, b_ada_final):
    p = _prepare({
        'w_ada': w_ada, 'b_ada': b_ada, 'norm_ffn1': norm_ffn1, 'w_ffn1_in': w_ffn1_in,
        'w_ffn1_out': w_ffn1_out, 'norm_mix': norm_mix, 'norm_ffn2': norm_ffn2,
        'w_ffn2_in': w_ffn2_in, 'w_ffn2_out': w_ffn2_out, 'w_in_even': w_in_even,
        'w_out_even': w_out_even, 'pool_w': pool_w, 'pool_scale': pool_scale,
        'sgu_norm': sgu_norm, 'sgu_w': sgu_w, 'sgu_b': sgu_b, 'w_in_odd': w_in_odd,
        'w_out_odd': w_out_odd, 'ret_decay_f': ret_decay_f, 'ret_decay_b': ret_decay_b,
        'ret_norm': ret_norm, 'dn_conv': dn_conv, 'dn_a_log_f': dn_a_log_f,
        'dn_a_log_b': dn_a_log_b, 'dn_dt_bias_f': dn_dt_bias_f, 'dn_dt_bias_b': dn_dt_bias_b,
        'dn_norm': dn_norm, 'norm_final': norm_final, 'w_ada_final': w_ada_final,
        'b_ada_final': b_ada_final,
    })
    bp, lp, d = x_prompt.shape
    bs, ls, _ = x_sample.shape
    groups = ((bp, lp), (bs, ls))
    x = jnp.concatenate([x_prompt.reshape(bp * lp, d), x_sample.reshape(bs * ls, d)], axis=0)
    c = jnp.concatenate([c_prompt, c_sample], axis=0)
    y_prompt, y_sample = _trunk(x, c, p, groups)
    return (y_prompt.reshape(bp, lp, d), y_sample.reshape(bs, ls, d))
```

```python
import functools

import jax
import jax.numpy as jnp
import numpy as np
from jax import lax
from jax.experimental import pallas as pl
from jax.experimental.pallas import tpu as pltpu

F32 = jnp.float32
BF16 = jnp.bfloat16

D_MODEL = 2048
DEPTH = 4
D_HALF = D_MODEL // 2
POOL_WINDOWS = (2, 4, 8, 16)
POOL_GROUP = D_HALF // len(POOL_WINDOWS)
SGU_CHUNK = 128
SGU_HEAD = 128
SGU_GROUPS = D_HALF // SGU_HEAD
RET_HEADS = 4
RET_D = D_HALF // RET_HEADS
RET_CHUNK = 128
ROPE_BASE = 10000.0
DN_HEADS = 8
DN_D = D_HALF // DN_HEADS
DN_CONV = 4
DN_CHUNK = 64
D_FF = 5632
N_MOD = 9
EPS = 1e-6
P_EVEN = 3 * D_HALF
P_RET = 4 * D_HALF
P_ODD_MAIN = P_RET + 4 * D_HALF
P_ODD = P_ODD_MAIN + 4 * DN_HEADS

LANES = 128
SUBLANES = 8
HALO = 2 * SUBLANES
VMEM_LIMIT = 56 * 1024 * 1024

TM_FFN = 1024
TF_FFN = 512
TM_PROJ = 1024
TN_PROJ = 1024
MOD_ROWS = 16
MOD_UNROLL = 8
TM_OUT = 512
TL_EVEN = 256
TL_RET = 512
TL_DN = 256
TL_PREP = 256
TM_FINAL = 512
TN_ADA = 1024
SEQ_PAD = 16


def _seq_info(row0, groups):
    seq = start = length = None
    t0 = s0 = 0
    for gi, (nb, ln) in enumerate(groups):
        rel = row0 - t0
        q = rel // ln
        if gi == 0:
            seq, start, length = q, q * ln, ln
        else:
            here = row0 >= t0
            seq = jnp.where(here, s0 + q, seq)
            start = jnp.where(here, t0 + q * ln, start)
            length = jnp.where(here, ln, length)
        t0 += nb * ln
        s0 += nb
    return seq, start, length


def _params(*sem):
    return pltpu.CompilerParams(dimension_semantics=sem, vmem_limit_bytes=VMEM_LIMIT)


def _modulated(x, gain, shift, scale):
    ms = jnp.mean(x * x, axis=-1, keepdims=True)
    y = x * lax.rsqrt(ms + EPS)
    return (y * gain) * (1.0 + scale) + shift


def _dot(a, b):
    return jnp.dot(a, b, preferred_element_type=F32)


def _dot_nt(a, b):
    return lax.dot_general(a, b, (((1,), (1,)), ((), ())), preferred_element_type=F32)


def _dot_tn(a, b):
    return lax.dot_general(a, b, (((0,), (0,)), ((), ())), preferred_element_type=F32)


def _split_bf16(a):
    hi = a.astype(BF16)
    lo = (a - hi.astype(F32)).astype(BF16)
    return hi, lo


def _dot_hi(a, b):
    ah, al = _split_bf16(a)
    bh, bl = _split_bf16(b)
    return _dot(ah, bh) + (_dot(ah, bl) + _dot(al, bh))


def _ada_kernel(c_ref, w_ref, b_ref, o_ref):
    c = c_ref[...]
    act = jax.nn.silu(c).astype(BF16)
    o_ref[...] = _dot(act, w_ref[...].astype(BF16)) + b_ref[...]


def _ada(c_pad, w, b):
    ly, d, n = w.shape
    s = c_pad.shape[0]
    tn = min(TN_ADA, n)
    return pl.pallas_call(
        _ada_kernel,
        out_shape=jax.ShapeDtypeStruct((ly, s, n), F32),
        grid=(ly, n // tn),
        in_specs=[
            pl.BlockSpec((s, d), lambda l, j: (0, 0)),
            pl.BlockSpec((None, d, tn), lambda l, j: (l, 0, j)),
            pl.BlockSpec((None, 1, tn), lambda l, j: (l, 0, j)),
        ],
        out_specs=pl.BlockSpec((None, s, tn), lambda l, j: (l, 0, j)),
        compiler_params=_params("parallel", "parallel"),
        name="ada_rows",
    )(c_pad, w, b.reshape(ly, 1, n))


def _modulate_into(h_ref, x_ref, gain_ref, sh_ref, sc_ref):
    tm = x_ref.shape[0]
    amp = gain_ref[...] * (1.0 + sc_ref[...])
    shift = sh_ref[...]

    def body(r, carry):
        rows = pl.ds(pl.multiple_of(r * MOD_ROWS, MOD_ROWS), MOD_ROWS)
        x = x_ref[rows, :]
        ms = jnp.mean(x * x, axis=-1, keepdims=True)
        h_ref[rows, :] = ((x * lax.rsqrt(ms + EPS)) * amp + shift).astype(BF16)
        return carry

    lax.fori_loop(0, tm // MOD_ROWS, body, 0, unroll=MOD_UNROLL)


def _ffn_kernel(x_ref, gain_ref, sh_ref, sc_ref, gt_ref, wg_ref, wu_ref, wo_ref, o_ref, h_ref, *, nf):
    f = pl.program_id(1)

    @pl.when(f == 0)
    def _():
        _modulate_into(h_ref, x_ref, gain_ref, sh_ref, sc_ref)
        o_ref[...] = jnp.zeros_like(o_ref)

    h = h_ref[...]
    g = _dot(h, wg_ref[...])
    u = _dot(h, wu_ref[...])
    a = (jax.nn.silu(g) * u).astype(BF16)
    o_ref[...] += _dot(a, wo_ref[...])

    @pl.when(f == nf - 1)
    def _():
        o_ref[...] = x_ref[...] + (0.5 * gt_ref[...]) * o_ref[...]


def _row_spec(groups, tm, d):
    return pl.BlockSpec((None, 1, d), lambda i, j: (_seq_info(i * tm, groups)[0], 0, 0))


def _ffn(x, gain, shift, scale, gate, w_in, w_out, layer, groups):
    t, d = x.shape
    ff = w_out.shape[1]
    tm = min(TM_FFN, groups[0][1])
    tf = min(TF_FFN, ff)
    nf = ff // tf
    row = _row_spec(groups, tm, d)
    return pl.pallas_call(
        functools.partial(_ffn_kernel, nf=nf),
        out_shape=jax.ShapeDtypeStruct((t, d), F32),
        grid=(t // tm, nf),
        in_specs=[
            pl.BlockSpec((tm, d), lambda i, f: (i, 0)),
            pl.BlockSpec((1, d), lambda i, f: (0, 0)),
            row, row, row,
            pl.BlockSpec((None, d, tf), lambda i, f: (layer, 0, f)),
            pl.BlockSpec((None, d, tf), lambda i, f: (layer, 0, nf + f)),
            pl.BlockSpec((None, tf, d), lambda i, f: (layer, f, 0)),
        ],
        out_specs=pl.BlockSpec((tm, d), lambda i, f: (i, 0)),
        scratch_shapes=[pltpu.VMEM((tm, d), BF16)],
        compiler_params=_params("parallel", "arbitrary"),
        name="ffn",
    )(x, gain, shift, scale, gate, w_in, w_in, w_out)


def _proj_kernel(*refs, odd):
    if odd:
        x_ref, gain_ref, sh_ref, sc_ref, w_ref, wn_ref, cos_ref, sin_ref, o_ref, on_ref, h_ref = refs
    else:
        x_ref, gain_ref, sh_ref, sc_ref, w_ref, o_ref, h_ref = refs
    j = pl.program_id(1)

    @pl.when(j == 0)
    def _():
        _modulate_into(h_ref, x_ref, gain_ref, sh_ref, sc_ref)
        if odd:
            on_ref[...] = _dot(h_ref[...], wn_ref[...])

    y = _dot(h_ref[...], w_ref[...])
    if not odd:
        o_ref[...] = y.astype(BF16)
        return

    tn = o_ref.shape[1]
    rope_tiles = 2 * D_HALF // tn
    half = RET_D // 2
    rotate = j < rope_tiles
    cos = jnp.where(rotate, cos_ref[...], 1.0)
    sin = jnp.where(rotate, sin_ref[...], 0.0)
    k_scale = jnp.where(rotate & (j >= rope_tiles // 2), RET_D ** -0.5, 1.0).astype(F32)
    for hd in range(tn // RET_D):
        x1 = y[:, hd * RET_D:hd * RET_D + half]
        x2 = y[:, hd * RET_D + half:(hd + 1) * RET_D]
        o_ref[:, hd * RET_D:hd * RET_D + half] = ((x1 * cos - x2 * sin) * k_scale).astype(BF16)
        o_ref[:, hd * RET_D + half:(hd + 1) * RET_D] = ((x1 * sin + x2 * cos) * k_scale).astype(BF16)


def _proj(x, gain, shift, scale, w, idx, n, groups, odd_extras=None):
    t, d = x.shape
    tm = min(TM_PROJ, groups[0][1])
    tn = TN_PROJ
    row = _row_spec(groups, tm, d)
    odd = odd_extras is not None
    in_specs = [
        pl.BlockSpec((tm, d), lambda i, j: (i, 0)),
        pl.BlockSpec((1, d), lambda i, j: (0, 0)),
        row, row,
        pl.BlockSpec((None, d, tn), lambda i, j: (idx, 0, j)),
    ]
    out_shape = jax.ShapeDtypeStruct((t, n), BF16)
    out_specs = pl.BlockSpec((tm, tn), lambda i, j: (i, j))
    args = (x, gain, shift, scale, w)
    if odd:
        half = RET_D // 2

        def pos_block(i, j):
            row0 = i * tm
            return ((row0 - _seq_info(row0, groups)[1]) // tm, 0)

        tab = pl.BlockSpec((tm, half), pos_block)
        in_specs += [pl.BlockSpec((None, d, LANES), lambda i, j: (idx, 0, 0)), tab, tab]
        out_shape = (out_shape, jax.ShapeDtypeStruct((t, LANES), F32))
        out_specs = (out_specs, pl.BlockSpec((tm, LANES), lambda i, j: (i, 0)))
        args = args + tuple(odd_extras)
    return pl.pallas_call(
        functools.partial(_proj_kernel, odd=odd),
        out_shape=out_shape,
        grid=(t // tm, n // tn),
        in_specs=in_specs,
        out_specs=out_specs,
        scratch_shapes=[pltpu.VMEM((tm, d), BF16)],
        compiler_params=_params("parallel", "arbitrary"),
        name="mix_proj",
    )(*args)


def _outproj_kernel(x_ref, ya_ref, yb_ref, gt_ref, wa_ref, wb_ref, o_ref):
    y = _dot(ya_ref[...], wa_ref[...]) + _dot(yb_ref[...], wb_ref[...])
    o_ref[...] = x_ref[...] + gt_ref[...] * y


def _outproj(x, ya, yb, ca, cb, gate, w, idx, groups):
    t, d = x.shape
    dh = d // 2
    tm = min(TM_OUT, groups[0][1])
    row = _row_spec(groups, tm, d)
    return pl.pallas_call(
        _outproj_kernel,
        out_shape=jax.ShapeDtypeStruct((t, d), F32),
        grid=(t // tm, 1),
        in_specs=[
            pl.BlockSpec((tm, d), lambda i, j: (i, 0)),
            pl.BlockSpec((tm, dh), lambda i, j: (i, ca)),
            pl.BlockSpec((tm, dh), lambda i, j: (i, cb)),
            row,
            pl.BlockSpec((None, dh, d), lambda i, j: (idx, 0, 0)),
            pl.BlockSpec((None, dh, d), lambda i, j: (idx, 1, 0)),
        ],
        out_specs=pl.BlockSpec((tm, d), lambda i, j: (i, 0)),
        compiler_params=_params("parallel", "arbitrary"),
        name="mix_out",
    )(x, ya, yb, gate, w, w)


def _final_kernel(x_ref, gain_ref, sh_ref, sc_ref, o0_ref, o1_ref, *, n0):
    y = _modulated(x_ref[...], gain_ref[...], sh_ref[...], sc_ref[...])
    i = pl.program_id(0)

    @pl.when(i < n0)
    def _():
        o0_ref[...] = y

    @pl.when(i >= n0)
    def _():
        o1_ref[...] = y


def _final(x, gain, shift, scale, groups):
    t, d = x.shape
    tm = min(TM_FINAL, groups[0][1])
    (b0, l0), (b1, l1) = groups
    n0 = b0 * l0 // tm
    row = _row_spec(groups, tm, d)
    return pl.pallas_call(
        functools.partial(_final_kernel, n0=n0),
        out_shape=(jax.ShapeDtypeStruct((b0 * l0, d), F32), jax.ShapeDtypeStruct((b1 * l1, d), F32)),
        grid=(t // tm, 1),
        in_specs=[
            pl.BlockSpec((tm, d), lambda i, j: (i, 0)),
            pl.BlockSpec((1, d), lambda i, j: (0, 0)),
            row, row,
        ],
        out_specs=(pl.BlockSpec((tm, d), lambda i, j: (jnp.minimum(i, n0 - 1), 0)),
                   pl.BlockSpec((tm, d), lambda i, j: (jnp.maximum(i - n0, 0), 0))),
        compiler_params=_params("arbitrary", "arbitrary"),
        name="final_mod",
    )(x, gain, shift, scale)


def _halo_specs(tl, width, nrows, col_of):
    per = tl // HALO
    last = nrows // HALO - 1
    prev = pl.BlockSpec((HALO, width), lambda i, *r: (jnp.maximum(i * per - 1, 0), col_of(i, *r)))
    nxt = pl.BlockSpec((HALO, width), lambda i, *r: (jnp.minimum((i + 1) * per, last), col_of(i, *r)))
    return prev, nxt


def _fill_ext(ext_ref, x_ref, prev_ref, next_ref, first, last, tl):
    dt = ext_ref.dtype
    ext_ref[HALO:HALO + tl, :] = x_ref[...].astype(dt)
    ext_ref[0:HALO, :] = jnp.where(first, jnp.zeros_like(prev_ref), prev_ref[...]).astype(dt)
    ext_ref[HALO + tl:2 * HALO + tl, :] = jnp.where(last, jnp.zeros_like(next_ref), next_ref[...]).astype(dt)


def _row_window(tl, lo, hi):
    r = lax.broadcasted_iota(jnp.int32, (tl, tl + 2 * HALO), 0) + HALO
    col = lax.broadcasted_iota(jnp.int32, (tl, tl + 2 * HALO), 1)
    return ((col >= r + lo) & (col < r + hi)).astype(BF16)


def _even_kernel(xa_ref, prev_ref, next_ref, u_ref, v_ref, pw_ref, ps_ref, ng_ref, sw_ref, sb_ref,
                 o_ref, ext_ref, vn_ref, win_ref, *, tl, groups):
    @pl.when(pl.program_id(0) == 0)
    def _():
        for gi, w in enumerate(POOL_WINDOWS):
            win_ref[gi] = _row_window(tl, -(w // 2), w - w // 2)

    row0 = pl.program_id(0) * tl
    _, sstart, slen = _seq_info(row0, groups)
    pos0 = row0 - sstart
    _fill_ext(ext_ref, xa_ref, prev_ref, next_ref, pos0 == 0, pos0 + tl == slen, tl)

    t = pos0 + lax.broadcasted_iota(jnp.int32, (tl, 1), 0)
    for gi, w in enumerate(POOL_WINDOWS):
        c0 = gi * POOL_GROUP
        cols = slice(c0, c0 + POOL_GROUP)
        s = _dot(win_ref[gi], ext_ref[:, cols])
        lo = jnp.clip(t - w // 2, 0, slen)
        hi = jnp.clip(t + (w - w // 2), 0, slen)
        cnt = (hi - lo).astype(F32)
        pooled = (s / cnt - xa_ref[:, cols].astype(F32)).astype(BF16)
        ya = _dot(pooled, pw_ref[gi]) * ps_ref[:, cols]
        o_ref[:, cols] = ya.astype(BF16)

    v = jax.nn.gelu(v_ref[...].astype(F32))
    vms = jnp.mean(v * v, axis=-1, keepdims=True)
    vn_ref[...] = ((v * lax.rsqrt(vms + EPS)) * ng_ref[...]).astype(BF16)
    for n in range(tl // SGU_CHUNK):
        rows = slice(n * SGU_CHUNK, (n + 1) * SGU_CHUNK)
        for g in range(SGU_GROUPS):
            cols = slice(g * SGU_HEAD, (g + 1) * SGU_HEAD)
            mixed = _dot(sw_ref[g], vn_ref[rows, cols]) + sb_ref[g]
            u = jax.nn.gelu(u_ref[rows, cols].astype(F32))
            o_ref[rows, D_HALF + g * SGU_HEAD:D_HALF + (g + 1) * SGU_HEAD] = (u * mixed).astype(BF16)


def _even_mix(proj, pool_w, pool_scale, sgu_norm, sgu_w, sgu_b, groups):
    t = proj.shape[0]
    tl = min(TL_EVEN, groups[0][1])
    prev, nxt = _halo_specs(tl, D_HALF, t, lambda i: 0)
    const2 = lambda i: (0, 0)
    const3 = lambda i: (0, 0, 0)
    return pl.pallas_call(
        functools.partial(_even_kernel, tl=tl, groups=groups),
        out_shape=jax.ShapeDtypeStruct((t, 2 * D_HALF), BF16),
        grid=(t // tl,),
        in_specs=[
            pl.BlockSpec((tl, D_HALF), lambda i: (i, 0)),
            prev, nxt,
            pl.BlockSpec((tl, D_HALF), lambda i: (i, 1)),
            pl.BlockSpec((tl, D_HALF), lambda i: (i, 2)),
            pl.BlockSpec(pool_w.shape, const3),
            pl.BlockSpec((1, D_HALF), const2),
            pl.BlockSpec((1, D_HALF), const2),
            pl.BlockSpec(sgu_w.shape, const3),
            pl.BlockSpec(sgu_b.shape, const3),
        ],
        out_specs=pl.BlockSpec((tl, 2 * D_HALF), lambda i: (i, 0)),
        scratch_shapes=[pltpu.VMEM((tl + 2 * HALO, D_HALF), BF16), pltpu.VMEM((tl, D_HALF), BF16),
                        pltpu.VMEM((len(POOL_WINDOWS), tl, tl + 2 * HALO), BF16)],
        compiler_params=_params("arbitrary"),
        name="even_mix",
    )(proj, proj, proj, proj, proj, pool_w, pool_scale, sgu_norm, sgu_w, sgu_b)


def _rope_kernel(inv_ref, cos_ref, sin_ref, *, tl):
    pos = (pl.program_id(0) * tl + lax.broadcasted_iota(jnp.int32, (tl, 1), 0)).astype(F32)
    ang = pos * inv_ref[...]
    cos_ref[...] = jnp.cos(ang)
    sin_ref[...] = jnp.sin(ang)


def _rope_tables(max_len, tl):
    half = RET_D // 2
    inv = (1.0 / (ROPE_BASE ** jnp.linspace(0.0, 1.0, half, dtype=F32))).reshape(1, half)
    shp = jax.ShapeDtypeStruct((max_len, half), F32)
    return pl.pallas_call(
        functools.partial(_rope_kernel, tl=tl),
        out_shape=(shp, shp),
        grid=(max_len // tl,),
        in_specs=[pl.BlockSpec((1, half), lambda i: (0, 0))],
        out_specs=(pl.BlockSpec((tl, half), lambda i: (i, 0)), pl.BlockSpec((tl, half), lambda i: (i, 0))),
        compiler_params=_params("parallel"),
        name="rope_table",
    )(inv)


def _ret_kernel(*refs, tl, nt, groups, reverse):
    if reverse:
        q_ref, k_ref, v_ref, dec_ref, o_ref, s_ref = refs
    else:
        q_ref, k_ref, v_ref, dec_ref, decb_ref, g_ref, ob_ref, ng_ref, o_ref, s_ref = refs
    j = pl.program_id(0)
    it = nt - 1 - j if reverse else j
    row0 = it * tl
    _, sstart, slen = _seq_info(row0, groups)
    pos0 = row0 - sstart
    reset = (pos0 + tl == slen) if reverse else (pos0 == 0)

    @pl.when(reset)
    def _():
        s_ref[...] = jnp.zeros_like(s_ref)

    c = RET_CHUNK
    heads = range(RET_HEADS)
    idx = lax.broadcasted_iota(jnp.int32, (c, 1), 0).astype(F32)
    lg = [jnp.log1p(-jnp.exp2(-dec_ref[h])) for h in heads]
    if reverse:
        q_dec = [jnp.exp(x * (c - idx)) for x in lg]
        k_dec = [jnp.exp(x * idx) for x in lg]
    else:
        q_dec = [jnp.exp(x * (idx + 1.0)) for x in lg]
        k_dec = [jnp.exp(x * (c - 1.0 - idx)) for x in lg]
        lgb = [jnp.log1p(-jnp.exp2(-decb_ref[h])) for h in heads]
        ri = lax.broadcasted_iota(jnp.int32, (c, c), 0)
        ci = lax.broadcasted_iota(jnp.int32, (c, c), 1)
        rel = (ri - ci).astype(F32)
        dmat = [jnp.where(rel >= 0, jnp.exp(x * jnp.maximum(rel, 0.0)), 0.0)
                + jnp.where(rel <= 0, jnp.exp(y * jnp.maximum(-rel, 0.0)), 0.0) for x, y in zip(lg, lgb)]
    chunk_dec = [jnp.exp(x * float(c)) for x in lg]

    nc = tl // c
    order = list(range(nc - 1, -1, -1) if reverse else range(nc))
    units = [(ch, h) for ch in order for h in heads]

    def rows(ch):
        return slice(ch * c, (ch + 1) * c)

    def cols(h):
        return slice(h * RET_D, (h + 1) * RET_D)

    q = [q_ref[rows(ch), cols(h)] for ch, h in units]
    v = [v_ref[rows(ch), cols(h)] for ch, h in units]
    kv = [_dot_tn(k_ref[rows(ch), cols(h)], (x.astype(F32) * k_dec[h]).astype(BF16))
          for (ch, h), x in zip(units, v)]
    if not reverse:
        scores = [(_dot_nt(x, k_ref[rows(ch), cols(h)]) * dmat[h]).astype(BF16) for (ch, h), x in zip(units, q)]
        intra = [_dot(x, y) for x, y in zip(scores, v)]

    for ci_, ch in enumerate(order):
        s = [s_ref[h] for h in heads]
        inter = [_dot(q[ci_ * RET_HEADS + h], s[h].astype(BF16)) * q_dec[h] for h in heads]
        for h in heads:
            s_ref[h] = s[h] * chunk_dec[h] + kv[ci_ * RET_HEADS + h]
        for h in heads:
            if reverse:
                o_ref[rows(ch), cols(h)] = inter[h]
            else:
                o = intra[ci_ * RET_HEADS + h] + inter[h] + ob_ref[rows(ch), cols(h)]
                mu = jnp.mean(o, axis=-1, keepdims=True)
                var = jnp.mean(jnp.square(o - mu), axis=-1, keepdims=True)
                on = ((o - mu) * lax.rsqrt(var + EPS)) * ng_ref[:, cols(h)]
                gate = jax.nn.silu(g_ref[rows(ch), cols(h)].astype(F32))
                o_ref[rows(ch), cols(h)] = (gate * on).astype(BF16)


def _retention(proj, decay_f, decay_b, norm_g, groups):
    t = proj.shape[0]
    tl = min(TL_RET, groups[0][1])
    nt = t // tl
    dec = pl.BlockSpec((RET_HEADS, 1, 1), lambda j: (0, 0, 0))
    state = pltpu.VMEM((RET_HEADS, RET_D, RET_D), F32)

    def section(reverse):
        tile = (lambda j: nt - 1 - j) if reverse else (lambda j: j)
        return lambda s: pl.BlockSpec((tl, D_HALF), lambda j: (tile(j), s))

    sec = section(True)
    ob = pl.pallas_call(
        functools.partial(_ret_kernel, tl=tl, nt=nt, groups=groups, reverse=True),
        out_shape=jax.ShapeDtypeStruct((t, D_HALF), F32),
        grid=(nt,),
        in_specs=[sec(0), sec(1), sec(2), dec],
        out_specs=sec(0),
        scratch_shapes=[state],
        compiler_params=_params("arbitrary"),
        name="ret_bwd",
    )(proj, proj, proj, decay_b.reshape(RET_HEADS, 1, 1))

    sec = section(False)
    return pl.pallas_call(
        functools.partial(_ret_kernel, tl=tl, nt=nt, groups=groups, reverse=False),
        out_shape=jax.ShapeDtypeStruct((t, D_HALF), BF16),
        grid=(nt,),
        in_specs=[sec(0), sec(1), sec(2), dec, dec, sec(3), sec(0),
                  pl.BlockSpec((1, D_HALF), lambda j: (0, 0))],
        out_specs=sec(0),
        scratch_shapes=[state],
        compiler_params=_params("arbitrary"),
        name="ret_fwd",
    )(proj, proj, proj, decay_f.reshape(RET_HEADS, 1, 1), decay_b.reshape(RET_HEADS, 1, 1),
      proj, ob, norm_g)


def _dnprep_kernel(x_ref, prev_ref, next_ref, w_ref, o_ref, ext_ref, *, tl, groups):
    row0 = pl.program_id(0) * tl
    part = pl.program_id(1)
    _, sstart, slen = _seq_info(row0, groups)
    pos0 = row0 - sstart
    _fill_ext(ext_ref, x_ref, prev_ref, next_ref, pos0 == 0, pos0 + tl == slen, tl)
    left = DN_CONV // 2
    q_scale = jnp.where(part == 0, DN_D ** -0.5, 1.0).astype(F32)
    for h in range(DN_HEADS):
        cols = slice(h * DN_D, (h + 1) * DN_D)
        conv = ext_ref[HALO - left:HALO - left + tl, cols] * w_ref[0:1, cols]
        for tap in range(1, DN_CONV):
            r0 = HALO - left + tap
            conv = conv + ext_ref[r0:r0 + tl, cols] * w_ref[tap:tap + 1, cols]
        y = jax.nn.silu(conv)
        inv_norm = lax.rsqrt(jnp.sum(y * y, axis=-1, keepdims=True) + EPS)
        o_ref[:, cols] = (y * jnp.where(part < 2, inv_norm * q_scale, 1.0)).astype(BF16)


def _dn_prep(proj, conv_w, groups):
    t = proj.shape[0]
    tl = min(TL_PREP, groups[0][1])
    base = P_RET // D_HALF
    prev, nxt = _halo_specs(tl, D_HALF, t, lambda i, part: base + part)
    return pl.pallas_call(
        functools.partial(_dnprep_kernel, tl=tl, groups=groups),
        out_shape=jax.ShapeDtypeStruct((t, 3 * D_HALF), BF16),
        grid=(t // tl, 3),
        in_specs=[pl.BlockSpec((tl, D_HALF), lambda i, part: (i, base + part)), prev, nxt,
                  pl.BlockSpec((DN_CONV, D_HALF), lambda i, part: (0, part))],
        out_specs=pl.BlockSpec((tl, D_HALF), lambda i, part: (i, part)),
        scratch_shapes=[pltpu.VMEM((tl + 2 * HALO, D_HALF), F32)],
        compiler_params=_params("parallel", "parallel"),
        name="dn_prep",
    )(proj, proj, proj, conv_w)


def _dot_hi_each(lhs_parts, rhs_parts):
    m = lhs_parts[0][0].shape[0]
    n = rhs_parts[0][0].shape[1]
    quads = [_dot(jnp.concatenate(a, axis=0), jnp.concatenate(b, axis=1)) for a, b in zip(lhs_parts, rhs_parts)]
    return [(x[:m, :n] + x[m:, :n]) + (x[:m, n:] + x[m:, n:]) for x in quads]


def _block_diag(y, left):
    zero = jnp.zeros_like(y)
    return jnp.concatenate([jnp.where(left, y, zero), jnp.where(left, zero, y)], axis=0)


def _pair_products(lhs, rhs, left):
    ls = [_split_bf16(x) for x in lhs]
    rs = [tuple(_block_diag(part, left) for part in _split_bf16(y)) for y in rhs]
    return _dot_hi_each(ls, rs)


def _unit_triangular_inverses(mats, eye, ri, ci, left):
    size = SUBLANES
    same = (ri // size) == (ci // size)
    ps = [jnp.where(same, a, 0.0) for a in mats]
    invs = [eye - d for d in ps]
    n = 2
    while n < size:
        ps = _pair_products(ps, ps, left)
        invs = [inv + x for inv, x in zip(invs, _pair_products(invs, ps, left))]
        n *= 2
    while size < DN_CHUNK:
        size *= 2
        merged = (ri // size) == (ci // size)
        es = [jnp.where(merged & ~same, a, 0.0) for a in mats]
        invs = [inv - x for inv, x in zip(invs, _pair_products(_pair_products(invs, es, left), invs, left))]
        same = merged
    return invs


def _dn_kernel(*refs, tl, nt, groups, reverse):
    if reverse:
        q_ref, k_ref, v_ref, gates_ref, alog_ref, dtb_ref, o_ref, s_ref = refs
    else:
        (q_ref, k_ref, v_ref, gates_ref, alog_ref, dtb_ref, z_ref, ob_ref, ng_ref,
         o_ref, s_ref) = refs
    j = pl.program_id(0)
    it = nt - 1 - j if reverse else j
    row0 = it * tl
    _, sstart, slen = _seq_info(row0, groups)
    pos0 = row0 - sstart
    reset = (pos0 + tl == slen) if reverse else (pos0 == 0)

    @pl.when(reset)
    def _():
        s_ref[...] = jnp.zeros_like(s_ref)

    gates = gates_ref[...]
    beta_all = jax.nn.sigmoid(gates)
    la_all = -jnp.exp(alog_ref[...]) * jax.nn.softplus(gates + dtb_ref[...])
    cb0 = DN_HEADS if reverse else 0
    ca0 = cb0 + 2 * DN_HEADS

    c = DN_CHUNK
    left = lax.broadcasted_iota(jnp.int32, (1, 2 * c), 1) < c
    ri = lax.broadcasted_iota(jnp.int32, (c, 2 * c), 0)
    ci = jnp.bitwise_and(lax.broadcasted_iota(jnp.int32, (c, 2 * c), 1), c - 1)
    eye = (ri == ci).astype(F32)
    incl = (ri <= ci) if reverse else (ri >= ci)
    strict = (ri < ci) if reverse else (ri > ci)
    incl_t = (ri >= ci) if reverse else (ri <= ci)
    last = 0 if reverse else c - 1

    nc = tl // c
    order = list(range(nc - 1, -1, -1) if reverse else range(nc))
    heads = range(DN_HEADS)
    half_heads = range(DN_HEADS // 2)
    units = [(ch, h) for ch in order for h in heads]
    pairs = range(len(units) // 2)

    def rows(ch):
        return slice(ch * c, (ch + 1) * c)

    def cols(h):
        return slice(h * DN_D, (h + 1) * DN_D)

    def paired(xs):
        return [jnp.where(left, xs[2 * p], xs[2 * p + 1]) for p in pairs]

    q = [q_ref[rows(ch), cols(h)] for ch, h in units]
    k = [k_ref[rows(ch), cols(h)] for ch, h in units]
    beta = [beta_all[rows(ch), cb0 + h:cb0 + h + 1] for ch, h in units]
    la = [la_all[rows(ch), ca0 + h:ca0 + h + 1] for ch, h in units]
    la2 = paired(la)
    la_row = [jnp.sum(eye * x, axis=0, keepdims=True) for x in la2]
    g_row = [jnp.sum(jnp.where(incl_t, x, 0.0), axis=0, keepdims=True) for x in la2]
    windowed = [jnp.where(incl, x, 0.0) for x in la_row]
    g_col = [jnp.sum(jnp.where(left == (u % 2 == 0), windowed[u // 2], 0.0), axis=1, keepdims=True)
             for u in range(len(units))]
    gam = [jnp.where(incl, jnp.exp(jnp.where(incl, gc - gr, 0.0)), 0.0) for gc, gr in zip(paired(g_col), g_row)]
    eg = [jnp.exp(gc) for gc in g_col]
    g_last = [gc[last:last + 1, :] for gc in g_col]
    kb = [x.astype(F32) * b for x, b in zip(k, beta)]
    kq = [_dot_nt(jnp.concatenate([kb[2 * p].astype(BF16), q[2 * p], kb[2 * p + 1].astype(BF16), q[2 * p + 1]],
                                  axis=0),
                  jnp.concatenate([k[2 * p], k[2 * p + 1]], axis=0)) for p in pairs]
    a = [jnp.where(strict, jnp.where(left, x[:c], x[2 * c:3 * c]) * gm, 0.0) for x, gm in zip(kq, gam)]
    attn = [_block_diag((jnp.where(left, x[c:2 * c], x[3 * c:]) * gm).astype(BF16), left)
            for x, gm in zip(kq, gam)]
    tinv = _unit_triangular_inverses(a, eye, ri, ci, left)
    rhs = [jnp.concatenate([v_ref[rows(ch), cols(h)].astype(F32) * b, x * e], axis=1)
           for (ch, h), b, x, e in zip(units, beta, kb, eg)]
    uw2 = _dot_hi_each([tuple(_block_diag(part, left) for part in _split_bf16(t)) for t in tinv],
                       [_split_bf16(jnp.concatenate([rhs[2 * p], rhs[2 * p + 1]], axis=0)) for p in pairs])
    uw = [uw2[u // 2][(u % 2) * c:(u % 2 + 1) * c] for u in range(len(units))]
    tail = [jnp.exp(gl - gc) for gl, gc in zip(g_last, g_col)]
    dec = [jnp.exp(gl) for gl in g_last]

    for ci_, ch in enumerate(order):
        idx = [ci_ * DN_HEADS + h for h in heads]
        s = [s_ref[h] for h in heads]
        sb = [x.astype(BF16) for x in s]
        ws = [_dot(jnp.concatenate([uw[i][:, DN_D:].astype(BF16), q[i]], axis=0), sb[h])
              for h, i in zip(heads, idx)]
        v_new = [uw[i][:, :DN_D] - x[:c] for i, x in zip(idx, ws)]
        av = [_dot(attn[idx[2 * x] // 2],
                   jnp.concatenate([v_new[2 * x].astype(BF16), v_new[2 * x + 1].astype(BF16)], axis=0))
              for x in half_heads]
        o = [ws[h][c:] * eg[idx[h]] + av[h // 2][(h % 2) * c:(h % 2 + 1) * c] for h in heads]
        for h, i in zip(heads, idx):
            s_ref[h] = s[h] * dec[i] + _dot_tn(k[i], (v_new[h] * tail[i]).astype(BF16))
        for h in heads:
            if reverse:
                o_ref[rows(ch), cols(h)] = o[h]
            else:
                oo = o[h] + ob_ref[rows(ch), cols(h)]
                on = (oo * lax.rsqrt(jnp.mean(oo * oo, axis=-1, keepdims=True) + EPS)) * ng_ref[...]
                z = z_ref[rows(ch), cols(h)].astype(F32)
                o_ref[rows(ch), cols(h)] = (on * jax.nn.silu(z)).astype(BF16)


def _deltanet(proj, gate_cols, qkv, alog_row, dtb_row, norm_g, groups):
    t = proj.shape[0]
    tl = min(TL_DN, groups[0][1])
    nt = t // tl
    z_blk = (P_RET + 3 * D_HALF) // D_HALF
    row = pl.BlockSpec((1, LANES), lambda j: (0, 0))
    state = pltpu.VMEM((DN_HEADS, DN_D, DN_D), F32)

    def common(reverse):
        tile = (lambda j: nt - 1 - j) if reverse else (lambda j: j)
        sec = lambda s: pl.BlockSpec((tl, D_HALF), lambda j: (tile(j), s))
        gates = pl.BlockSpec((tl, LANES), lambda j: (tile(j), 0))
        return sec, gates

    sec, gates = common(True)
    ob = pl.pallas_call(
        functools.partial(_dn_kernel, tl=tl, nt=nt, groups=groups, reverse=True),
        out_shape=jax.ShapeDtypeStruct((t, D_HALF), F32),
        grid=(nt,),
        in_specs=[sec(0), sec(1), sec(2), gates, row, row],
        out_specs=sec(0),
        scratch_shapes=[state],
        compiler_params=_params("arbitrary"),
        name="dn_bwd",
    )(qkv, qkv, qkv, gate_cols, alog_row, dtb_row)

    sec, gates = common(False)
    return pl.pallas_call(
        functools.partial(_dn_kernel, tl=tl, nt=nt, groups=groups, reverse=False),
        out_shape=jax.ShapeDtypeStruct((t, D_HALF), BF16),
        grid=(nt,),
        in_specs=[sec(0), sec(1), sec(2), gates, row, row, sec(z_blk), sec(0), row],
        out_specs=sec(0),
        scratch_shapes=[state],
        compiler_params=_params("arbitrary"),
        name="dn_fwd",
    )(qkv, qkv, qkv, gate_cols, alog_row, dtb_row, proj, ob, norm_g)


def _gate_row(f_vals, b_vals):
    row = jnp.zeros((LANES,), F32)
    row = row.at[2 * DN_HEADS:3 * DN_HEADS].set(f_vals.astype(F32))
    row = row.at[3 * DN_HEADS:4 * DN_HEADS].set(b_vals.astype(F32))
    return row.reshape(1, LANES)


def _trunk(x, c, p, groups):
    t, d = x.shape
    n_seq = c.shape[0]
    c_pad = jnp.zeros((SEQ_PAD, d), F32).at[:n_seq].set(c)
    mods = _ada(c_pad, p['w_ada'], p['b_ada'])
    mods = mods.reshape(DEPTH, SEQ_PAD, N_MOD, 1, d).transpose(0, 2, 1, 3, 4)
    fin = _ada(c_pad, p['w_ada_final'][None], p['b_ada_final'][None])
    fin = fin.reshape(SEQ_PAD, 2, 1, d).transpose(1, 0, 2, 3)

    max_len = max(ln for _, ln in groups)
    cos, sin = _rope_tables(max_len, min(TL_RET, groups[0][1]))

    row = lambda a: a.reshape(1, -1)
    for layer in range(DEPTH):
        sh1, sc1, g1, sh2, sc2, g2, sh3, sc3, g3 = [mods[layer, jm] for jm in range(N_MOD)]
        x = _ffn(x, row(p['norm_ffn1'][layer]), sh1, sc1, g1, p['w_ffn1_in'], p['w_ffn1_out'], layer, groups)
        idx = layer // 2
        gain = row(p['norm_mix'][layer])
        if layer % 2 == 0:
            proj = _proj(x, gain, sh2, sc2, p['w_in_even'], idx, P_EVEN, groups)
            y = _even_mix(proj, p['pool_w'][idx], row(p['pool_scale'][idx]), row(p['sgu_norm'][idx]),
                          p['sgu_w'][idx], p['sgu_b'][idx][..., None], groups)
            x = _outproj(x, y, y, 0, 1, g2, p['w_out_even'], idx, groups)
        else:
            proj, gate_cols = _proj(x, gain, sh2, sc2, p['w_in_odd'], idx, P_ODD_MAIN, groups,
                                    odd_extras=(p['w_in_odd_gates'], cos, sin))
            yc = _retention(proj, p['ret_decay_f'][idx], p['ret_decay_b'][idx],
                            row(p['ret_norm'][idx]), groups)
            qkv = _dn_prep(proj, p['dn_conv'][idx], groups)
            yd = _deltanet(proj, gate_cols, qkv,
                           _gate_row(p['dn_a_log_f'][idx], p['dn_a_log_b'][idx]),
                           _gate_row(p['dn_dt_bias_f'][idx], p['dn_dt_bias_b'][idx]),
                           row(p['dn_norm'][idx]), groups)
            x = _outproj(x, yc, yd, 0, 0, g2, p['w_out_odd'], idx, groups)
        x = _ffn(x, row(p['norm_ffn2'][layer]), sh3, sc3, g3, p['w_ffn2_in'], p['w_ffn2_out'], layer, groups)
    return _final(x, row(p['norm_final']), fin[0], fin[1], groups)


def _prepare(p):
    q = dict(p)
    for name in ('w_ffn1_in', 'w_ffn1_out', 'w_ffn2_in', 'w_ffn2_out', 'w_in_even', 'w_out_even',
                 'pool_w', 'sgu_w', 'w_in_odd', 'w_out_odd'):
        q[name] = p[name].astype(BF16)
    gates = q['w_in_odd'][:, :, P_ODD_MAIN:]
    q['w_in_odd_gates'] = jnp.pad(gates, ((0, 0), (0, 0), (0, LANES - gates.shape[-1])))
    return q


def kernel(x_prompt, x_sample, c_prompt, c_sample, w_ada, b_ada, norm_ffn1, w_ffn1_in, w_ffn1_out, norm_mix, norm_ffn2, w_ffn2_in, w_ffn2_out, w_in_even, w_out_even, pool_w, pool_scale, sgu_norm, sgu_w, sgu_b, w_in_odd, w_out_odd, ret_decay_f, ret_decay_b, ret_norm, dn_conv, dn_a_log_f, dn_a_log_b, dn_dt_bias_f, dn_dt_bias_b, dn_norm, norm_final, w_ada_final, b_ada_final):
    p = _prepare({
        'w_ada': w_ada, 'b_ada': b_ada, 'norm_ffn1': norm_ffn1, 'w_ffn1_in': w_ffn1_in,
        'w_ffn1_out': w_ffn1_out, 'norm_mix': norm_mix, 'norm_ffn2': norm_ffn2,
        'w_ffn2_in': w_ffn2_in, 'w_ffn2_out': w_ffn2_out, 'w_in_even': w_in_even,
        'w_out_even': w_out_even, 'pool_w': pool_w, 'pool_scale': pool_scale,
        'sgu_norm': sgu_norm, 'sgu_w': sgu_w, 'sgu_b': sgu_b, 'w_in_odd': w_in_odd,
        'w_out_odd': w_out_odd, 'ret_decay_f': ret_decay_f, 'ret_decay_b': ret_decay_b,
        'ret_norm': ret_norm, 'dn_conv': dn_conv, 'dn_a_log_f': dn_a_log_f,
        'dn_a_log_b': dn_a_log_b, 'dn_dt_bias_f': dn_dt_bias_f, 'dn_dt_bias_b': dn_dt_bias_b,
        'dn_norm': dn_norm, 'norm_final': norm_final, 'w_ada_final': w_ada_final,
        'b_ada_final': b_ada_final,
    })
    bp, lp, d = x_prompt.shape
    bs, ls, _ = x_sample.shape
    groups = ((bp, lp), (bs, ls))
    x = jnp.concatenate([x_prompt.reshape(bp * lp, d), x_sample.reshape(bs * ls, d)], axis=0)
    c = jnp.concatenate([c_prompt, c_sample], axis=0)
    y_prompt, y_sample = _trunk(x, c, p, groups)
    return (y_prompt.reshape(bp, lp, d), y_sample.reshape(bs, ls, d))
```

```python
import functools

import jax
import jax.numpy as jnp
from jax import lax
from jax.experimental import pallas as pl
from jax.experimental.pallas import tpu as pltpu

F32 = jnp.float32
BF16 = jnp.bfloat16

D_MODEL = 2048
DEPTH = 4
D_HALF = D_MODEL // 2
POOL_WINDOWS = (2, 4, 8, 16)
POOL_GROUP = D_HALF // len(POOL_WINDOWS)
SGU_CHUNK = 128
SGU_HEAD = 128
SGU_GROUPS = D_HALF // SGU_HEAD
RET_HEADS = 4
RET_D = D_HALF // RET_HEADS
RET_CHUNK = 128
ROPE_BASE = 10000.0
DN_HEADS = 8
DN_D = D_HALF // DN_HEADS
DN_CONV = 4
DN_CHUNK = 64
D_FF = 5632
N_MOD = 9
EPS = 1e-6
P_EVEN = 3 * D_HALF
P_RET = 4 * D_HALF
P_ODD_MAIN = P_RET + 4 * D_HALF
P_ODD = P_ODD_MAIN + 4 * DN_HEADS

LANES = 128
SUBLANES = 8
HALO = 2 * SUBLANES
VMEM_LIMIT = 56 * 1024 * 1024

TM_FFN = 1024
TF_FFN = 512
TM_PROJ = 1024
TN_PROJ = 1024
MOD_ROWS = 16
MOD_UNROLL = 8
TM_OUT = 512
TL_EVEN = 256
TL_RET = 512
TL_DN = 256
TL_PREP = 256
PREP_ROWS = 64
TM_FINAL = 512
TN_ADA = 1024
SEQ_PAD = 16


def _seq_info(row0, groups):
    seq = start = length = None
    t0 = s0 = 0
    for gi, (nb, ln) in enumerate(groups):
        rel = row0 - t0
        q = rel // ln
        if gi == 0:
            seq, start, length = q, q * ln, ln
        else:
            here = row0 >= t0
            seq = jnp.where(here, s0 + q, seq)
            start = jnp.where(here, t0 + q * ln, start)
            length = jnp.where(here, ln, length)
        t0 += nb * ln
        s0 += nb
    return seq, start, length


def _params(*sem):
    return pltpu.CompilerParams(dimension_semantics=sem, vmem_limit_bytes=VMEM_LIMIT)


def _modulated(x, gain, shift, scale):
    ms = jnp.mean(x * x, axis=-1, keepdims=True)
    y = x * lax.rsqrt(ms + EPS)
    return (y * gain) * (1.0 + scale) + shift


def _dot(a, b):
    return jnp.dot(a, b, preferred_element_type=F32)


def _dot_nt(a, b):
    return lax.dot_general(a, b, (((1,), (1,)), ((), ())), preferred_element_type=F32)


def _dot_tn(a, b):
    return lax.dot_general(a, b, (((0,), (0,)), ((), ())), preferred_element_type=F32)


def _split_bf16(a):
    hi = a.astype(BF16)
    lo = (a - hi.astype(F32)).astype(BF16)
    return hi, lo


def _ada_kernel(c_ref, w_ref, b_ref, o_ref):
    c = c_ref[...]
    act = jax.nn.silu(c).astype(BF16)
    o_ref[...] = _dot(act, w_ref[...].astype(BF16)) + b_ref[...]


def _ada(c_pad, w, b):
    ly, d, n = w.shape
    s = c_pad.shape[0]
    tn = min(TN_ADA, n)
    return pl.pallas_call(
        _ada_kernel,
        out_shape=jax.ShapeDtypeStruct((ly, s, n), F32),
        grid=(ly, n // tn),
        in_specs=[
            pl.BlockSpec((s, d), lambda l, j: (0, 0)),
            pl.BlockSpec((None, d, tn), lambda l, j: (l, 0, j)),
            pl.BlockSpec((None, 1, tn), lambda l, j: (l, 0, j)),
        ],
        out_specs=pl.BlockSpec((None, s, tn), lambda l, j: (l, 0, j)),
        compiler_params=_params("parallel", "parallel"),
        name="ada_rows",
    )(c_pad, w, b.reshape(ly, 1, n))


def _modulate_into(h_ref, x_ref, gain_ref, sh_ref, sc_ref, zero_ref=None):
    tm = x_ref.shape[0]
    amp = gain_ref[...] * (1.0 + sc_ref[...])
    shift = sh_ref[...]

    def body(r, carry):
        rows = pl.ds(pl.multiple_of(r * MOD_ROWS, MOD_ROWS), MOD_ROWS)
        x = x_ref[rows, :]
        ms = jnp.mean(x * x, axis=-1, keepdims=True)
        h_ref[rows, :] = ((x * lax.rsqrt(ms + EPS)) * amp + shift).astype(BF16)
        if zero_ref is not None:
            zero_ref[rows, :] = jnp.zeros((MOD_ROWS, zero_ref.shape[1]), zero_ref.dtype)
        return carry

    lax.fori_loop(0, tm // MOD_ROWS, body, 0, unroll=MOD_UNROLL)


def _ffn_kernel(x_ref, gain_ref, sh_ref, sc_ref, gt_ref, wg_ref, wu_ref, wo_ref, o_ref, h_ref, *, nf):
    f = pl.program_id(1)

    @pl.when(f == 0)
    def _():
        _modulate_into(h_ref, x_ref, gain_ref, sh_ref, sc_ref, zero_ref=o_ref)

    h = h_ref[...]
    g = _dot(h, wg_ref[...])
    u = _dot(h, wu_ref[...])
    a = (jax.nn.silu(g) * u).astype(BF16)
    o_ref[...] += _dot(a, wo_ref[...])

    @pl.when(f == nf - 1)
    def _():
        o_ref[...] = x_ref[...] + (0.5 * gt_ref[...]) * o_ref[...]


def _row_spec(groups, tm, d):
    return pl.BlockSpec((None, 1, d), lambda i, j: (_seq_info(i * tm, groups)[0], 0, 0))


def _ffn(x, gain, shift, scale, gate, w_in, w_out, layer, groups):
    t, d = x.shape
    ff = w_out.shape[1]
    tm = min(TM_FFN, groups[0][1])
    tf = min(TF_FFN, ff)
    nf = ff // tf
    row = _row_spec(groups, tm, d)
    return pl.pallas_call(
        functools.partial(_ffn_kernel, nf=nf),
        out_shape=jax.ShapeDtypeStruct((t, d), F32),
        grid=(t // tm, nf),
        in_specs=[
            pl.BlockSpec((tm, d), lambda i, f: (i, 0)),
            pl.BlockSpec((1, d), lambda i, f: (0, 0)),
            row, row, row,
            pl.BlockSpec((None, d, tf), lambda i, f: (layer, 0, f)),
            pl.BlockSpec((None, d, tf), lambda i, f: (layer, 0, nf + f)),
            pl.BlockSpec((None, tf, d), lambda i, f: (layer, f, 0)),
        ],
        out_specs=pl.BlockSpec((tm, d), lambda i, f: (i, 0)),
        scratch_shapes=[pltpu.VMEM((tm, d), BF16)],
        compiler_params=_params("parallel", "arbitrary"),
        name="ffn",
    )(x, gain, shift, scale, gate, w_in, w_in, w_out)


def _proj_kernel(*refs, odd):
    if odd:
        x_ref, gain_ref, sh_ref, sc_ref, w_ref, wn_ref, cos_ref, sin_ref, o_ref, on_ref, h_ref = refs
    else:
        x_ref, gain_ref, sh_ref, sc_ref, w_ref, o_ref, h_ref = refs
    j = pl.program_id(1)

    @pl.when(j == 0)
    def _():
        _modulate_into(h_ref, x_ref, gain_ref, sh_ref, sc_ref)
        if odd:
            on_ref[...] = _dot(h_ref[...], wn_ref[...])

    y = _dot(h_ref[...], w_ref[...])
    if not odd:
        o_ref[...] = y.astype(BF16)
        return

    tn = o_ref.shape[1]
    rope_tiles = 2 * D_HALF // tn
    half = RET_D // 2
    rotate = j < rope_tiles
    cos = jnp.where(rotate, cos_ref[...], 1.0)
    sin = jnp.where(rotate, sin_ref[...], 0.0)
    k_scale = jnp.where(rotate & (j >= rope_tiles // 2), RET_D ** -0.5, 1.0).astype(F32)
    for hd in range(tn // RET_D):
        x1 = y[:, hd * RET_D:hd * RET_D + half]
        x2 = y[:, hd * RET_D + half:(hd + 1) * RET_D]
        o_ref[:, hd * RET_D:hd * RET_D + half] = ((x1 * cos - x2 * sin) * k_scale).astype(BF16)
        o_ref[:, hd * RET_D + half:(hd + 1) * RET_D] = ((x1 * sin + x2 * cos) * k_scale).astype(BF16)


def _proj(x, gain, shift, scale, w, idx, n, groups, odd_extras=None):
    t, d = x.shape
    tm = min(TM_PROJ, groups[0][1])
    tn = TN_PROJ
    row = _row_spec(groups, tm, d)
    odd = odd_extras is not None
    in_specs = [
        pl.BlockSpec((tm, d), lambda i, j: (i, 0)),
        pl.BlockSpec((1, d), lambda i, j: (0, 0)),
        row, row,
        pl.BlockSpec((None, d, tn), lambda i, j: (idx, 0, j)),
    ]
    out_shape = jax.ShapeDtypeStruct((t, n), BF16)
    out_specs = pl.BlockSpec((tm, tn), lambda i, j: (i, j))
    args = (x, gain, shift, scale, w)
    if odd:
        half = RET_D // 2

        def pos_block(i, j):
            row0 = i * tm
            return ((row0 - _seq_info(row0, groups)[1]) // tm, 0)

        tab = pl.BlockSpec((tm, half), pos_block)
        in_specs += [pl.BlockSpec((None, d, LANES), lambda i, j: (idx, 0, 0)), tab, tab]
        out_shape = (out_shape, jax.ShapeDtypeStruct((t, LANES), F32))
        out_specs = (out_specs, pl.BlockSpec((tm, LANES), lambda i, j: (i, 0)))
        args = args + tuple(odd_extras)
    return pl.pallas_call(
        functools.partial(_proj_kernel, odd=odd),
        out_shape=out_shape,
        grid=(t // tm, n // tn),
        in_specs=in_specs,
        out_specs=out_specs,
        scratch_shapes=[pltpu.VMEM((tm, d), BF16)],
        compiler_params=_params("parallel", "arbitrary"),
        name="mix_proj",
    )(*args)


def _outproj_kernel(x_ref, ya_ref, yb_ref, gt_ref, wa_ref, wb_ref, o_ref):
    y = _dot(ya_ref[...], wa_ref[...]) + _dot(yb_ref[...], wb_ref[...])
    o_ref[...] = x_ref[...] + gt_ref[...] * y


def _outproj(x, ya, yb, ca, cb, gate, w, idx, groups):
    t, d = x.shape
    dh = d // 2
    tm = min(TM_OUT, groups[0][1])
    row = _row_spec(groups, tm, d)
    return pl.pallas_call(
        _outproj_kernel,
        out_shape=jax.ShapeDtypeStruct((t, d), F32),
        grid=(t // tm, 1),
        in_specs=[
            pl.BlockSpec((tm, d), lambda i, j: (i, 0)),
            pl.BlockSpec((tm, dh), lambda i, j: (i, ca)),
            pl.BlockSpec((tm, dh), lambda i, j: (i, cb)),
            row,
            pl.BlockSpec((None, dh, d), lambda i, j: (idx, 0, 0)),
            pl.BlockSpec((None, dh, d), lambda i, j: (idx, 1, 0)),
        ],
        out_specs=pl.BlockSpec((tm, d), lambda i, j: (i, 0)),
        compiler_params=_params("parallel", "arbitrary"),
        name="mix_out",
    )(x, ya, yb, gate, w, w)


def _final_kernel(x_ref, gain_ref, sh_ref, sc_ref, o0_ref, o1_ref, *, n0):
    y = _modulated(x_ref[...], gain_ref[...], sh_ref[...], sc_ref[...])
    i = pl.program_id(0)

    @pl.when(i < n0)
    def _():
        o0_ref[...] = y

    @pl.when(i >= n0)
    def _():
        o1_ref[...] = y


def _final(x, gain, shift, scale, groups):
    t, d = x.shape
    tm = min(TM_FINAL, groups[0][1])
    (b0, l0), (b1, l1) = groups
    n0 = b0 * l0 // tm
    row = _row_spec(groups, tm, d)
    return pl.pallas_call(
        functools.partial(_final_kernel, n0=n0),
        out_shape=(jax.ShapeDtypeStruct((b0 * l0, d), F32), jax.ShapeDtypeStruct((b1 * l1, d), F32)),
        grid=(t // tm, 1),
        in_specs=[
            pl.BlockSpec((tm, d), lambda i, j: (i, 0)),
            pl.BlockSpec((1, d), lambda i, j: (0, 0)),
            row, row,
        ],
        out_specs=(pl.BlockSpec((tm, d), lambda i, j: (jnp.minimum(i, n0 - 1), 0)),
                   pl.BlockSpec((tm, d), lambda i, j: (jnp.maximum(i - n0, 0), 0))),
        compiler_params=_params("arbitrary", "arbitrary"),
        name="final_mod",
    )(x, gain, shift, scale)


def _halo_specs(tl, width, nrows, col_of):
    per = tl // HALO
    last = nrows // HALO - 1
    prev = pl.BlockSpec((HALO, width), lambda i, *r: (jnp.maximum(i * per - 1, 0), col_of(i, *r)))
    nxt = pl.BlockSpec((HALO, width), lambda i, *r: (jnp.minimum((i + 1) * per, last), col_of(i, *r)))
    return prev, nxt


def _fill_ext(ext_ref, x_ref, prev_ref, next_ref, first, last, tl):
    dt = ext_ref.dtype
    ext_ref[HALO:HALO + tl, :] = x_ref[...].astype(dt)
    ext_ref[0:HALO, :] = jnp.where(first, jnp.zeros_like(prev_ref), prev_ref[...]).astype(dt)
    ext_ref[HALO + tl:2 * HALO + tl, :] = jnp.where(last, jnp.zeros_like(next_ref), next_ref[...]).astype(dt)


def _row_window(tl, lo, hi):
    r = lax.broadcasted_iota(jnp.int32, (tl, tl + 2 * HALO), 0) + HALO
    col = lax.broadcasted_iota(jnp.int32, (tl, tl + 2 * HALO), 1)
    return ((col >= r + lo) & (col < r + hi)).astype(BF16)


def _even_kernel(xa_ref, prev_ref, next_ref, u_ref, v_ref, pw_ref, ps_ref, ng_ref, sw_ref, sb_ref,
                 o_ref, ext_ref, vn_ref, win_ref, *, tl, groups):
    @pl.when(pl.program_id(0) == 0)
    def _():
        for gi, w in enumerate(POOL_WINDOWS):
            win_ref[gi] = _row_window(tl, -(w // 2), w - w // 2)

    row0 = pl.program_id(0) * tl
    _, sstart, slen = _seq_info(row0, groups)
    pos0 = row0 - sstart
    _fill_ext(ext_ref, xa_ref, prev_ref, next_ref, pos0 == 0, pos0 + tl == slen, tl)

    t = pos0 + lax.broadcasted_iota(jnp.int32, (tl, 1), 0)
    for gi, w in enumerate(POOL_WINDOWS):
        c0 = gi * POOL_GROUP
        cols = slice(c0, c0 + POOL_GROUP)
        s = _dot(win_ref[gi], ext_ref[:, cols])
        lo = jnp.clip(t - w // 2, 0, slen)
        hi = jnp.clip(t + (w - w // 2), 0, slen)
        cnt = (hi - lo).astype(F32)
        pooled = (s / cnt - xa_ref[:, cols].astype(F32)).astype(BF16)
        ya = _dot(pooled, pw_ref[gi]) * ps_ref[:, cols]
        o_ref[:, cols] = ya.astype(BF16)

    v = jax.nn.gelu(v_ref[...].astype(F32))
    vms = jnp.mean(v * v, axis=-1, keepdims=True)
    vn_ref[...] = ((v * lax.rsqrt(vms + EPS)) * ng_ref[...]).astype(BF16)
    for n in range(tl // SGU_CHUNK):
        rows = slice(n * SGU_CHUNK, (n + 1) * SGU_CHUNK)
        for g in range(SGU_GROUPS):
            cols = slice(g * SGU_HEAD, (g + 1) * SGU_HEAD)
            mixed = _dot(sw_ref[g], vn_ref[rows, cols]) + sb_ref[g]
            u = jax.nn.gelu(u_ref[rows, cols].astype(F32))
            o_ref[rows, D_HALF + g * SGU_HEAD:D_HALF + (g + 1) * SGU_HEAD] = (u * mixed).astype(BF16)


def _even_mix(proj, pool_w, pool_scale, sgu_norm, sgu_w, sgu_b, groups):
    t = proj.shape[0]
    tl = min(TL_EVEN, groups[0][1])
    prev, nxt = _halo_specs(tl, D_HALF, t, lambda i: 0)
    const2 = lambda i: (0, 0)
    const3 = lambda i: (0, 0, 0)
    return pl.pallas_call(
        functools.partial(_even_kernel, tl=tl, groups=groups),
        out_shape=jax.ShapeDtypeStruct((t, 2 * D_HALF), BF16),
        grid=(t // tl,),
        in_specs=[
            pl.BlockSpec((tl, D_HALF), lambda i: (i, 0)),
            prev, nxt,
            pl.BlockSpec((tl, D_HALF), lambda i: (i, 1)),
            pl.BlockSpec((tl, D_HALF), lambda i: (i, 2)),
            pl.BlockSpec(pool_w.shape, const3),
            pl.BlockSpec((1, D_HALF), const2),
            pl.BlockSpec((1, D_HALF), const2),
            pl.BlockSpec(sgu_w.shape, const3),
            pl.BlockSpec(sgu_b.shape, const3),
        ],
        out_specs=pl.BlockSpec((tl, 2 * D_HALF), lambda i: (i, 0)),
        scratch_shapes=[pltpu.VMEM((tl + 2 * HALO, D_HALF), BF16), pltpu.VMEM((tl, D_HALF), BF16),
                        pltpu.VMEM((len(POOL_WINDOWS), tl, tl + 2 * HALO), BF16)],
        compiler_params=_params("arbitrary"),
        name="even_mix",
    )(proj, proj, proj, proj, proj, pool_w, pool_scale, sgu_norm, sgu_w, sgu_b)


def _rope_kernel(inv_ref, cos_ref, sin_ref, *, tl):
    pos = (pl.program_id(0) * tl + lax.broadcasted_iota(jnp.int32, (tl, 1), 0)).astype(F32)
    ang = pos * inv_ref[...]
    cos_ref[...] = jnp.cos(ang)
    sin_ref[...] = jnp.sin(ang)


def _rope_tables(max_len, tl):
    half = RET_D // 2
    inv = (1.0 / (ROPE_BASE ** jnp.linspace(0.0, 1.0, half, dtype=F32))).reshape(1, half)
    shp = jax.ShapeDtypeStruct((max_len, half), F32)
    return pl.pallas_call(
        functools.partial(_rope_kernel, tl=tl),
        out_shape=(shp, shp),
        grid=(max_len // tl,),
        in_specs=[pl.BlockSpec((1, half), lambda i: (0, 0))],
        out_specs=(pl.BlockSpec((tl, half), lambda i: (i, 0)), pl.BlockSpec((tl, half), lambda i: (i, 0))),
        compiler_params=_params("parallel"),
        name="rope_table",
    )(inv)


def _ret_kernel(*refs, tl, nt, groups, reverse):
    if reverse:
        q_ref, k_ref, v_ref, dec_ref, o_ref, s_ref = refs
    else:
        q_ref, k_ref, v_ref, dec_ref, decb_ref, g_ref, ob_ref, ng_ref, o_ref, s_ref = refs
    j = pl.program_id(0)
    it = nt - 1 - j if reverse else j
    row0 = it * tl
    _, sstart, slen = _seq_info(row0, groups)
    pos0 = row0 - sstart
    reset = (pos0 + tl == slen) if reverse else (pos0 == 0)

    @pl.when(reset)
    def _():
        s_ref[...] = jnp.zeros_like(s_ref)

    c = RET_CHUNK
    heads = range(RET_HEADS)
    idx = lax.broadcasted_iota(jnp.int32, (c, 1), 0).astype(F32)
    lg = [jnp.log1p(-jnp.exp2(-dec_ref[h])) for h in heads]
    if reverse:
        q_dec = [jnp.exp(x * (c - idx)) for x in lg]
        k_dec = [jnp.exp(x * idx) for x in lg]
    else:
        q_dec = [jnp.exp(x * (idx + 1.0)) for x in lg]
        k_dec = [jnp.exp(x * (c - 1.0 - idx)) for x in lg]
        lgb = [jnp.log1p(-jnp.exp2(-decb_ref[h])) for h in heads]
        ri = lax.broadcasted_iota(jnp.int32, (c, c), 0)
        ci = lax.broadcasted_iota(jnp.int32, (c, c), 1)
        rel = (ri - ci).astype(F32)
        dmat = [jnp.where(rel >= 0, jnp.exp(x * jnp.maximum(rel, 0.0)), 0.0)
                + jnp.where(rel <= 0, jnp.exp(y * jnp.maximum(-rel, 0.0)), 0.0) for x, y in zip(lg, lgb)]
    chunk_dec = [jnp.exp(x * float(c)) for x in lg]

    nc = tl // c
    order = list(range(nc - 1, -1, -1) if reverse else range(nc))
    units = [(ch, h) for ch in order for h in heads]

    def rows(ch):
        return slice(ch * c, (ch + 1) * c)

    def cols(h):
        return slice(h * RET_D, (h + 1) * RET_D)

    q = [q_ref[rows(ch), cols(h)] for ch, h in units]
    v = [v_ref[rows(ch), cols(h)] for ch, h in units]
    kv = [_dot_tn(k_ref[rows(ch), cols(h)], (x.astype(F32) * k_dec[h]).astype(BF16))
          for (ch, h), x in zip(units, v)]
    if not reverse:
        scores = [(_dot_nt(x, k_ref[rows(ch), cols(h)]) * dmat[h]).astype(BF16) for (ch, h), x in zip(units, q)]
        intra = [_dot(x, y) for x, y in zip(scores, v)]

    for ci_, ch in enumerate(order):
        s = [s_ref[h] for h in heads]
        inter = [_dot(q[ci_ * RET_HEADS + h], s[h].astype(BF16)) * q_dec[h] for h in heads]
        for h in heads:
            s_ref[h] = s[h] * chunk_dec[h] + kv[ci_ * RET_HEADS + h]
        for h in heads:
            if reverse:
                o_ref[rows(ch), cols(h)] = inter[h]
            else:
                o = intra[ci_ * RET_HEADS + h] + inter[h] + ob_ref[rows(ch), cols(h)]
                mu = jnp.mean(o, axis=-1, keepdims=True)
                var = jnp.mean(jnp.square(o - mu), axis=-1, keepdims=True)
                on = ((o - mu) * lax.rsqrt(var + EPS)) * ng_ref[:, cols(h)]
                gate = jax.nn.silu(g_ref[rows(ch), cols(h)].astype(F32))
                o_ref[rows(ch), cols(h)] = (gate * on).astype(BF16)


def _retention(proj, decay_f, decay_b, norm_g, groups):
    t = proj.shape[0]
    tl = min(TL_RET, groups[0][1])
    nt = t // tl
    dec = pl.BlockSpec((RET_HEADS, 1, 1), lambda j: (0, 0, 0))
    state = pltpu.VMEM((RET_HEADS, RET_D, RET_D), F32)

    def section(reverse):
        tile = (lambda j: nt - 1 - j) if reverse else (lambda j: j)
        return lambda s: pl.BlockSpec((tl, D_HALF), lambda j: (tile(j), s))

    sec = section(True)
    ob = pl.pallas_call(
        functools.partial(_ret_kernel, tl=tl, nt=nt, groups=groups, reverse=True),
        out_shape=jax.ShapeDtypeStruct((t, D_HALF), F32),
        grid=(nt,),
        in_specs=[sec(0), sec(1), sec(2), dec],
        out_specs=sec(0),
        scratch_shapes=[state],
        compiler_params=_params("arbitrary"),
        name="ret_bwd",
    )(proj, proj, proj, decay_b.reshape(RET_HEADS, 1, 1))

    sec = section(False)
    return pl.pallas_call(
        functools.partial(_ret_kernel, tl=tl, nt=nt, groups=groups, reverse=False),
        out_shape=jax.ShapeDtypeStruct((t, D_HALF), BF16),
        grid=(nt,),
        in_specs=[sec(0), sec(1), sec(2), dec, dec, sec(3), sec(0),
                  pl.BlockSpec((1, D_HALF), lambda j: (0, 0))],
        out_specs=sec(0),
        scratch_shapes=[state],
        compiler_params=_params("arbitrary"),
        name="ret_fwd",
    )(proj, proj, proj, decay_f.reshape(RET_HEADS, 1, 1), decay_b.reshape(RET_HEADS, 1, 1),
      proj, ob, norm_g)


def _dnprep_kernel(x_ref, prev_ref, next_ref, w_ref, o_ref, ext_ref, *, tl, groups):
    row0 = pl.program_id(0) * tl
    part = pl.program_id(1)
    _, sstart, slen = _seq_info(row0, groups)
    pos0 = row0 - sstart
    _fill_ext(ext_ref, x_ref, prev_ref, next_ref, pos0 == 0, pos0 + tl == slen, tl)
    left = DN_CONV // 2
    q_scale = jnp.where(part == 0, DN_D ** -0.5, 1.0).astype(F32)
    rb = PREP_ROWS
    taps = [tap for tap in range(DN_CONV) if tap != left]
    r = lax.broadcasted_iota(jnp.int32, (rb, rb + 2 * HALO), 0) + HALO
    col = lax.broadcasted_iota(jnp.int32, (rb, rb + 2 * HALO), 1)
    select = jnp.concatenate([(col == r + (tap - left)).astype(BF16) for tap in taps], axis=0)
    for blk in range(tl // rb):
        rows = slice(blk * rb, (blk + 1) * rb)
        for h in range(DN_HEADS):
            cols = slice(h * DN_D, (h + 1) * DN_D)
            shifted = _dot(select, ext_ref[blk * rb:(blk + 1) * rb + 2 * HALO, cols])
            conv = x_ref[rows, cols].astype(F32) * w_ref[left:left + 1, cols]
            for n, tap in enumerate(taps):
                conv = conv + shifted[n * rb:(n + 1) * rb] * w_ref[tap:tap + 1, cols]
            y = jax.nn.silu(conv)
            inv_norm = lax.rsqrt(jnp.sum(y * y, axis=-1, keepdims=True) + EPS)
            o_ref[rows, cols] = (y * jnp.where(part < 2, inv_norm * q_scale, 1.0)).astype(BF16)


def _dn_prep(proj, conv_w, groups):
    t = proj.shape[0]
    tl = min(TL_PREP, groups[0][1])
    base = P_RET // D_HALF
    prev, nxt = _halo_specs(tl, D_HALF, t, lambda i, part: base + part)
    return pl.pallas_call(
        functools.partial(_dnprep_kernel, tl=tl, groups=groups),
        out_shape=jax.ShapeDtypeStruct((t, 3 * D_HALF), BF16),
        grid=(t // tl, 3),
        in_specs=[pl.BlockSpec((tl, D_HALF), lambda i, part: (i, base + part)), prev, nxt,
                  pl.BlockSpec((DN_CONV, D_HALF), lambda i, part: (0, part))],
        out_specs=pl.BlockSpec((tl, D_HALF), lambda i, part: (i, part)),
        scratch_shapes=[pltpu.VMEM((tl + 2 * HALO, D_HALF), BF16)],
        compiler_params=_params("parallel", "parallel"),
        name="dn_prep",
    )(proj, proj, proj, conv_w)


def _dot_hi_each(lhs_parts, rhs_parts):
    m = lhs_parts[0][0].shape[0]
    n = rhs_parts[0][0].shape[1]
    quads = [_dot(jnp.concatenate(a, axis=0), jnp.concatenate(b, axis=1)) for a, b in zip(lhs_parts, rhs_parts)]
    return [(x[:m, :n] + x[m:, :n]) + (x[:m, n:] + x[m:, n:]) for x in quads]


def _block_diag(y, left):
    zero = jnp.zeros_like(y)
    return jnp.concatenate([jnp.where(left, y, zero), jnp.where(left, zero, y)], axis=0)


def _pair_products(lhs, rhs, left):
    ls = [_split_bf16(x) for x in lhs]
    rs = [tuple(_block_diag(part, left) for part in _split_bf16(y)) for y in rhs]
    return _dot_hi_each(ls, rs)


def _unit_triangular_inverses(mats, eye, ri, ci, left):
    size = SUBLANES
    same = (ri // size) == (ci // size)
    ps = [jnp.where(same, a, 0.0) for a in mats]
    invs = [eye - d for d in ps]
    n = 2
    while n < size:
        ps = _pair_products(ps, ps, left)
        invs = [inv + x for inv, x in zip(invs, _pair_products(invs, ps, left))]
        n *= 2
    while size < DN_CHUNK:
        size *= 2
        merged = (ri // size) == (ci // size)
        es = [jnp.where(merged & ~same, a, 0.0) for a in mats]
        invs = [inv - x for inv, x in zip(invs, _pair_products(_pair_products(invs, es, left), invs, left))]
        same = merged
    return invs


def _dn_kernel(*refs, tl, nt, groups, reverse):
    if reverse:
        q_ref, k_ref, v_ref, gates_ref, alog_ref, dtb_ref, o_ref, s_ref = refs
    else:
        (q_ref, k_ref, v_ref, gates_ref, alog_ref, dtb_ref, z_ref, ob_ref, ng_ref,
         o_ref, s_ref) = refs
    j = pl.program_id(0)
    it = nt - 1 - j if reverse else j
    row0 = it * tl
    _, sstart, slen = _seq_info(row0, groups)
    pos0 = row0 - sstart
    reset = (pos0 + tl == slen) if reverse else (pos0 == 0)

    @pl.when(reset)
    def _():
        s_ref[...] = jnp.zeros_like(s_ref)

    gates = gates_ref[...]
    beta_all = jax.nn.sigmoid(gates)
    la_all = -jnp.exp(alog_ref[...]) * jax.nn.softplus(gates + dtb_ref[...])
    cb0 = DN_HEADS if reverse else 0
    ca0 = cb0 + 2 * DN_HEADS

    c = DN_CHUNK
    left = lax.broadcasted_iota(jnp.int32, (1, 2 * c), 1) < c
    ri = lax.broadcasted_iota(jnp.int32, (c, 2 * c), 0)
    ci = jnp.bitwise_and(lax.broadcasted_iota(jnp.int32, (c, 2 * c), 1), c - 1)
    eye = (ri == ci).astype(F32)
    incl = (ri <= ci) if reverse else (ri >= ci)
    strict = (ri < ci) if reverse else (ri > ci)
    incl_t = (ri >= ci) if reverse else (ri <= ci)
    last = 0 if reverse else c - 1

    nc = tl // c
    order = list(range(nc - 1, -1, -1) if reverse else range(nc))
    heads = range(DN_HEADS)
    half_heads = range(DN_HEADS // 2)
    units = [(ch, h) for ch in order for h in heads]
    pairs = range(len(units) // 2)

    def rows(ch):
        return slice(ch * c, (ch + 1) * c)

    def cols(h):
        return slice(h * DN_D, (h + 1) * DN_D)

    def paired(xs):
        return [jnp.where(left, xs[2 * p], xs[2 * p + 1]) for p in pairs]

    q = [q_ref[rows(ch), cols(h)] for ch, h in units]
    k = [k_ref[rows(ch), cols(h)] for ch, h in units]
    beta = [beta_all[rows(ch), cb0 + h:cb0 + h + 1] for ch, h in units]
    la = [la_all[rows(ch), ca0 + h:ca0 + h + 1] for ch, h in units]
    la2 = paired(la)
    la_row = [jnp.sum(eye * x, axis=0, keepdims=True) for x in la2]
    g_row = [jnp.sum(jnp.where(incl_t, x, 0.0), axis=0, keepdims=True) for x in la2]
    windowed = [jnp.where(incl, x, 0.0) for x in la_row]
    g_col = [jnp.sum(jnp.where(left == (u % 2 == 0), windowed[u // 2], 0.0), axis=1, keepdims=True)
             for u in range(len(units))]
    gam = [jnp.where(incl, jnp.exp(jnp.where(incl, gc - gr, 0.0)), 0.0) for gc, gr in zip(paired(g_col), g_row)]
    eg = [jnp.exp(gc) for gc in g_col]
    g_last = [gc[last:last + 1, :] for gc in g_col]
    kb = [x.astype(F32) * b for x, b in zip(k, beta)]
    kq = [_dot_nt(jnp.concatenate([kb[2 * p].astype(BF16), q[2 * p], kb[2 * p + 1].astype(BF16), q[2 * p + 1]],
                                  axis=0),
                  jnp.concatenate([k[2 * p], k[2 * p + 1]], axis=0)) for p in pairs]
    a = [jnp.where(strict, jnp.where(left, x[:c], x[2 * c:3 * c]) * gm, 0.0) for x, gm in zip(kq, gam)]
    attn = [_block_diag((jnp.where(left, x[c:2 * c], x[3 * c:]) * gm).astype(BF16), left)
            for x, gm in zip(kq, gam)]
    tinv = _unit_triangular_inverses(a, eye, ri, ci, left)
    rhs = [jnp.concatenate([v_ref[rows(ch), cols(h)].astype(F32) * b, x * e], axis=1)
           for (ch, h), b, x, e in zip(units, beta, kb, eg)]
    uw2 = _dot_hi_each([tuple(_block_diag(part, left) for part in _split_bf16(t)) for t in tinv],
                       [_split_bf16(jnp.concatenate([rhs[2 * p], rhs[2 * p + 1]], axis=0)) for p in pairs])
    uw = [uw2[u // 2][(u % 2) * c:(u % 2 + 1) * c] for u in range(len(units))]
    tail = [jnp.exp(gl - gc) for gl, gc in zip(g_last, g_col)]
    dec = [jnp.exp(gl) for gl in g_last]

    for ci_, ch in enumerate(order):
        idx = [ci_ * DN_HEADS + h for h in heads]
        s = [s_ref[h] for h in heads]
        sb = [x.astype(BF16) for x in s]
        ws = [_dot(jnp.concatenate([uw[i][:, DN_D:].astype(BF16), q[i]], axis=0), sb[h])
              for h, i in zip(heads, idx)]
        v_new = [uw[i][:, :DN_D] - x[:c] for i, x in zip(idx, ws)]
        av = [_dot(attn[idx[2 * x] // 2],
                   jnp.concatenate([v_new[2 * x].astype(BF16), v_new[2 * x + 1].astype(BF16)], axis=0))
              for x in half_heads]
        o = [ws[h][c:] * eg[idx[h]] + av[h // 2][(h % 2) * c:(h % 2 + 1) * c] for h in heads]
        for h, i in zip(heads, idx):
            s_ref[h] = s[h] * dec[i] + _dot_tn(k[i], (v_new[h] * tail[i]).astype(BF16))
        for h in heads:
            if reverse:
                o_ref[rows(ch), cols(h)] = o[h]
            else:
                oo = o[h] + ob_ref[rows(ch), cols(h)]
                on = (oo * lax.rsqrt(jnp.mean(oo * oo, axis=-1, keepdims=True) + EPS)) * ng_ref[...]
                z = z_ref[rows(ch), cols(h)].astype(F32)
                o_ref[rows(ch), cols(h)] = (on * jax.nn.silu(z)).astype(BF16)


def _deltanet(proj, gate_cols, qkv, alog_row, dtb_row, norm_g, groups):
    t = proj.shape[0]
    tl = min(TL_DN, groups[0][1])
    nt = t // tl
    z_blk = (P_RET + 3 * D_HALF) // D_HALF
    row = pl.BlockSpec((1, LANES), lambda j: (0, 0))
    state = pltpu.VMEM((DN_HEADS, DN_D, DN_D), F32)

    def common(reverse):
        tile = (lambda j: nt - 1 - j) if reverse else (lambda j: j)
        sec = lambda s: pl.BlockSpec((tl, D_HALF), lambda j: (tile(j), s))
        gates = pl.BlockSpec((tl, LANES), lambda j: (tile(j), 0))
        return sec, gates

    sec, gates = common(True)
    ob = pl.pallas_call(
        functools.partial(_dn_kernel, tl=tl, nt=nt, groups=groups, reverse=True),
        out_shape=jax.ShapeDtypeStruct((t, D_HALF), F32),
        grid=(nt,),
        in_specs=[sec(0), sec(1), sec(2), gates, row, row],
        out_specs=sec(0),
        scratch_shapes=[state],
        compiler_params=_params("arbitrary"),
        name="dn_bwd",
    )(qkv, qkv, qkv, gate_cols, alog_row, dtb_row)

    sec, gates = common(False)
    return pl.pallas_call(
        functools.partial(_dn_kernel, tl=tl, nt=nt, groups=groups, reverse=False),
        out_shape=jax.ShapeDtypeStruct((t, D_HALF), BF16),
        grid=(nt,),
        in_specs=[sec(0), sec(1), sec(2), gates, row, row, sec(z_blk), sec(0), row],
        out_specs=sec(0),
        scratch_shapes=[state],
        compiler_params=_params("arbitrary"),
        name="dn_fwd",
    )(qkv, qkv, qkv, gate_cols, alog_row, dtb_row, proj, ob, norm_g)


def _gate_row(f_vals, b_vals):
    row = jnp.zeros((LANES,), F32)
    row = row.at[2 * DN_HEADS:3 * DN_HEADS].set(f_vals.astype(F32))
    row = row.at[3 * DN_HEADS:4 * DN_HEADS].set(b_vals.astype(F32))
    return row.reshape(1, LANES)


def _trunk(x, c, p, groups):
    t, d = x.shape
    n_seq = c.shape[0]
    c_pad = jnp.zeros((SEQ_PAD, d), F32).at[:n_seq].set(c)
    mods = _ada(c_pad, p['w_ada'], p['b_ada'])
    mods = mods.reshape(DEPTH, SEQ_PAD, N_MOD, 1, d).transpose(0, 2, 1, 3, 4)
    fin = _ada(c_pad, p['w_ada_final'][None], p['b_ada_final'][None])
    fin = fin.reshape(SEQ_PAD, 2, 1, d).transpose(1, 0, 2, 3)

    max_len = max(ln for _, ln in groups)
    cos, sin = _rope_tables(max_len, min(TL_RET, groups[0][1]))

    row = lambda a: a.reshape(1, -1)
    for layer in range(DEPTH):
        sh1, sc1, g1, sh2, sc2, g2, sh3, sc3, g3 = [mods[layer, jm] for jm in range(N_MOD)]
        x = _ffn(x, row(p['norm_ffn1'][layer]), sh1, sc1, g1, p['w_ffn1_in'], p['w_ffn1_out'], layer, groups)
        idx = layer // 2
        gain = row(p['norm_mix'][layer])
        if layer % 2 == 0:
            proj = _proj(x, gain, sh2, sc2, p['w_in_even'], idx, P_EVEN, groups)
            y = _even_mix(proj, p['pool_w'][idx], row(p['pool_scale'][idx]), row(p['sgu_norm'][idx]),
                          p['sgu_w'][idx], p['sgu_b'][idx][..., None], groups)
            x = _outproj(x, y, y, 0, 1, g2, p['w_out_even'], idx, groups)
        else:
            proj, gate_cols = _proj(x, gain, sh2, sc2, p['w_in_odd'], idx, P_ODD_MAIN, groups,
                                    odd_extras=(p['w_in_odd_gates'], cos, sin))
            yc = _retention(proj, p['ret_decay_f'][idx], p['ret_decay_b'][idx],
                            row(p['ret_norm'][idx]), groups)
            qkv = _dn_prep(proj, p['dn_conv'][idx], groups)
            yd = _deltanet(proj, gate_cols, qkv,
                           _gate_row(p['dn_a_log_f'][idx], p['dn_a_log_b'][idx]),
                           _gate_row(p['dn_dt_bias_f'][idx], p['dn_dt_bias_b'][idx]),
                           row(p['dn_norm'][idx]), groups)
            x = _outproj(x, yc, yd, 0, 0, g2, p['w_out_odd'], idx, groups)
        x = _ffn(x, row(p['norm_ffn2'][layer]), sh3, sc3, g3, p['w_ffn2_in'], p['w_ffn2_out'], layer, groups)
    return _final(x, row(p['norm_final']), fin[0], fin[1], groups)


def _prepare(p):
    q = dict(p)
    for name in ('w_ffn1_in', 'w_ffn1_out', 'w_ffn2_in', 'w_ffn2_out', 'w_in_even', 'w_out_even',
                 'pool_w', 'sgu_w', 'w_in_odd', 'w_out_odd'):
        q[name] = p[name].astype(BF16)
    gates = q['w_in_odd'][:, :, P_ODD_MAIN:]
    q['w_in_odd_gates'] = jnp.pad(gates, ((0, 0), (0, 0), (0, LANES - gates.shape[-1])))
    return q


def kernel(x_prompt, x_sample, c_prompt, c_sample, w_ada, b_ada, norm_ffn1, w_ffn1_in, w_ffn1_out, norm_mix, norm_ffn2, w_ffn2_in, w_ffn2_out, w_in_even, w_out_even, pool_w, pool_scale, sgu_norm, sgu_w, sgu_b, w_in_odd, w_out_odd, ret_decay_f, ret_decay_b, ret_norm, dn_conv, dn_a_log_f, dn_a_log_b, dn_dt_bias_f, dn_dt_bias_b, dn_norm, norm_final, w_ada_final, b_ada_final):
    p = _prepare({
        'w_ada': w_ada, 'b_ada': b_ada, 'norm_ffn1': norm_ffn1, 'w_ffn1_in': w_ffn1_in,
        'w_ffn1_out': w_ffn1_out, 'norm_mix': norm_mix, 'norm_ffn2': norm_ffn2,
        'w_ffn2_in': w_ffn2_in, 'w_ffn2_out': w_ffn2_out, 'w_in_even': w_in_even,
        'w_out_even': w_out_even, 'pool_w': pool_w, 'pool_scale': pool_scale,
        'sgu_norm': sgu_norm, 'sgu_w': sgu_w, 'sgu_b': sgu_b, 'w_in_odd': w_in_odd,
        'w_out_odd': w_out_odd, 'ret_decay_f': ret_decay_f, 'ret_decay_b': ret_decay_b,
        'ret_norm': ret_norm, 'dn_conv': dn_conv, 'dn_a_log_f': dn_a_log_f,
        'dn_a_log_b': dn_a_log_b, 'dn_dt_bias_f': dn_dt_bias_f, 'dn_dt_bias_b': dn_dt_bias_b,
        'dn_norm': dn_norm, 'norm_final': norm_final, 'w_ada_final': w_ada_final,
        'b_ada_final': b_ada_final,
    })
    bp, lp, d = x_prompt.shape
    bs, ls, _ = x_sample.shape
    groups = ((bp, lp), (bs, ls))
    x = jnp.concatenate([x_prompt.reshape(bp * lp, d), x_sample.reshape(bs * ls, d)], axis=0)
    c = jnp.concatenate([c_prompt, c_sample], axis=0)
    y_prompt, y_sample = _trunk(x, c, p, groups)
    return (y_prompt.reshape(bp, lp, d), y_sample.reshape(bs, ls, d))
```

```python
import functools

import jax
import jax.numpy as jnp
from jax import lax
from jax.experimental import pallas as pl
from jax.experimental.pallas import tpu as pltpu

F32 = jnp.float32
BF16 = jnp.bfloat16

D_MODEL = 2048
DEPTH = 4
D_HALF = D_MODEL // 2
POOL_WINDOWS = (2, 4, 8, 16)
POOL_GROUP = D_HALF // len(POOL_WINDOWS)
SGU_CHUNK = 128
SGU_HEAD = 128
SGU_GROUPS = D_HALF // SGU_HEAD
RET_HEADS = 4
RET_D = D_HALF // RET_HEADS
RET_CHUNK = 128
ROPE_BASE = 10000.0
DN_HEADS = 8
DN_D = D_HALF // DN_HEADS
DN_CONV = 4
DN_CHUNK = 64
D_FF = 5632
N_MOD = 9
EPS = 1e-6
P_EVEN = 3 * D_HALF
P_RET = 4 * D_HALF
P_ODD_MAIN = P_RET + 4 * D_HALF
P_ODD = P_ODD_MAIN + 4 * DN_HEADS

LANES = 128
SUBLANES = 8
HALO = 2 * SUBLANES
VMEM_LIMIT = 56 * 1024 * 1024

TM_FFN = 1024
TF_FFN = 512
TM_PROJ = 1024
TN_PROJ = 1024
MOD_ROWS = 16
MOD_UNROLL = 8
TM_OUT = 512
TL_EVEN = 512
TL_RET = 512
TL_DN = 512
TL_PREP = 512
PREP_ROWS = 64
TM_FINAL = 512
TN_ADA = 1024
SEQ_PAD = 16


def _seq_info(row0, groups):
    seq = start = length = None
    t0 = s0 = 0
    for gi, (nb, ln) in enumerate(groups):
        rel = row0 - t0
        q = rel // ln
        if gi == 0:
            seq, start, length = q, q * ln, ln
        else:
            here = row0 >= t0
            seq = jnp.where(here, s0 + q, seq)
            start = jnp.where(here, t0 + q * ln, start)
            length = jnp.where(here, ln, length)
        t0 += nb * ln
        s0 += nb
    return seq, start, length


def _params(*sem):
    return pltpu.CompilerParams(dimension_semantics=sem, vmem_limit_bytes=VMEM_LIMIT)


def _modulated(x, gain, shift, scale):
    ms = jnp.mean(x * x, axis=-1, keepdims=True)
    y = x * lax.rsqrt(ms + EPS)
    return (y * gain) * (1.0 + scale) + shift


def _dot(a, b):
    return jnp.dot(a, b, preferred_element_type=F32)


def _dot_nt(a, b):
    return lax.dot_general(a, b, (((1,), (1,)), ((), ())), preferred_element_type=F32)


def _dot_tn(a, b):
    return lax.dot_general(a, b, (((0,), (0,)), ((), ())), preferred_element_type=F32)


def _split_bf16(a):
    hi = a.astype(BF16)
    lo = (a - hi.astype(F32)).astype(BF16)
    return hi, lo


def _ada_kernel(c_ref, w_ref, b_ref, o_ref):
    c = c_ref[...]
    act = jax.nn.silu(c).astype(BF16)
    o_ref[...] = _dot(act, w_ref[...].astype(BF16)) + b_ref[...]


def _ada(c_pad, w, b):
    ly, d, n = w.shape
    s = c_pad.shape[0]
    tn = min(TN_ADA, n)
    return pl.pallas_call(
        _ada_kernel,
        out_shape=jax.ShapeDtypeStruct((ly, s, n), F32),
        grid=(ly, n // tn),
        in_specs=[
            pl.BlockSpec((s, d), lambda l, j: (0, 0)),
            pl.BlockSpec((None, d, tn), lambda l, j: (l, 0, j)),
            pl.BlockSpec((None, 1, tn), lambda l, j: (l, 0, j)),
        ],
        out_specs=pl.BlockSpec((None, s, tn), lambda l, j: (l, 0, j)),
        compiler_params=_params("parallel", "parallel"),
        name="ada_rows",
    )(c_pad, w, b.reshape(ly, 1, n))


def _modulate_into(h_ref, x_ref, gain_ref, sh_ref, sc_ref, zero_ref=None):
    tm = x_ref.shape[0]
    amp = gain_ref[...] * (1.0 + sc_ref[...])
    shift = sh_ref[...]

    def body(r, carry):
        rows = pl.ds(pl.multiple_of(r * MOD_ROWS, MOD_ROWS), MOD_ROWS)
        x = x_ref[rows, :]
        ms = jnp.mean(x * x, axis=-1, keepdims=True)
        h_ref[rows, :] = ((x * lax.rsqrt(ms + EPS)) * amp + shift).astype(BF16)
        if zero_ref is not None:
            zero_ref[rows, :] = jnp.zeros((MOD_ROWS, zero_ref.shape[1]), zero_ref.dtype)
        return carry

    lax.fori_loop(0, tm // MOD_ROWS, body, 0, unroll=MOD_UNROLL)


def _ffn_kernel(x_ref, gain_ref, sh_ref, sc_ref, gt_ref, wg_ref, wu_ref, wo_ref, o_ref, h_ref, *, nf):
    f = pl.program_id(1)

    @pl.when(f == 0)
    def _():
        _modulate_into(h_ref, x_ref, gain_ref, sh_ref, sc_ref, zero_ref=o_ref)

    h = h_ref[...]
    g = _dot(h, wg_ref[...])
    u = _dot(h, wu_ref[...])
    a = (jax.nn.silu(g) * u).astype(BF16)
    o_ref[...] += _dot(a, wo_ref[...])

    @pl.when(f == nf - 1)
    def _():
        o_ref[...] = x_ref[...] + (0.5 * gt_ref[...]) * o_ref[...]


def _row_spec(groups, tm, d):
    return pl.BlockSpec((None, 1, d), lambda i, j: (_seq_info(i * tm, groups)[0], 0, 0))


def _ffn(x, gain, shift, scale, gate, w_in, w_out, layer, groups):
    t, d = x.shape
    ff = w_out.shape[1]
    tm = min(TM_FFN, groups[0][1])
    tf = min(TF_FFN, ff)
    nf = ff // tf
    row = _row_spec(groups, tm, d)
    return pl.pallas_call(
        functools.partial(_ffn_kernel, nf=nf),
        out_shape=jax.ShapeDtypeStruct((t, d), F32),
        grid=(t // tm, nf),
        in_specs=[
            pl.BlockSpec((tm, d), lambda i, f: (i, 0)),
            pl.BlockSpec((1, d), lambda i, f: (0, 0)),
            row, row, row,
            pl.BlockSpec((None, d, tf), lambda i, f: (layer, 0, f)),
            pl.BlockSpec((None, d, tf), lambda i, f: (layer, 0, nf + f)),
            pl.BlockSpec((None, tf, d), lambda i, f: (layer, f, 0)),
        ],
        out_specs=pl.BlockSpec((tm, d), lambda i, f: (i, 0)),
        scratch_shapes=[pltpu.VMEM((tm, d), BF16)],
        compiler_params=_params("parallel", "arbitrary"),
        name="ffn",
    )(x, gain, shift, scale, gate, w_in, w_in, w_out)


def _proj_kernel(*refs, odd):
    if odd:
        x_ref, gain_ref, sh_ref, sc_ref, w_ref, wn_ref, cos_ref, sin_ref, o_ref, on_ref, h_ref = refs
    else:
        x_ref, gain_ref, sh_ref, sc_ref, w_ref, o_ref, h_ref = refs
    j = pl.program_id(1)

    @pl.when(j == 0)
    def _():
        _modulate_into(h_ref, x_ref, gain_ref, sh_ref, sc_ref)
        if odd:
            on_ref[...] = _dot(h_ref[...], wn_ref[...])

    y = _dot(h_ref[...], w_ref[...])
    if not odd:
        o_ref[...] = y.astype(BF16)
        return

    tn = o_ref.shape[1]
    rope_tiles = 2 * D_HALF // tn
    half = RET_D // 2
    rotate = j < rope_tiles
    cos = jnp.where(rotate, cos_ref[...], 1.0)
    sin = jnp.where(rotate, sin_ref[...], 0.0)
    k_scale = jnp.where(rotate & (j >= rope_tiles // 2), RET_D ** -0.5, 1.0).astype(F32)
    for hd in range(tn // RET_D):
        x1 = y[:, hd * RET_D:hd * RET_D + half]
        x2 = y[:, hd * RET_D + half:(hd + 1) * RET_D]
        o_ref[:, hd * RET_D:hd * RET_D + half] = ((x1 * cos - x2 * sin) * k_scale).astype(BF16)
        o_ref[:, hd * RET_D + half:(hd + 1) * RET_D] = ((x1 * sin + x2 * cos) * k_scale).astype(BF16)


def _proj(x, gain, shift, scale, w, idx, n, groups, odd_extras=None):
    t, d = x.shape
    tm = min(TM_PROJ, groups[0][1])
    tn = TN_PROJ
    row = _row_spec(groups, tm, d)
    odd = odd_extras is not None
    in_specs = [
        pl.BlockSpec((tm, d), lambda i, j: (i, 0)),
        pl.BlockSpec((1, d), lambda i, j: (0, 0)),
        row, row,
        pl.BlockSpec((None, d, tn), lambda i, j: (idx, 0, j)),
    ]
    out_shape = jax.ShapeDtypeStruct((t, n), BF16)
    out_specs = pl.BlockSpec((tm, tn), lambda i, j: (i, j))
    args = (x, gain, shift, scale, w)
    if odd:
        half = RET_D // 2

        def pos_block(i, j):
            row0 = i * tm
            return ((row0 - _seq_info(row0, groups)[1]) // tm, 0)

        tab = pl.BlockSpec((tm, half), pos_block)
        in_specs += [pl.BlockSpec((None, d, LANES), lambda i, j: (idx, 0, 0)), tab, tab]
        out_shape = (out_shape, jax.ShapeDtypeStruct((t, LANES), F32))
        out_specs = (out_specs, pl.BlockSpec((tm, LANES), lambda i, j: (i, 0)))
        args = args + tuple(odd_extras)
    return pl.pallas_call(
        functools.partial(_proj_kernel, odd=odd),
        out_shape=out_shape,
        grid=(t // tm, n // tn),
        in_specs=in_specs,
        out_specs=out_specs,
        scratch_shapes=[pltpu.VMEM((tm, d), BF16)],
        compiler_params=_params("parallel", "arbitrary"),
        name="mix_proj",
    )(*args)


def _outproj_kernel(x_ref, ya_ref, yb_ref, gt_ref, wa_ref, wb_ref, o_ref):
    y = _dot(ya_ref[...], wa_ref[...]) + _dot(yb_ref[...], wb_ref[...])
    o_ref[...] = x_ref[...] + gt_ref[...] * y


def _outproj(x, ya, yb, ca, cb, gate, w, idx, groups):
    t, d = x.shape
    dh = d // 2
    tm = min(TM_OUT, groups[0][1])
    row = _row_spec(groups, tm, d)
    return pl.pallas_call(
        _outproj_kernel,
        out_shape=jax.ShapeDtypeStruct((t, d), F32),
        grid=(t // tm, 1),
        in_specs=[
            pl.BlockSpec((tm, d), lambda i, j: (i, 0)),
            pl.BlockSpec((tm, dh), lambda i, j: (i, ca)),
            pl.BlockSpec((tm, dh), lambda i, j: (i, cb)),
            row,
            pl.BlockSpec((None, dh, d), lambda i, j: (idx, 0, 0)),
            pl.BlockSpec((None, dh, d), lambda i, j: (idx, 1, 0)),
        ],
        out_specs=pl.BlockSpec((tm, d), lambda i, j: (i, 0)),
        compiler_params=_params("parallel", "arbitrary"),
        name="mix_out",
    )(x, ya, yb, gate, w, w)


def _final_kernel(x_ref, gain_ref, sh_ref, sc_ref, o0_ref, o1_ref, *, n0):
    y = _modulated(x_ref[...], gain_ref[...], sh_ref[...], sc_ref[...])
    i = pl.program_id(0)

    @pl.when(i < n0)
    def _():
        o0_ref[...] = y

    @pl.when(i >= n0)
    def _():
        o1_ref[...] = y


def _final(x, gain, shift, scale, groups):
    t, d = x.shape
    tm = min(TM_FINAL, groups[0][1])
    (b0, l0), (b1, l1) = groups
    n0 = b0 * l0 // tm
    row = _row_spec(groups, tm, d)
    return pl.pallas_call(
        functools.partial(_final_kernel, n0=n0),
        out_shape=(jax.ShapeDtypeStruct((b0 * l0, d), F32), jax.ShapeDtypeStruct((b1 * l1, d), F32)),
        grid=(t // tm, 1),
        in_specs=[
            pl.BlockSpec((tm, d), lambda i, j: (i, 0)),
            pl.BlockSpec((1, d), lambda i, j: (0, 0)),
            row, row,
        ],
        out_specs=(pl.BlockSpec((tm, d), lambda i, j: (jnp.minimum(i, n0 - 1), 0)),
                   pl.BlockSpec((tm, d), lambda i, j: (jnp.maximum(i - n0, 0), 0))),
        compiler_params=_params("arbitrary", "arbitrary"),
        name="final_mod",
    )(x, gain, shift, scale)


def _halo_specs(tl, width, nrows, col_of):
    per = tl // HALO
    last = nrows // HALO - 1
    prev = pl.BlockSpec((HALO, width), lambda i, *r: (jnp.maximum(i * per - 1, 0), col_of(i, *r)))
    nxt = pl.BlockSpec((HALO, width), lambda i, *r: (jnp.minimum((i + 1) * per, last), col_of(i, *r)))
    return prev, nxt


def _fill_ext(ext_ref, x_ref, prev_ref, next_ref, first, last, tl):
    dt = ext_ref.dtype
    ext_ref[HALO:HALO + tl, :] = x_ref[...].astype(dt)
    ext_ref[0:HALO, :] = jnp.where(first, jnp.zeros_like(prev_ref), prev_ref[...]).astype(dt)
    ext_ref[HALO + tl:2 * HALO + tl, :] = jnp.where(last, jnp.zeros_like(next_ref), next_ref[...]).astype(dt)


def _row_window(tl, lo, hi):
    r = lax.broadcasted_iota(jnp.int32, (tl, tl + 2 * HALO), 0) + HALO
    col = lax.broadcasted_iota(jnp.int32, (tl, tl + 2 * HALO), 1)
    return ((col >= r + lo) & (col < r + hi)).astype(BF16)


def _even_kernel(xa_ref, prev_ref, next_ref, u_ref, v_ref, pw_ref, ps_ref, ng_ref, sw_ref, sb_ref,
                 o_ref, ext_ref, vn_ref, win_ref, *, tl, groups):
    @pl.when(pl.program_id(0) == 0)
    def _():
        for gi, w in enumerate(POOL_WINDOWS):
            win_ref[gi] = _row_window(tl, -(w // 2), w - w // 2)

    row0 = pl.program_id(0) * tl
    _, sstart, slen = _seq_info(row0, groups)
    pos0 = row0 - sstart
    _fill_ext(ext_ref, xa_ref, prev_ref, next_ref, pos0 == 0, pos0 + tl == slen, tl)

    t = pos0 + lax.broadcasted_iota(jnp.int32, (tl, 1), 0)
    for gi, w in enumerate(POOL_WINDOWS):
        c0 = gi * POOL_GROUP
        cols = slice(c0, c0 + POOL_GROUP)
        s = _dot(win_ref[gi], ext_ref[:, cols])
        lo = jnp.clip(t - w // 2, 0, slen)
        hi = jnp.clip(t + (w - w // 2), 0, slen)
        cnt = (hi - lo).astype(F32)
        pooled = (s / cnt - xa_ref[:, cols].astype(F32)).astype(BF16)
        ya = _dot(pooled, pw_ref[gi]) * ps_ref[:, cols]
        o_ref[:, cols] = ya.astype(BF16)

    v = jax.nn.gelu(v_ref[...].astype(F32))
    vms = jnp.mean(v * v, axis=-1, keepdims=True)
    vn_ref[...] = ((v * lax.rsqrt(vms + EPS)) * ng_ref[...]).astype(BF16)
    for n in range(tl // SGU_CHUNK):
        rows = slice(n * SGU_CHUNK, (n + 1) * SGU_CHUNK)
        for g in range(SGU_GROUPS):
            cols = slice(g * SGU_HEAD, (g + 1) * SGU_HEAD)
            mixed = _dot(sw_ref[g], vn_ref[rows, cols]) + sb_ref[g]
            u = jax.nn.gelu(u_ref[rows, cols].astype(F32))
            o_ref[rows, D_HALF + g * SGU_HEAD:D_HALF + (g + 1) * SGU_HEAD] = (u * mixed).astype(BF16)


def _even_mix(proj, pool_w, pool_scale, sgu_norm, sgu_w, sgu_b, groups):
    t = proj.shape[0]
    tl = min(TL_EVEN, groups[0][1])
    prev, nxt = _halo_specs(tl, D_HALF, t, lambda i: 0)
    const2 = lambda i: (0, 0)
    const3 = lambda i: (0, 0, 0)
    return pl.pallas_call(
        functools.partial(_even_kernel, tl=tl, groups=groups),
        out_shape=jax.ShapeDtypeStruct((t, 2 * D_HALF), BF16),
        grid=(t // tl,),
        in_specs=[
            pl.BlockSpec((tl, D_HALF), lambda i: (i, 0)),
            prev, nxt,
            pl.BlockSpec((tl, D_HALF), lambda i: (i, 1)),
            pl.BlockSpec((tl, D_HALF), lambda i: (i, 2)),
            pl.BlockSpec(pool_w.shape, const3),
            pl.BlockSpec((1, D_HALF), const2),
            pl.BlockSpec((1, D_HALF), const2),
            pl.BlockSpec(sgu_w.shape, const3),
            pl.BlockSpec(sgu_b.shape, const3),
        ],
        out_specs=pl.BlockSpec((tl, 2 * D_HALF), lambda i: (i, 0)),
        scratch_shapes=[pltpu.VMEM((tl + 2 * HALO, D_HALF), BF16), pltpu.VMEM((tl, D_HALF), BF16),
                        pltpu.VMEM((len(POOL_WINDOWS), tl, tl + 2 * HALO), BF16)],
        compiler_params=_params("arbitrary"),
        name="even_mix",
    )(proj, proj, proj, proj, proj, pool_w, pool_scale, sgu_norm, sgu_w, sgu_b)


def _rope_kernel(inv_ref, cos_ref, sin_ref, *, tl):
    pos = (pl.program_id(0) * tl + lax.broadcasted_iota(jnp.int32, (tl, 1), 0)).astype(F32)
    ang = pos * inv_ref[...]
    cos_ref[...] = jnp.cos(ang)
    sin_ref[...] = jnp.sin(ang)


def _rope_tables(max_len, tl):
    half = RET_D // 2
    inv = (1.0 / (ROPE_BASE ** jnp.linspace(0.0, 1.0, half, dtype=F32))).reshape(1, half)
    shp = jax.ShapeDtypeStruct((max_len, half), F32)
    return pl.pallas_call(
        functools.partial(_rope_kernel, tl=tl),
        out_shape=(shp, shp),
        grid=(max_len // tl,),
        in_specs=[pl.BlockSpec((1, half), lambda i: (0, 0))],
        out_specs=(pl.BlockSpec((tl, half), lambda i: (i, 0)), pl.BlockSpec((tl, half), lambda i: (i, 0))),
        compiler_params=_params("parallel"),
        name="rope_table",
    )(inv)


def _ret_kernel(*refs, tl, nt, groups, reverse):
    if reverse:
        q_ref, k_ref, v_ref, dec_ref, o_ref, s_ref = refs
    else:
        q_ref, k_ref, v_ref, dec_ref, decb_ref, g_ref, ob_ref, ng_ref, o_ref, s_ref = refs
    j = pl.program_id(0)
    it = nt - 1 - j if reverse else j
    row0 = it * tl
    _, sstart, slen = _seq_info(row0, groups)
    pos0 = row0 - sstart
    reset = (pos0 + tl == slen) if reverse else (pos0 == 0)

    @pl.when(reset)
    def _():
        s_ref[...] = jnp.zeros_like(s_ref)

    c = RET_CHUNK
    heads = range(RET_HEADS)
    idx = lax.broadcasted_iota(jnp.int32, (c, 1), 0).astype(F32)
    lg = [jnp.log1p(-jnp.exp2(-dec_ref[h])) for h in heads]
    if reverse:
        q_dec = [jnp.exp(x * (c - idx)) for x in lg]
        k_dec = [jnp.exp(x * idx) for x in lg]
    else:
        q_dec = [jnp.exp(x * (idx + 1.0)) for x in lg]
        k_dec = [jnp.exp(x * (c - 1.0 - idx)) for x in lg]
        lgb = [jnp.log1p(-jnp.exp2(-decb_ref[h])) for h in heads]
        ri = lax.broadcasted_iota(jnp.int32, (c, c), 0)
        ci = lax.broadcasted_iota(jnp.int32, (c, c), 1)
        rel = (ri - ci).astype(F32)
        dmat = [jnp.where(rel >= 0, jnp.exp(x * jnp.maximum(rel, 0.0)), 0.0)
                + jnp.where(rel <= 0, jnp.exp(y * jnp.maximum(-rel, 0.0)), 0.0) for x, y in zip(lg, lgb)]
    chunk_dec = [jnp.exp(x * float(c)) for x in lg]

    nc = tl // c
    order = list(range(nc - 1, -1, -1) if reverse else range(nc))
    units = [(ch, h) for ch in order for h in heads]

    def rows(ch):
        return slice(ch * c, (ch + 1) * c)

    def cols(h):
        return slice(h * RET_D, (h + 1) * RET_D)

    q = [q_ref[rows(ch), cols(h)] for ch, h in units]
    v = [v_ref[rows(ch), cols(h)] for ch, h in units]
    kv = [_dot_tn(k_ref[rows(ch), cols(h)], (x.astype(F32) * k_dec[h]).astype(BF16))
          for (ch, h), x in zip(units, v)]
    if not reverse:
        scores = [(_dot_nt(x, k_ref[rows(ch), cols(h)]) * dmat[h]).astype(BF16) for (ch, h), x in zip(units, q)]
        intra = [_dot(x, y) for x, y in zip(scores, v)]

    for ci_, ch in enumerate(order):
        s = [s_ref[h] for h in heads]
        inter = [_dot(q[ci_ * RET_HEADS + h], s[h].astype(BF16)) * q_dec[h] for h in heads]
        for h in heads:
            s_ref[h] = s[h] * chunk_dec[h] + kv[ci_ * RET_HEADS + h]
        for h in heads:
            if reverse:
                o_ref[rows(ch), cols(h)] = inter[h]
            else:
                o = intra[ci_ * RET_HEADS + h] + inter[h] + ob_ref[rows(ch), cols(h)]
                mu = jnp.mean(o, axis=-1, keepdims=True)
                var = jnp.mean(jnp.square(o - mu), axis=-1, keepdims=True)
                on = ((o - mu) * lax.rsqrt(var + EPS)) * ng_ref[:, cols(h)]
                gate = jax.nn.silu(g_ref[rows(ch), cols(h)].astype(F32))
                o_ref[rows(ch), cols(h)] = (gate * on).astype(BF16)


def _retention(proj, decay_f, decay_b, norm_g, groups):
    t = proj.shape[0]
    tl = min(TL_RET, groups[0][1])
    nt = t // tl
    dec = pl.BlockSpec((RET_HEADS, 1, 1), lambda j: (0, 0, 0))
    state = pltpu.VMEM((RET_HEADS, RET_D, RET_D), F32)

    def section(reverse):
        tile = (lambda j: nt - 1 - j) if reverse else (lambda j: j)
        return lambda s: pl.BlockSpec((tl, D_HALF), lambda j: (tile(j), s))

    sec = section(True)
    ob = pl.pallas_call(
        functools.partial(_ret_kernel, tl=tl, nt=nt, groups=groups, reverse=True),
        out_shape=jax.ShapeDtypeStruct((t, D_HALF), F32),
        grid=(nt,),
        in_specs=[sec(0), sec(1), sec(2), dec],
        out_specs=sec(0),
        scratch_shapes=[state],
        compiler_params=_params("arbitrary"),
        name="ret_bwd",
    )(proj, proj, proj, decay_b.reshape(RET_HEADS, 1, 1))

    sec = section(False)
    return pl.pallas_call(
        functools.partial(_ret_kernel, tl=tl, nt=nt, groups=groups, reverse=False),
        out_shape=jax.ShapeDtypeStruct((t, D_HALF), BF16),
        grid=(nt,),
        in_specs=[sec(0), sec(1), sec(2), dec, dec, sec(3), sec(0),
                  pl.BlockSpec((1, D_HALF), lambda j: (0, 0))],
        out_specs=sec(0),
        scratch_shapes=[state],
        compiler_params=_params("arbitrary"),
        name="ret_fwd",
    )(proj, proj, proj, decay_f.reshape(RET_HEADS, 1, 1), decay_b.reshape(RET_HEADS, 1, 1),
      proj, ob, norm_g)


def _dnprep_kernel(x_ref, prev_ref, next_ref, w_ref, o_ref, ext_ref, *, tl, groups):
    row0 = pl.program_id(0) * tl
    part = pl.program_id(1)
    _, sstart, slen = _seq_info(row0, groups)
    pos0 = row0 - sstart
    _fill_ext(ext_ref, x_ref, prev_ref, next_ref, pos0 == 0, pos0 + tl == slen, tl)
    left = DN_CONV // 2
    q_scale = jnp.where(part == 0, DN_D ** -0.5, 1.0).astype(F32)
    rb = PREP_ROWS
    taps = [tap for tap in range(DN_CONV) if tap != left]
    r = lax.broadcasted_iota(jnp.int32, (rb, rb + 2 * HALO), 0) + HALO
    col = lax.broadcasted_iota(jnp.int32, (rb, rb + 2 * HALO), 1)
    select = jnp.concatenate([(col == r + (tap - left)).astype(BF16) for tap in taps], axis=0)
    for blk in range(tl // rb):
        rows = slice(blk * rb, (blk + 1) * rb)
        for h in range(DN_HEADS):
            cols = slice(h * DN_D, (h + 1) * DN_D)
            shifted = _dot(select, ext_ref[blk * rb:(blk + 1) * rb + 2 * HALO, cols])
            conv = x_ref[rows, cols].astype(F32) * w_ref[left:left + 1, cols]
            for n, tap in enumerate(taps):
                conv = conv + shifted[n * rb:(n + 1) * rb] * w_ref[tap:tap + 1, cols]
            y = jax.nn.silu(conv)
            inv_norm = lax.rsqrt(jnp.sum(y * y, axis=-1, keepdims=True) + EPS)
            o_ref[rows, cols] = (y * jnp.where(part < 2, inv_norm * q_scale, 1.0)).astype(BF16)


def _dn_prep(proj, conv_w, groups):
    t = proj.shape[0]
    tl = min(TL_PREP, groups[0][1])
    base = P_RET // D_HALF
    prev, nxt = _halo_specs(tl, D_HALF, t, lambda i, part: base + part)
    return pl.pallas_call(
        functools.partial(_dnprep_kernel, tl=tl, groups=groups),
        out_shape=jax.ShapeDtypeStruct((t, 3 * D_HALF), BF16),
        grid=(t // tl, 3),
        in_specs=[pl.BlockSpec((tl, D_HALF), lambda i, part: (i, base + part)), prev, nxt,
                  pl.BlockSpec((DN_CONV, D_HALF), lambda i, part: (0, part))],
        out_specs=pl.BlockSpec((tl, D_HALF), lambda i, part: (i, part)),
        scratch_shapes=[pltpu.VMEM((tl + 2 * HALO, D_HALF), BF16)],
        compiler_params=_params("parallel", "parallel"),
        name="dn_prep",
    )(proj, proj, proj, conv_w)


def _dot_hi_each(lhs_parts, rhs_parts):
    m = lhs_parts[0][0].shape[0]
    n = rhs_parts[0][0].shape[1]
    quads = [_dot(jnp.concatenate(a, axis=0), jnp.concatenate(b, axis=1)) for a, b in zip(lhs_parts, rhs_parts)]
    return [(x[:m, :n] + x[m:, :n]) + (x[:m, n:] + x[m:, n:]) for x in quads]


def _block_diag(y, left):
    zero = jnp.zeros_like(y)
    return jnp.concatenate([jnp.where(left, y, zero), jnp.where(left, zero, y)], axis=0)


def _pair_products(lhs, rhs, left):
    ls = [_split_bf16(x) for x in lhs]
    rs = [tuple(_block_diag(part, left) for part in _split_bf16(y)) for y in rhs]
    return _dot_hi_each(ls, rs)


def _unit_triangular_inverses(mats, eye, ri, ci, left):
    size = SUBLANES
    same = (ri // size) == (ci // size)
    ps = [jnp.where(same, a, 0.0) for a in mats]
    invs = [eye - d for d in ps]
    n = 2
    while n < size:
        ps = _pair_products(ps, ps, left)
        invs = [inv + x for inv, x in zip(invs, _pair_products(invs, ps, left))]
        n *= 2
    while size < DN_CHUNK:
        size *= 2
        merged = (ri // size) == (ci // size)
        es = [jnp.where(merged & ~same, a, 0.0) for a in mats]
        invs = [inv - x for inv, x in zip(invs, _pair_products(_pair_products(invs, es, left), invs, left))]
        same = merged
    return invs


def _dn_kernel(*refs, tl, nt, groups, reverse):
    if reverse:
        q_ref, k_ref, v_ref, gates_ref, alog_ref, dtb_ref, o_ref, s_ref = refs
    else:
        (q_ref, k_ref, v_ref, gates_ref, alog_ref, dtb_ref, z_ref, ob_ref, ng_ref,
         o_ref, s_ref) = refs
    j = pl.program_id(0)
    it = nt - 1 - j if reverse else j
    row0 = it * tl
    _, sstart, slen = _seq_info(row0, groups)
    pos0 = row0 - sstart
    reset = (pos0 + tl == slen) if reverse else (pos0 == 0)

    @pl.when(reset)
    def _():
        s_ref[...] = jnp.zeros_like(s_ref)

    gates = gates_ref[...]
    beta_all = jax.nn.sigmoid(gates)
    la_all = -jnp.exp(alog_ref[...]) * jax.nn.softplus(gates + dtb_ref[...])
    cb0 = DN_HEADS if reverse else 0
    ca0 = cb0 + 2 * DN_HEADS

    c = DN_CHUNK
    left = lax.broadcasted_iota(jnp.int32, (1, 2 * c), 1) < c
    ri = lax.broadcasted_iota(jnp.int32, (c, 2 * c), 0)
    ci = jnp.bitwise_and(lax.broadcasted_iota(jnp.int32, (c, 2 * c), 1), c - 1)
    eye = (ri == ci).astype(F32)
    incl = (ri <= ci) if reverse else (ri >= ci)
    strict = (ri < ci) if reverse else (ri > ci)
    incl_t = (ri >= ci) if reverse else (ri <= ci)
    last = 0 if reverse else c - 1

    nc = tl // c
    order = list(range(nc - 1, -1, -1) if reverse else range(nc))
    heads = range(DN_HEADS)
    half_heads = range(DN_HEADS // 2)
    units = [(ch, h) for ch in order for h in heads]
    pairs = range(len(units) // 2)

    def rows(ch):
        return slice(ch * c, (ch + 1) * c)

    def cols(h):
        return slice(h * DN_D, (h + 1) * DN_D)

    def paired(xs):
        return [jnp.where(left, xs[2 * p], xs[2 * p + 1]) for p in pairs]

    q = [q_ref[rows(ch), cols(h)] for ch, h in units]
    k = [k_ref[rows(ch), cols(h)] for ch, h in units]
    beta = [beta_all[rows(ch), cb0 + h:cb0 + h + 1] for ch, h in units]
    la = [la_all[rows(ch), ca0 + h:ca0 + h + 1] for ch, h in units]
    la2 = paired(la)
    la_row = [jnp.sum(eye * x, axis=0, keepdims=True) for x in la2]
    g_row = [jnp.sum(jnp.where(incl_t, x, 0.0), axis=0, keepdims=True) for x in la2]
    windowed = [jnp.where(incl, x, 0.0) for x in la_row]
    g_col = [jnp.sum(jnp.where(left == (u % 2 == 0), windowed[u // 2], 0.0), axis=1, keepdims=True)
             for u in range(len(units))]
    gam = [jnp.where(incl, jnp.exp(jnp.where(incl, gc - gr, 0.0)), 0.0) for gc, gr in zip(paired(g_col), g_row)]
    eg = [jnp.exp(gc) for gc in g_col]
    g_last = [gc[last:last + 1, :] for gc in g_col]
    kb = [x.astype(F32) * b for x, b in zip(k, beta)]
    kq = [_dot_nt(jnp.concatenate([kb[2 * p].astype(BF16), q[2 * p], kb[2 * p + 1].astype(BF16), q[2 * p + 1]],
                                  axis=0),
                  jnp.concatenate([k[2 * p], k[2 * p + 1]], axis=0)) for p in pairs]
    a = [jnp.where(strict, jnp.where(left, x[:c], x[2 * c:3 * c]) * gm, 0.0) for x, gm in zip(kq, gam)]
    attn = [_block_diag((jnp.where(left, x[c:2 * c], x[3 * c:]) * gm).astype(BF16), left)
            for x, gm in zip(kq, gam)]
    tinv = _unit_triangular_inverses(a, eye, ri, ci, left)
    rhs = [jnp.concatenate([v_ref[rows(ch), cols(h)].astype(F32) * b, x * e], axis=1)
           for (ch, h), b, x, e in zip(units, beta, kb, eg)]
    uw2 = _dot_hi_each([tuple(_block_diag(part, left) for part in _split_bf16(t)) for t in tinv],
                       [_split_bf16(jnp.concatenate([rhs[2 * p], rhs[2 * p + 1]], axis=0)) for p in pairs])
    uw = [uw2[u // 2][(u % 2) * c:(u % 2 + 1) * c] for u in range(len(units))]
    tail = [jnp.exp(gl - gc) for gl, gc in zip(g_last, g_col)]
    dec = [jnp.exp(gl) for gl in g_last]

    for ci_, ch in enumerate(order):
        idx = [ci_ * DN_HEADS + h for h in heads]
        s = [s_ref[h] for h in heads]
        sb = [x.astype(BF16) for x in s]
        ws = [_dot(jnp.concatenate([uw[i][:, DN_D:].astype(BF16), q[i]], axis=0), sb[h])
              for h, i in zip(heads, idx)]
        v_new = [uw[i][:, :DN_D] - x[:c] for i, x in zip(idx, ws)]
        av = [_dot(attn[idx[2 * x] // 2],
                   jnp.concatenate([v_new[2 * x].astype(BF16), v_new[2 * x + 1].astype(BF16)], axis=0))
              for x in half_heads]
        o = [ws[h][c:] * eg[idx[h]] + av[h // 2][(h % 2) * c:(h % 2 + 1) * c] for h in heads]
        for h, i in zip(heads, idx):
            s_ref[h] = s[h] * dec[i] + _dot_tn(k[i], (v_new[h] * tail[i]).astype(BF16))
        for h in heads:
            if reverse:
                o_ref[rows(ch), cols(h)] = o[h]
            else:
                oo = o[h] + ob_ref[rows(ch), cols(h)]
                on = (oo * lax.rsqrt(jnp.mean(oo * oo, axis=-1, keepdims=True) + EPS)) * ng_ref[...]
                z = z_ref[rows(ch), cols(h)].astype(F32)
                o_ref[rows(ch), cols(h)] = (on * jax.nn.silu(z)).astype(BF16)


def _deltanet(proj, gate_cols, qkv, alog_row, dtb_row, norm_g, groups):
    t = proj.shape[0]
    tl = min(TL_DN, groups[0][1])
    nt = t // tl
    z_blk = (P_RET + 3 * D_HALF) // D_HALF
    row = pl.BlockSpec((1, LANES), lambda j: (0, 0))
    state = pltpu.VMEM((DN_HEADS, DN_D, DN_D), F32)

    def common(reverse):
        tile = (lambda j: nt - 1 - j) if reverse else (lambda j: j)
        sec = lambda s: pl.BlockSpec((tl, D_HALF), lambda j: (tile(j), s))
        gates = pl.BlockSpec((tl, LANES), lambda j: (tile(j), 0))
        return sec, gates

    sec, gates = common(True)
    ob = pl.pallas_call(
        functools.partial(_dn_kernel, tl=tl, nt=nt, groups=groups, reverse=True),
        out_shape=jax.ShapeDtypeStruct((t, D_HALF), F32),
        grid=(nt,),
        in_specs=[sec(0), sec(1), sec(2), gates, row, row],
        out_specs=sec(0),
        scratch_shapes=[state],
        compiler_params=_params("arbitrary"),
        name="dn_bwd",
    )(qkv, qkv, qkv, gate_cols, alog_row, dtb_row)

    sec, gates = common(False)
    return pl.pallas_call(
        functools.partial(_dn_kernel, tl=tl, nt=nt, groups=groups, reverse=False),
        out_shape=jax.ShapeDtypeStruct((t, D_HALF), BF16),
        grid=(nt,),
        in_specs=[sec(0), sec(1), sec(2), gates, row, row, sec(z_blk), sec(0), row],
        out_specs=sec(0),
        scratch_shapes=[state],
        compiler_params=_params("arbitrary"),
        name="dn_fwd",
    )(qkv, qkv, qkv, gate_cols, alog_row, dtb_row, proj, ob, norm_g)


def _gate_row(f_vals, b_vals):
    row = jnp.zeros((LANES,), F32)
    row = row.at[2 * DN_HEADS:3 * DN_HEADS].set(f_vals.astype(F32))
    row = row.at[3 * DN_HEADS:4 * DN_HEADS].set(b_vals.astype(F32))
    return row.reshape(1, LANES)


def _trunk(x, c, p, groups):
    t, d = x.shape
    n_seq = c.shape[0]
    c_pad = jnp.zeros((SEQ_PAD, d), F32).at[:n_seq].set(c)
    mods = _ada(c_pad, p['w_ada'], p['b_ada'])
    mods = mods.reshape(DEPTH, SEQ_PAD, N_MOD, 1, d).transpose(0, 2, 1, 3, 4)
    fin = _ada(c_pad, p['w_ada_final'][None], p['b_ada_final'][None])
    fin = fin.reshape(SEQ_PAD, 2, 1, d).transpose(1, 0, 2, 3)

    max_len = max(ln for _, ln in groups)
    cos, sin = _rope_tables(max_len, min(TL_RET, groups[0][1]))

    row = lambda a: a.reshape(1, -1)
    for layer in range(DEPTH):
        sh1, sc1, g1, sh2, sc2, g2, sh3, sc3, g3 = [mods[layer, jm] for jm in range(N_MOD)]
        x = _ffn(x, row(p['norm_ffn1'][layer]), sh1, sc1, g1, p['w_ffn1_in'], p['w_ffn1_out'], layer, groups)
        idx = layer // 2
        gain = row(p['norm_mix'][layer])
        if layer % 2 == 0:
            proj = _proj(x, gain, sh2, sc2, p['w_in_even'], idx, P_EVEN, groups)
            y = _even_mix(proj, p['pool_w'][idx], row(p['pool_scale'][idx]), row(p['sgu_norm'][idx]),
                          p['sgu_w'][idx], p['sgu_b'][idx][..., None], groups)
            x = _outproj(x, y, y, 0, 1, g2, p['w_out_even'], idx, groups)
        else:
            proj, gate_cols = _proj(x, gain, sh2, sc2, p['w_in_odd'], idx, P_ODD_MAIN, groups,
                                    odd_extras=(p['w_in_odd_gates'], cos, sin))
            yc = _retention(proj, p['ret_decay_f'][idx], p['ret_decay_b'][idx],
                            row(p['ret_norm'][idx]), groups)
            qkv = _dn_prep(proj, p['dn_conv'][idx], groups)
            yd = _deltanet(proj, gate_cols, qkv,
                           _gate_row(p['dn_a_log_f'][idx], p['dn_a_log_b'][idx]),
                           _gate_row(p['dn_dt_bias_f'][idx], p['dn_dt_bias_b'][idx]),
                           row(p['dn_norm'][idx]), groups)
            x = _outproj(x, yc, yd, 0, 0, g2, p['w_out_odd'], idx, groups)
        x = _ffn(x, row(p['norm_ffn2'][layer]), sh3, sc3, g3, p['w_ffn2_in'], p['w_ffn2_out'], layer, groups)
    return _final(x, row(p['norm_final']), fin[0], fin[1], groups)


def _prepare(p):
    q = dict(p)
    for name in ('w_ffn1_in', 'w_ffn1_out', 'w_ffn2_in', 'w_ffn2_out', 'w_in_even', 'w_out_even',
                 'pool_w', 'sgu_w', 'w_in_odd', 'w_out_odd'):
        q[name] = p[name].astype(BF16)
    gates = q['w_in_odd'][:, :, P_ODD_MAIN:]
    q['w_in_odd_gates'] = jnp.pad(gates, ((0, 0), (0, 0), (0, LANES - gates.shape[-1])))
    return q


def kernel(x_prompt, x_sample, c_prompt, c_sample, w_ada, b_ada, norm_ffn1, w_ffn1_in, w_ffn1_out, norm_mix, norm_ffn2, w_ffn2_in, w_ffn2_out, w_in_even, w_out_even, pool_w, pool_scale, sgu_norm, sgu_w, sgu_b, w_in_odd, w_out_odd, ret_decay_f, ret_decay_b, ret_norm, dn_conv, dn_a_log_f, dn_a_log_b, dn_dt_bias_f, dn_dt_bias_b, dn_norm, norm_final, w_ada_final, b_ada_final):
    p = _prepare({
        'w_ada': w_ada, 'b_ada': b_ada, 'norm_ffn1': norm_ffn1, 'w_ffn1_in': w_ffn1_in,
        'w_ffn1_out': w_ffn1_out, 'norm_mix': norm_mix, 'norm_ffn2': norm_ffn2,
        'w_ffn2_in': w_ffn2_in, 'w_ffn2_out': w_ffn2_out, 'w_in_even': w_in_even,
        'w_out_even': w_out_even, 'pool_w': pool_w, 'pool_scale': pool_scale,
        'sgu_norm': sgu_norm, 'sgu_w': sgu_w, 'sgu_b': sgu_b, 'w_in_odd': w_in_odd,
        'w_out_odd': w_out_odd, 'ret_decay_f': ret_decay_f, 'ret_decay_b': ret_decay_b,
        'ret_norm': ret_norm, 'dn_conv': dn_conv, 'dn_a_log_f': dn_a_log_f,
        'dn_a_log_b': dn_a_log_b, 'dn_dt_bias_f': dn_dt_bias_f, 'dn_dt_bias_b': dn_dt_bias_b,
        'dn_norm': dn_norm, 'norm_final': norm_final, 'w_ada_final': w_ada_final,
        'b_ada_final': b_ada_final,
    })
    bp, lp, d = x_prompt.shape
    bs, ls, _ = x_sample.shape
    groups = ((bp, lp), (bs, ls))
    x = jnp.concatenate([x_prompt.reshape(bp * lp, d), x_sample.reshape(bs * ls, d)], axis=0)
    c = jnp.concatenate([c_prompt, c_sample], axis=0)
    y_prompt, y_sample = _trunk(x, c, p, groups)
    return (y_prompt.reshape(bp, lp, d), y_sample.reshape(bs, ls, d))
```

```python
import functools

import jax
import jax.numpy as jnp
from jax import lax
from jax.experimental import pallas as pl
from jax.experimental.pallas import tpu as pltpu

F32 = jnp.float32
BF16 = jnp.bfloat16

D_MODEL = 2048
DEPTH = 4
D_HALF = D_MODEL // 2
POOL_WINDOWS = (2, 4, 8, 16)
POOL_GROUP = D_HALF // len(POOL_WINDOWS)
SGU_CHUNK = 128
SGU_HEAD = 128
SGU_GROUPS = D_HALF // SGU_HEAD
RET_HEADS = 4
RET_D = D_HALF // RET_HEADS
RET_CHUNK = 128
ROPE_BASE = 10000.0
DN_HEADS = 8
DN_D = D_HALF // DN_HEADS
DN_CONV = 4
DN_CHUNK = 64
D_FF = 5632
N_MOD = 9
EPS = 1e-6
P_EVEN = 3 * D_HALF
P_RET = 4 * D_HALF
P_ODD_MAIN = P_RET + 4 * D_HALF
P_ODD = P_ODD_MAIN + 4 * DN_HEADS

LANES = 128
SUBLANES = 8
HALO = 2 * SUBLANES
VMEM_LIMIT = 56 * 1024 * 1024

TM_FFN = 1024
TF_FFN = 512
TM_PROJ = 1024
TN_PROJ = 1024
MOD_ROWS = 16
MOD_UNROLL = 8
TM_OUT = 512
TL_EVEN = 256
TL_RET = 512
TL_DN = 512
TL_PREP = 512
PREP_ROWS = 64
TN_ADA = 1024
SEQ_PAD = 16


def _seq_info(row0, groups):
    seq = start = length = None
    t0 = s0 = 0
    for gi, (nb, ln) in enumerate(groups):
        rel = row0 - t0
        q = rel // ln
        if gi == 0:
            seq, start, length = q, q * ln, ln
        else:
            here = row0 >= t0
            seq = jnp.where(here, s0 + q, seq)
            start = jnp.where(here, t0 + q * ln, start)
            length = jnp.where(here, ln, length)
        t0 += nb * ln
        s0 += nb
    return seq, start, length


def _params(*sem):
    return pltpu.CompilerParams(dimension_semantics=sem, vmem_limit_bytes=VMEM_LIMIT)


def _dot(a, b):
    return jnp.dot(a, b, preferred_element_type=F32)


def _dot_nt(a, b):
    return lax.dot_general(a, b, (((1,), (1,)), ((), ())), preferred_element_type=F32)


def _dot_tn(a, b):
    return lax.dot_general(a, b, (((0,), (0,)), ((), ())), preferred_element_type=F32)


def _split_bf16(a):
    hi = a.astype(BF16)
    lo = (a - hi.astype(F32)).astype(BF16)
    return hi, lo


def _ada_kernel(c_ref, w_ref, b_ref, o_ref):
    c = c_ref[...]
    act = jax.nn.silu(c).astype(BF16)
    o_ref[...] = _dot(act, w_ref[...].astype(BF16)) + b_ref[...]


def _ada(c_pad, w, b):
    ly, d, n = w.shape
    s = c_pad.shape[0]
    tn = min(TN_ADA, n)
    return pl.pallas_call(
        _ada_kernel,
        out_shape=jax.ShapeDtypeStruct((ly, s, n), F32),
        grid=(ly, n // tn),
        in_specs=[
            pl.BlockSpec((s, d), lambda l, j: (0, 0)),
            pl.BlockSpec((None, d, tn), lambda l, j: (l, 0, j)),
            pl.BlockSpec((None, 1, tn), lambda l, j: (l, 0, j)),
        ],
        out_specs=pl.BlockSpec((None, s, tn), lambda l, j: (l, 0, j)),
        compiler_params=_params("parallel", "parallel"),
        name="ada_rows",
    )(c_pad, w, b.reshape(ly, 1, n))


def _modulate_into(h_ref, x_ref, gain_ref, sh_ref, sc_ref, zero_ref=None):
    tm = x_ref.shape[0]
    amp = gain_ref[...] * (1.0 + sc_ref[...])
    shift = sh_ref[...]

    def body(r, carry):
        rows = pl.ds(pl.multiple_of(r * MOD_ROWS, MOD_ROWS), MOD_ROWS)
        x = x_ref[rows, :]
        ms = jnp.mean(x * x, axis=-1, keepdims=True)
        h_ref[rows, :] = ((x * lax.rsqrt(ms + EPS)) * amp + shift).astype(BF16)
        if zero_ref is not None:
            zero_ref[rows, :] = jnp.zeros((MOD_ROWS, zero_ref.shape[1]), zero_ref.dtype)
        return carry

    lax.fori_loop(0, tm // MOD_ROWS, body, 0, unroll=MOD_UNROLL)


def _ffn_kernel(*refs, nf, carried, final):
    x_ref, gain_ref, sh_ref, sc_ref, gt_ref, wg_ref, wu_ref, wo_ref = refs[:8]
    rest = refs[8:]
    if final:
        fgain_ref, fsh_ref, fsc_ref = rest[:3]
        rest = rest[3:]
    if carried:
        rest = rest[1:]
    o_ref, h_ref = rest
    f = pl.program_id(1)

    @pl.when(f == 0)
    def _():
        _modulate_into(h_ref, x_ref, gain_ref, sh_ref, sc_ref, zero_ref=o_ref)

    h = h_ref[...]
    g = _dot(h, wg_ref[...])
    u = _dot(h, wu_ref[...])
    a = (jax.nn.silu(g) * u).astype(BF16)
    o_ref[...] += _dot(a, wo_ref[...])

    @pl.when(f == nf - 1)
    def _():
        half_gate = 0.5 * gt_ref[...]
        if not final:
            o_ref[...] = x_ref[...] + half_gate * o_ref[...]
            return
        amp = fgain_ref[...] * (1.0 + fsc_ref[...])
        shift = fsh_ref[...]

        def body(r, carry):
            rows = pl.ds(pl.multiple_of(r * MOD_ROWS, MOD_ROWS), MOD_ROWS)
            y = x_ref[rows, :] + half_gate * o_ref[rows, :]
            ms = jnp.mean(y * y, axis=-1, keepdims=True)
            o_ref[rows, :] = (y * lax.rsqrt(ms + EPS)) * amp + shift
            return carry

        lax.fori_loop(0, x_ref.shape[0] // MOD_ROWS, body, 0, unroll=MOD_UNROLL)


def _row_spec(groups, tm, d, row_off=0):
    return pl.BlockSpec((None, 1, d), lambda i, j: (_seq_info(i * tm + row_off, groups)[0], 0, 0))


def _ffn(x, gain, shift, scale, gate, w_in, w_out, layer, groups, *, span=None, x_is_span=False,
         out_is_span=False, carry=None, final=None):
    d = x.shape[1]
    t = sum(nb * ln for nb, ln in groups)
    row_off, n_rows = span if span is not None else (0, t)
    ff = w_out.shape[1]
    tm = min(TM_FFN, groups[0][1])
    tf = min(TF_FFN, ff)
    nf = ff // tf
    tile_off = row_off // tm
    x_off = 0 if x_is_span else tile_off
    o_off = 0 if out_is_span else tile_off
    row = _row_spec(groups, tm, d, row_off)
    in_specs = [
        pl.BlockSpec((tm, d), lambda i, f: (i + x_off, 0)),
        pl.BlockSpec((1, d), lambda i, f: (0, 0)),
        row, row, row,
        pl.BlockSpec((None, d, tf), lambda i, f: (layer, 0, f)),
        pl.BlockSpec((None, d, tf), lambda i, f: (layer, 0, nf + f)),
        pl.BlockSpec((None, tf, d), lambda i, f: (layer, f, 0)),
    ]
    args = [x, gain, shift, scale, gate, w_in, w_in, w_out]
    if final is not None:
        in_specs += [pl.BlockSpec((1, d), lambda i, f: (0, 0)), row, row]
        args += list(final)
    aliases = {}
    if carry is not None:
        in_specs.append(pl.BlockSpec(memory_space=pl.ANY))
        aliases = {len(args): 0}
        args.append(carry)
    return pl.pallas_call(
        functools.partial(_ffn_kernel, nf=nf, carried=carry is not None, final=final is not None),
        out_shape=jax.ShapeDtypeStruct((n_rows if out_is_span else t, d), F32),
        grid=(n_rows // tm, nf),
        in_specs=in_specs,
        out_specs=pl.BlockSpec((tm, d), lambda i, f: (i + o_off, 0)),
        scratch_shapes=[pltpu.VMEM((tm, d), BF16)],
        input_output_aliases=aliases,
        compiler_params=_params("parallel", "arbitrary"),
        name="ffn",
    )(*args)


def _proj_kernel(*refs, odd):
    if odd:
        x_ref, gain_ref, sh_ref, sc_ref, w_ref, wn_ref, cos_ref, sin_ref, o_ref, on_ref, h_ref = refs
    else:
        x_ref, gain_ref, sh_ref, sc_ref, w_ref, o_ref, h_ref = refs
    j = pl.program_id(1)

    @pl.when(j == 0)
    def _():
        _modulate_into(h_ref, x_ref, gain_ref, sh_ref, sc_ref)
        if odd:
            on_ref[...] = _dot(h_ref[...], wn_ref[...])

    y = _dot(h_ref[...], w_ref[...])
    if not odd:
        o_ref[...] = y.astype(BF16)
        return

    tn = o_ref.shape[1]
    rope_tiles = 2 * D_HALF // tn
    half = RET_D // 2
    rotate = j < rope_tiles
    cos = jnp.where(rotate, cos_ref[...], 1.0)
    sin = jnp.where(rotate, sin_ref[...], 0.0)
    k_scale = jnp.where(rotate & (j >= rope_tiles // 2), RET_D ** -0.5, 1.0).astype(F32)
    for hd in range(tn // RET_D):
        x1 = y[:, hd * RET_D:hd * RET_D + half]
        x2 = y[:, hd * RET_D + half:(hd + 1) * RET_D]
        o_ref[:, hd * RET_D:hd * RET_D + half] = ((x1 * cos - x2 * sin) * k_scale).astype(BF16)
        o_ref[:, hd * RET_D + half:(hd + 1) * RET_D] = ((x1 * sin + x2 * cos) * k_scale).astype(BF16)


def _proj(x, gain, shift, scale, w, idx, n, groups, odd_extras=None):
    t, d = x.shape
    tm = min(TM_PROJ, groups[0][1])
    tn = TN_PROJ
    row = _row_spec(groups, tm, d)
    odd = odd_extras is not None
    in_specs = [
        pl.BlockSpec((tm, d), lambda i, j: (i, 0)),
        pl.BlockSpec((1, d), lambda i, j: (0, 0)),
        row, row,
        pl.BlockSpec((None, d, tn), lambda i, j: (idx, 0, j)),
    ]
    out_shape = jax.ShapeDtypeStruct((t, n), BF16)
    out_specs = pl.BlockSpec((tm, tn), lambda i, j: (i, j))
    args = (x, gain, shift, scale, w)
    if odd:
        half = RET_D // 2

        def pos_block(i, j):
            row0 = i * tm
            return ((row0 - _seq_info(row0, groups)[1]) // tm, 0)

        tab = pl.BlockSpec((tm, half), pos_block)
        in_specs += [pl.BlockSpec((None, d, LANES), lambda i, j: (idx, 0, 0)), tab, tab]
        out_shape = (out_shape, jax.ShapeDtypeStruct((t, LANES), F32))
        out_specs = (out_specs, pl.BlockSpec((tm, LANES), lambda i, j: (i, 0)))
        args = args + tuple(odd_extras)
    return pl.pallas_call(
        functools.partial(_proj_kernel, odd=odd),
        out_shape=out_shape,
        grid=(t // tm, n // tn),
        in_specs=in_specs,
        out_specs=out_specs,
        scratch_shapes=[pltpu.VMEM((tm, d), BF16)],
        compiler_params=_params("parallel", "arbitrary"),
        name="mix_proj",
    )(*args)


def _outproj_kernel(x_ref, ya_ref, yb_ref, gt_ref, wa_ref, wb_ref, o_ref):
    y = _dot(ya_ref[...], wa_ref[...]) + _dot(yb_ref[...], wb_ref[...])
    o_ref[...] = x_ref[...] + gt_ref[...] * y


def _outproj(x, ya, yb, ca, cb, gate, w, idx, groups):
    t, d = x.shape
    dh = d // 2
    tm = min(TM_OUT, groups[0][1])
    row = _row_spec(groups, tm, d)
    return pl.pallas_call(
        _outproj_kernel,
        out_shape=jax.ShapeDtypeStruct((t, d), F32),
        grid=(t // tm, 1),
        in_specs=[
            pl.BlockSpec((tm, d), lambda i, j: (i, 0)),
            pl.BlockSpec((tm, dh), lambda i, j: (i, ca)),
            pl.BlockSpec((tm, dh), lambda i, j: (i, cb)),
            row,
            pl.BlockSpec((None, dh, d), lambda i, j: (idx, 0, 0)),
            pl.BlockSpec((None, dh, d), lambda i, j: (idx, 1, 0)),
        ],
        out_specs=pl.BlockSpec((tm, d), lambda i, j: (i, 0)),
        compiler_params=_params("parallel", "arbitrary"),
        name="mix_out",
    )(x, ya, yb, gate, w, w)


def _halo_specs(tl, width, nrows, col_of):
    per = tl // HALO
    last = nrows // HALO - 1
    prev = pl.BlockSpec((HALO, width), lambda i, *r: (jnp.maximum(i * per - 1, 0), col_of(i, *r)))
    nxt = pl.BlockSpec((HALO, width), lambda i, *r: (jnp.minimum((i + 1) * per, last), col_of(i, *r)))
    return prev, nxt


def _fill_ext(ext_ref, x_ref, prev_ref, next_ref, first, last, tl):
    dt = ext_ref.dtype
    ext_ref[HALO:HALO + tl, :] = x_ref[...].astype(dt)
    ext_ref[0:HALO, :] = jnp.where(first, jnp.zeros_like(prev_ref), prev_ref[...]).astype(dt)
    ext_ref[HALO + tl:2 * HALO + tl, :] = jnp.where(last, jnp.zeros_like(next_ref), next_ref[...]).astype(dt)


def _row_window(tl, lo, hi):
    r = lax.broadcasted_iota(jnp.int32, (tl, tl + 2 * HALO), 0) + HALO
    col = lax.broadcasted_iota(jnp.int32, (tl, tl + 2 * HALO), 1)
    return ((col >= r + lo) & (col < r + hi)).astype(BF16)


def _even_kernel(xa_ref, prev_ref, next_ref, u_ref, v_ref, pw_ref, ps_ref, ng_ref, sw_ref, sb_ref,
                 o_ref, ext_ref, vn_ref, win_ref, *, tl, groups):
    @pl.when(pl.program_id(0) == 0)
    def _():
        for gi, w in enumerate(POOL_WINDOWS):
            win_ref[gi] = _row_window(tl, -(w // 2), w - w // 2)

    row0 = pl.program_id(0) * tl
    _, sstart, slen = _seq_info(row0, groups)
    pos0 = row0 - sstart
    _fill_ext(ext_ref, xa_ref, prev_ref, next_ref, pos0 == 0, pos0 + tl == slen, tl)

    t = pos0 + lax.broadcasted_iota(jnp.int32, (tl, 1), 0)
    for gi, w in enumerate(POOL_WINDOWS):
        c0 = gi * POOL_GROUP
        cols = slice(c0, c0 + POOL_GROUP)
        s = _dot(win_ref[gi], ext_ref[:, cols])
        lo = jnp.clip(t - w // 2, 0, slen)
        hi = jnp.clip(t + (w - w // 2), 0, slen)
        cnt = (hi - lo).astype(F32)
        pooled = (s / cnt - xa_ref[:, cols].astype(F32)).astype(BF16)
        ya = _dot(pooled, pw_ref[gi]) * ps_ref[:, cols]
        o_ref[:, cols] = ya.astype(BF16)

    v = jax.nn.gelu(v_ref[...].astype(F32))
    vms = jnp.mean(v * v, axis=-1, keepdims=True)
    vn_ref[...] = ((v * lax.rsqrt(vms + EPS)) * ng_ref[...]).astype(BF16)
    for n in range(tl // SGU_CHUNK):
        rows = slice(n * SGU_CHUNK, (n + 1) * SGU_CHUNK)
        for g in range(SGU_GROUPS):
            cols = slice(g * SGU_HEAD, (g + 1) * SGU_HEAD)
            mixed = _dot(sw_ref[g], vn_ref[rows, cols]) + sb_ref[g]
            u = jax.nn.gelu(u_ref[rows, cols].astype(F32))
            o_ref[rows, D_HALF + g * SGU_HEAD:D_HALF + (g + 1) * SGU_HEAD] = (u * mixed).astype(BF16)


def _even_mix(proj, pool_w, pool_scale, sgu_norm, sgu_w, sgu_b, groups):
    t = proj.shape[0]
    tl = min(TL_EVEN, groups[0][1])
    prev, nxt = _halo_specs(tl, D_HALF, t, lambda i: 0)
    const2 = lambda i: (0, 0)
    const3 = lambda i: (0, 0, 0)
    return pl.pallas_call(
        functools.partial(_even_kernel, tl=tl, groups=groups),
        out_shape=jax.ShapeDtypeStruct((t, 2 * D_HALF), BF16),
        grid=(t // tl,),
        in_specs=[
            pl.BlockSpec((tl, D_HALF), lambda i: (i, 0)),
            prev, nxt,
            pl.BlockSpec((tl, D_HALF), lambda i: (i, 1)),
            pl.BlockSpec((tl, D_HALF), lambda i: (i, 2)),
            pl.BlockSpec(pool_w.shape, const3),
            pl.BlockSpec((1, D_HALF), const2),
            pl.BlockSpec((1, D_HALF), const2),
            pl.BlockSpec(sgu_w.shape, const3),
            pl.BlockSpec(sgu_b.shape, const3),
        ],
        out_specs=pl.BlockSpec((tl, 2 * D_HALF), lambda i: (i, 0)),
        scratch_shapes=[pltpu.VMEM((tl + 2 * HALO, D_HALF), BF16), pltpu.VMEM((tl, D_HALF), BF16),
                        pltpu.VMEM((len(POOL_WINDOWS), tl, tl + 2 * HALO), BF16)],
        compiler_params=_params("arbitrary"),
        name="even_mix",
    )(proj, proj, proj, proj, proj, pool_w, pool_scale, sgu_norm, sgu_w, sgu_b)


def _rope_kernel(inv_ref, cos_ref, sin_ref, *, tl):
    pos = (pl.program_id(0) * tl + lax.broadcasted_iota(jnp.int32, (tl, 1), 0)).astype(F32)
    ang = pos * inv_ref[...]
    cos_ref[...] = jnp.cos(ang)
    sin_ref[...] = jnp.sin(ang)


def _rope_tables(max_len, tl):
    half = RET_D // 2
    inv = (1.0 / (ROPE_BASE ** jnp.linspace(0.0, 1.0, half, dtype=F32))).reshape(1, half)
    shp = jax.ShapeDtypeStruct((max_len, half), F32)
    return pl.pallas_call(
        functools.partial(_rope_kernel, tl=tl),
        out_shape=(shp, shp),
        grid=(max_len // tl,),
        in_specs=[pl.BlockSpec((1, half), lambda i: (0, 0))],
        out_specs=(pl.BlockSpec((tl, half), lambda i: (i, 0)), pl.BlockSpec((tl, half), lambda i: (i, 0))),
        compiler_params=_params("parallel"),
        name="rope_table",
    )(inv)


def _ret_kernel(*refs, tl, nt, groups, reverse):
    if reverse:
        q_ref, k_ref, v_ref, dec_ref, o_ref, s_ref = refs
    else:
        q_ref, k_ref, v_ref, dec_ref, decb_ref, g_ref, ob_ref, ng_ref, o_ref, s_ref = refs
    j = pl.program_id(0)
    it = nt - 1 - j if reverse else j
    row0 = it * tl
    _, sstart, slen = _seq_info(row0, groups)
    pos0 = row0 - sstart
    reset = (pos0 + tl == slen) if reverse else (pos0 == 0)

    @pl.when(reset)
    def _():
        s_ref[...] = jnp.zeros_like(s_ref)

    c = RET_CHUNK
    heads = range(RET_HEADS)
    idx = lax.broadcasted_iota(jnp.int32, (c, 1), 0).astype(F32)
    lg = [jnp.log1p(-jnp.exp2(-dec_ref[h])) for h in heads]
    if reverse:
        q_dec = [jnp.exp(x * (c - idx)) for x in lg]
        k_dec = [jnp.exp(x * idx) for x in lg]
    else:
        q_dec = [jnp.exp(x * (idx + 1.0)) for x in lg]
        k_dec = [jnp.exp(x * (c - 1.0 - idx)) for x in lg]
        lgb = [jnp.log1p(-jnp.exp2(-decb_ref[h])) for h in heads]
        ri = lax.broadcasted_iota(jnp.int32, (c, c), 0)
        ci = lax.broadcasted_iota(jnp.int32, (c, c), 1)
        rel = (ri - ci).astype(F32)
        dmat = [jnp.where(rel >= 0, jnp.exp(x * jnp.maximum(rel, 0.0)), 0.0)
                + jnp.where(rel <= 0, jnp.exp(y * jnp.maximum(-rel, 0.0)), 0.0) for x, y in zip(lg, lgb)]
    chunk_dec = [jnp.exp(x * float(c)) for x in lg]

    nc = tl // c
    order = list(range(nc - 1, -1, -1) if reverse else range(nc))
    units = [(ch, h) for ch in order for h in heads]

    def rows(ch):
        return slice(ch * c, (ch + 1) * c)

    def cols(h):
        return slice(h * RET_D, (h + 1) * RET_D)

    q = [q_ref[rows(ch), cols(h)] for ch, h in units]
    v = [v_ref[rows(ch), cols(h)] for ch, h in units]
    kv = [_dot_tn(k_ref[rows(ch), cols(h)], (x.astype(F32) * k_dec[h]).astype(BF16))
          for (ch, h), x in zip(units, v)]
    if not reverse:
        scores = [(_dot_nt(x, k_ref[rows(ch), cols(h)]) * dmat[h]).astype(BF16) for (ch, h), x in zip(units, q)]
        intra = [_dot(x, y) for x, y in zip(scores, v)]

    for ci_, ch in enumerate(order):
        s = [s_ref[h] for h in heads]
        inter = [_dot(q[ci_ * RET_HEADS + h], s[h].astype(BF16)) * q_dec[h] for h in heads]
        for h in heads:
            s_ref[h] = s[h] * chunk_dec[h] + kv[ci_ * RET_HEADS + h]
        for h in heads:
            if reverse:
                o_ref[rows(ch), cols(h)] = inter[h]
            else:
                o = intra[ci_ * RET_HEADS + h] + inter[h] + ob_ref[rows(ch), cols(h)]
                mu = jnp.mean(o, axis=-1, keepdims=True)
                var = jnp.mean(jnp.square(o - mu), axis=-1, keepdims=True)
                on = ((o - mu) * lax.rsqrt(var + EPS)) * ng_ref[:, cols(h)]
                gate = jax.nn.silu(g_ref[rows(ch), cols(h)].astype(F32))
                o_ref[rows(ch), cols(h)] = (gate * on).astype(BF16)


def _retention(proj, decay_f, decay_b, norm_g, groups):
    t = proj.shape[0]
    tl = min(TL_RET, groups[0][1])
    nt = t // tl
    dec = pl.BlockSpec((RET_HEADS, 1, 1), lambda j: (0, 0, 0))
    state = pltpu.VMEM((RET_HEADS, RET_D, RET_D), F32)

    def section(reverse):
        tile = (lambda j: nt - 1 - j) if reverse else (lambda j: j)
        return lambda s: pl.BlockSpec((tl, D_HALF), lambda j: (tile(j), s))

    sec = section(True)
    ob = pl.pallas_call(
        functools.partial(_ret_kernel, tl=tl, nt=nt, groups=groups, reverse=True),
        out_shape=jax.ShapeDtypeStruct((t, D_HALF), F32),
        grid=(nt,),
        in_specs=[sec(0), sec(1), sec(2), dec],
        out_specs=sec(0),
        scratch_shapes=[state],
        compiler_params=_params("arbitrary"),
        name="ret_bwd",
    )(proj, proj, proj, decay_b.reshape(RET_HEADS, 1, 1))

    sec = section(False)
    return pl.pallas_call(
        functools.partial(_ret_kernel, tl=tl, nt=nt, groups=groups, reverse=False),
        out_shape=jax.ShapeDtypeStruct((t, D_HALF), BF16),
        grid=(nt,),
        in_specs=[sec(0), sec(1), sec(2), dec, dec, sec(3), sec(0),
                  pl.BlockSpec((1, D_HALF), lambda j: (0, 0))],
        out_specs=sec(0),
        scratch_shapes=[state],
        compiler_params=_params("arbitrary"),
        name="ret_fwd",
    )(proj, proj, proj, decay_f.reshape(RET_HEADS, 1, 1), decay_b.reshape(RET_HEADS, 1, 1),
      proj, ob, norm_g)


def _dnprep_kernel(x_ref, prev_ref, next_ref, w_ref, o_ref, ext_ref, *, tl, groups):
    row0 = pl.program_id(0) * tl
    part = pl.program_id(1)
    _, sstart, slen = _seq_info(row0, groups)
    pos0 = row0 - sstart
    _fill_ext(ext_ref, x_ref, prev_ref, next_ref, pos0 == 0, pos0 + tl == slen, tl)
    left = DN_CONV // 2
    q_scale = jnp.where(part == 0, DN_D ** -0.5, 1.0).astype(F32)
    rb = PREP_ROWS
    taps = [tap for tap in range(DN_CONV) if tap != left]
    r = lax.broadcasted_iota(jnp.int32, (rb, rb + 2 * HALO), 0) + HALO
    col = lax.broadcasted_iota(jnp.int32, (rb, rb + 2 * HALO), 1)
    select = jnp.concatenate([(col == r + (tap - left)).astype(BF16) for tap in taps], axis=0)
    for blk in range(tl // rb):
        rows = slice(blk * rb, (blk + 1) * rb)
        for h in range(DN_HEADS):
            cols = slice(h * DN_D, (h + 1) * DN_D)
            shifted = _dot(select, ext_ref[blk * rb:(blk + 1) * rb + 2 * HALO, cols])
            conv = x_ref[rows, cols].astype(F32) * w_ref[left:left + 1, cols]
            for n, tap in enumerate(taps):
                conv = conv + shifted[n * rb:(n + 1) * rb] * w_ref[tap:tap + 1, cols]
            y = jax.nn.silu(conv)
            inv_norm = lax.rsqrt(jnp.sum(y * y, axis=-1, keepdims=True) + EPS)
            o_ref[rows, cols] = (y * jnp.where(part < 2, inv_norm * q_scale, 1.0)).astype(BF16)


def _dn_prep(proj, conv_w, groups):
    t = proj.shape[0]
    tl = min(TL_PREP, groups[0][1])
    base = P_RET // D_HALF
    prev, nxt = _halo_specs(tl, D_HALF, t, lambda i, part: base + part)
    return pl.pallas_call(
        functools.partial(_dnprep_kernel, tl=tl, groups=groups),
        out_shape=jax.ShapeDtypeStruct((t, 3 * D_HALF), BF16),
        grid=(t // tl, 3),
        in_specs=[pl.BlockSpec((tl, D_HALF), lambda i, part: (i, base + part)), prev, nxt,
                  pl.BlockSpec((DN_CONV, D_HALF), lambda i, part: (0, part))],
        out_specs=pl.BlockSpec((tl, D_HALF), lambda i, part: (i, part)),
        scratch_shapes=[pltpu.VMEM((tl + 2 * HALO, D_HALF), BF16)],
        compiler_params=_params("parallel", "parallel"),
        name="dn_prep",
    )(proj, proj, proj, conv_w)


def _dot_hi_each(lhs_parts, rhs_parts):
    m = lhs_parts[0][0].shape[0]
    n = rhs_parts[0][0].shape[1]
    quads = [_dot(jnp.concatenate(a, axis=0), jnp.concatenate(b, axis=1)) for a, b in zip(lhs_parts, rhs_parts)]
    return [(x[:m, :n] + x[m:, :n]) + (x[:m, n:] + x[m:, n:]) for x in quads]


def _block_diag(y, left):
    zero = jnp.zeros_like(y)
    return jnp.concatenate([jnp.where(left, y, zero), jnp.where(left, zero, y)], axis=0)


def _pair_products(lhs, rhs, left):
    ls = [_split_bf16(x) for x in lhs]
    rs = [tuple(_block_diag(part, left) for part in _split_bf16(y)) for y in rhs]
    return _dot_hi_each(ls, rs)


def _unit_triangular_inverses(mats, eye, ri, ci, left):
    size = SUBLANES
    same = (ri // size) == (ci // size)
    ps = [jnp.where(same, a, 0.0) for a in mats]
    invs = [eye - d for d in ps]
    n = 2
    while n < size:
        ps = _pair_products(ps, ps, left)
        invs = [inv + x for inv, x in zip(invs, _pair_products(invs, ps, left))]
        n *= 2
    while size < DN_CHUNK:
        size *= 2
        merged = (ri // size) == (ci // size)
        es = [jnp.where(merged & ~same, a, 0.0) for a in mats]
        invs = [inv - x for inv, x in zip(invs, _pair_products(_pair_products(invs, es, left), invs, left))]
        same = merged
    return invs


def _dn_kernel(*refs, tl, nt, groups, reverse):
    if reverse:
        q_ref, k_ref, v_ref, gates_ref, alog_ref, dtb_ref, o_ref, s_ref = refs
    else:
        (q_ref, k_ref, v_ref, gates_ref, alog_ref, dtb_ref, z_ref, ob_ref, ng_ref,
         o_ref, s_ref) = refs
    j = pl.program_id(0)
    it = nt - 1 - j if reverse else j
    row0 = it * tl
    _, sstart, slen = _seq_info(row0, groups)
    pos0 = row0 - sstart
    reset = (pos0 + tl == slen) if reverse else (pos0 == 0)

    @pl.when(reset)
    def _():
        s_ref[...] = jnp.zeros_like(s_ref)

    gates = gates_ref[...]
    beta_all = jax.nn.sigmoid(gates)
    la_all = -jnp.exp(alog_ref[...]) * jax.nn.softplus(gates + dtb_ref[...])
    cb0 = DN_HEADS if reverse else 0
    ca0 = cb0 + 2 * DN_HEADS

    c = DN_CHUNK
    left = lax.broadcasted_iota(jnp.int32, (1, 2 * c), 1) < c
    ri = lax.broadcasted_iota(jnp.int32, (c, 2 * c), 0)
    ci = jnp.bitwise_and(lax.broadcasted_iota(jnp.int32, (c, 2 * c), 1), c - 1)
    eye = (ri == ci).astype(F32)
    incl = (ri <= ci) if reverse else (ri >= ci)
    strict = (ri < ci) if reverse else (ri > ci)
    incl_t = (ri >= ci) if reverse else (ri <= ci)
    last = 0 if reverse else c - 1

    nc = tl // c
    order = list(range(nc - 1, -1, -1) if reverse else range(nc))
    heads = range(DN_HEADS)
    half_heads = range(DN_HEADS // 2)
    units = [(ch, h) for ch in order for h in heads]
    pairs = range(len(units) // 2)

    def rows(ch):
        return slice(ch * c, (ch + 1) * c)

    def cols(h):
        return slice(h * DN_D, (h + 1) * DN_D)

    def paired(xs):
        return [jnp.where(left, xs[2 * p], xs[2 * p + 1]) for p in pairs]

    q = [q_ref[rows(ch), cols(h)] for ch, h in units]
    k = [k_ref[rows(ch), cols(h)] for ch, h in units]
    beta = [beta_all[rows(ch), cb0 + h:cb0 + h + 1] for ch, h in units]
    la = [la_all[rows(ch), ca0 + h:ca0 + h + 1] for ch, h in units]
    la2 = paired(la)
    la_row = [jnp.sum(eye * x, axis=0, keepdims=True) for x in la2]
    g_row = [jnp.sum(jnp.where(incl_t, x, 0.0), axis=0, keepdims=True) for x in la2]
    windowed = [jnp.where(incl, x, 0.0) for x in la_row]
    g_col = [jnp.sum(jnp.where(left == (u % 2 == 0), windowed[u // 2], 0.0), axis=1, keepdims=True)
             for u in range(len(units))]
    gam = [jnp.where(incl, jnp.exp(jnp.where(incl, gc - gr, 0.0)), 0.0) for gc, gr in zip(paired(g_col), g_row)]
    eg = [jnp.exp(gc) for gc in g_col]
    g_last = [gc[last:last + 1, :] for gc in g_col]
    kb = [x.astype(F32) * b for x, b in zip(k, beta)]
    kq = [_dot_nt(jnp.concatenate([kb[2 * p].astype(BF16), q[2 * p], kb[2 * p + 1].astype(BF16), q[2 * p + 1]],
                                  axis=0),
                  jnp.concatenate([k[2 * p], k[2 * p + 1]], axis=0)) for p in pairs]
    a = [jnp.where(strict, jnp.where(left, x[:c], x[2 * c:3 * c]) * gm, 0.0) for x, gm in zip(kq, gam)]
    attn = [_block_diag((jnp.where(left, x[c:2 * c], x[3 * c:]) * gm).astype(BF16), left)
            for x, gm in zip(kq, gam)]
    tinv = _unit_triangular_inverses(a, eye, ri, ci, left)
    rhs = [jnp.concatenate([v_ref[rows(ch), cols(h)].astype(F32) * b, x * e], axis=1)
           for (ch, h), b, x, e in zip(units, beta, kb, eg)]
    uw2 = _dot_hi_each([tuple(_block_diag(part, left) for part in _split_bf16(t)) for t in tinv],
                       [_split_bf16(jnp.concatenate([rhs[2 * p], rhs[2 * p + 1]], axis=0)) for p in pairs])
    uw = [uw2[u // 2][(u % 2) * c:(u % 2 + 1) * c] for u in range(len(units))]
    tail = [jnp.exp(gl - gc) for gl, gc in zip(g_last, g_col)]
    dec = [jnp.exp(gl) for gl in g_last]

    for ci_, ch in enumerate(order):
        idx = [ci_ * DN_HEADS + h for h in heads]
        s = [s_ref[h] for h in heads]
        sb = [x.astype(BF16) for x in s]
        ws = [_dot(jnp.concatenate([uw[i][:, DN_D:].astype(BF16), q[i]], axis=0), sb[h])
              for h, i in zip(heads, idx)]
        v_new = [uw[i][:, :DN_D] - x[:c] for i, x in zip(idx, ws)]
        av = [_dot(attn[idx[2 * x] // 2],
                   jnp.concatenate([v_new[2 * x].astype(BF16), v_new[2 * x + 1].astype(BF16)], axis=0))
              for x in half_heads]
        o = [ws[h][c:] * eg[idx[h]] + av[h // 2][(h % 2) * c:(h % 2 + 1) * c] for h in heads]
        for h, i in zip(heads, idx):
            s_ref[h] = s[h] * dec[i] + _dot_tn(k[i], (v_new[h] * tail[i]).astype(BF16))
        for h in heads:
            if reverse:
                o_ref[rows(ch), cols(h)] = o[h]
            else:
                oo = o[h] + ob_ref[rows(ch), cols(h)]
                on = (oo * lax.rsqrt(jnp.mean(oo * oo, axis=-1, keepdims=True) + EPS)) * ng_ref[...]
                z = z_ref[rows(ch), cols(h)].astype(F32)
                o_ref[rows(ch), cols(h)] = (on * jax.nn.silu(z)).astype(BF16)


def _deltanet(proj, gate_cols, qkv, alog_row, dtb_row, norm_g, groups):
    t = proj.shape[0]
    tl = min(TL_DN, groups[0][1])
    nt = t // tl
    z_blk = (P_RET + 3 * D_HALF) // D_HALF
    row = pl.BlockSpec((1, LANES), lambda j: (0, 0))
    state = pltpu.VMEM((DN_HEADS, DN_D, DN_D), F32)

    def common(reverse):
        tile = (lambda j: nt - 1 - j) if reverse else (lambda j: j)
        sec = lambda s: pl.BlockSpec((tl, D_HALF), lambda j: (tile(j), s))
        gates = pl.BlockSpec((tl, LANES), lambda j: (tile(j), 0))
        return sec, gates

    sec, gates = common(True)
    ob = pl.pallas_call(
        functools.partial(_dn_kernel, tl=tl, nt=nt, groups=groups, reverse=True),
        out_shape=jax.ShapeDtypeStruct((t, D_HALF), F32),
        grid=(nt,),
        in_specs=[sec(0), sec(1), sec(2), gates, row, row],
        out_specs=sec(0),
        scratch_shapes=[state],
        compiler_params=_params("arbitrary"),
        name="dn_bwd",
    )(qkv, qkv, qkv, gate_cols, alog_row, dtb_row)

    sec, gates = common(False)
    return pl.pallas_call(
        functools.partial(_dn_kernel, tl=tl, nt=nt, groups=groups, reverse=False),
        out_shape=jax.ShapeDtypeStruct((t, D_HALF), BF16),
        grid=(nt,),
        in_specs=[sec(0), sec(1), sec(2), gates, row, row, sec(z_blk), sec(0), row],
        out_specs=sec(0),
        scratch_shapes=[state],
        compiler_params=_params("arbitrary"),
        name="dn_fwd",
    )(qkv, qkv, qkv, gate_cols, alog_row, dtb_row, proj, ob, norm_g)


def _gate_row(f_vals, b_vals):
    row = jnp.zeros((LANES,), F32)
    row = row.at[2 * DN_HEADS:3 * DN_HEADS].set(f_vals.astype(F32))
    row = row.at[3 * DN_HEADS:4 * DN_HEADS].set(b_vals.astype(F32))
    return row.reshape(1, LANES)


def _trunk(xs, c, p, groups):
    d = xs[0].shape[1]
    spans = []
    for nb, ln in groups:
        spans.append((sum(r for _, r in spans), nb * ln))
    n_seq = c.shape[0]
    c_pad = jnp.zeros((SEQ_PAD, d), F32).at[:n_seq].set(c)
    mods = _ada(c_pad, p['w_ada'], p['b_ada'])
    mods = mods.reshape(DEPTH, SEQ_PAD, N_MOD, 1, d).transpose(0, 2, 1, 3, 4)
    fin = _ada(c_pad, p['w_ada_final'][None], p['b_ada_final'][None])
    fin = fin.reshape(SEQ_PAD, 2, 1, d).transpose(1, 0, 2, 3)

    max_len = max(ln for _, ln in groups)
    cos, sin = _rope_tables(max_len, min(TL_RET, groups[0][1]))

    row = lambda a: a.reshape(1, -1)
    for layer in range(DEPTH):
        sh1, sc1, g1, sh2, sc2, g2, sh3, sc3, g3 = [mods[layer, jm] for jm in range(N_MOD)]
        ffn1 = functools.partial(_ffn, gain=row(p['norm_ffn1'][layer]), shift=sh1, scale=sc1, gate=g1,
                                 w_in=p['w_ffn1_in'], w_out=p['w_ffn1_out'], layer=layer, groups=groups)
        if layer == 0:
            x = None
            for xg, span in zip(xs, spans):
                x = ffn1(xg, span=span, x_is_span=True, carry=x)
        else:
            x = ffn1(x)
        idx = layer // 2
        gain = row(p['norm_mix'][layer])
        if layer % 2 == 0:
            proj = _proj(x, gain, sh2, sc2, p['w_in_even'], idx, P_EVEN, groups)
            y = _even_mix(proj, p['pool_w'][idx], row(p['pool_scale'][idx]), row(p['sgu_norm'][idx]),
                          p['sgu_w'][idx], p['sgu_b'][idx][..., None], groups)
            x = _outproj(x, y, y, 0, 1, g2, p['w_out_even'], idx, groups)
        else:
            proj, gate_cols = _proj(x, gain, sh2, sc2, p['w_in_odd'], idx, P_ODD_MAIN, groups,
                                    odd_extras=(p['w_in_odd_gates'], cos, sin))
            yc = _retention(proj, p['ret_decay_f'][idx], p['ret_decay_b'][idx],
                            row(p['ret_norm'][idx]), groups)
            qkv = _dn_prep(proj, p['dn_conv'][idx], groups)
            yd = _deltanet(proj, gate_cols, qkv,
                           _gate_row(p['dn_a_log_f'][idx], p['dn_a_log_b'][idx]),
                           _gate_row(p['dn_dt_bias_f'][idx], p['dn_dt_bias_b'][idx]),
                           row(p['dn_norm'][idx]), groups)
            x = _outproj(x, yc, yd, 0, 0, g2, p['w_out_odd'], idx, groups)
        ffn2 = functools.partial(_ffn, gain=row(p['norm_ffn2'][layer]), shift=sh3, scale=sc3, gate=g3,
                                 w_in=p['w_ffn2_in'], w_out=p['w_ffn2_out'], layer=layer, groups=groups)
        if layer < DEPTH - 1:
            x = ffn2(x)
    closing = (row(p['norm_final']), fin[0], fin[1])
    return tuple(ffn2(x, span=span, out_is_span=True, final=closing) for span in spans)


def _prepare(p):
    q = dict(p)
    for name in ('w_ffn1_in', 'w_ffn1_out', 'w_ffn2_in', 'w_ffn2_out', 'w_in_even', 'w_out_even',
                 'pool_w', 'sgu_w', 'w_in_odd', 'w_out_odd'):
        q[name] = p[name].astype(BF16)
    gates = q['w_in_odd'][:, :, P_ODD_MAIN:]
    q['w_in_odd_gates'] = jnp.pad(gates, ((0, 0), (0, 0), (0, LANES - gates.shape[-1])))
    return q


def kernel(x_prompt, x_sample, c_prompt, c_sample, w_ada, b_ada, norm_ffn1, w_ffn1_in, w_ffn1_out, norm_mix, norm_ffn2, w_ffn2_in, w_ffn2_out, w_in_even, w_out_even, pool_w, pool_scale, sgu_norm, sgu_w, sgu_b, w_in_odd, w_out_odd, ret_decay_f, ret_decay_b, ret_norm, dn_conv, dn_a_log_f, dn_a_log_b, dn_dt_bias_f, dn_dt_bias_b, dn_norm, norm_final, w_ada_final, b_ada_final):
    p = _prepare({
        'w_ada': w_ada, 'b_ada': b_ada, 'norm_ffn1': norm_ffn1, 'w_ffn1_in': w_ffn1_in,
        'w_ffn1_out': w_ffn1_out, 'norm_mix': norm_mix, 'norm_ffn2': norm_ffn2,
        'w_ffn2_in': w_ffn2_in, 'w_ffn2_out': w_ffn2_out, 'w_in_even': w_in_even,
        'w_out_even': w_out_even, 'pool_w': pool_w, 'pool_scale': pool_scale,
        'sgu_norm': sgu_norm, 'sgu_w': sgu_w, 'sgu_b': sgu_b, 'w_in_odd': w_in_odd,
        'w_out_odd': w_out_odd, 'ret_decay_f': ret_decay_f, 'ret_decay_b': ret_decay_b,
        'ret_norm': ret_norm, 'dn_conv': dn_conv, 'dn_a_log_f': dn_a_log_f,
        'dn_a_log_b': dn_a_log_b, 'dn_dt_bias_f': dn_dt_bias_f, 'dn_dt_bias_b': dn_dt_bias_b,
        'dn_norm': dn_norm, 'norm_final': norm_final, 'w_ada_final': w_ada_final,
        'b_ada_final': b_ada_final,
    })
    bp, lp, d = x_prompt.shape
    bs, ls, _ = x_sample.shape
    groups = ((bp, lp), (bs, ls))
    xs = (x_prompt.reshape(bp * lp, d), x_sample.reshape(bs * ls, d))
    c = jnp.concatenate([c_prompt, c_sample], axis=0)
    y_prompt, y_sample = _trunk(xs, c, p, groups)
    return (y_prompt.reshape(bp, lp, d), y_sample.reshape(bs, ls, d))
```

```python
import functools

import jax
import jax.numpy as jnp
from jax import lax
from jax.experimental import pallas as pl
from jax.experimental.pallas import tpu as pltpu

F32 = jnp.float32
BF16 = jnp.bfloat16

D_MODEL = 2048
DEPTH = 4
D_HALF = D_MODEL // 2
POOL_WINDOWS = (2, 4, 8, 16)
POOL_GROUP = D_HALF // len(POOL_WINDOWS)
SGU_CHUNK = 128
SGU_HEAD = 128
SGU_GROUPS = D_HALF // SGU_HEAD
RET_HEADS = 4
RET_D = D_HALF // RET_HEADS
RET_CHUNK = 128
ROPE_BASE = 10000.0
DN_HEADS = 8
DN_D = D_HALF // DN_HEADS
DN_CONV = 4
DN_CHUNK = 64
D_FF = 5632
N_MOD = 9
EPS = 1e-6
P_EVEN = 3 * D_HALF
P_RET = 4 * D_HALF
P_ODD_MAIN = P_RET + 4 * D_HALF
P_ODD = P_ODD_MAIN + 4 * DN_HEADS

LANES = 128
SUBLANES = 8
HALO = 2 * SUBLANES
VMEM_LIMIT = 56 * 1024 * 1024

TM_FFN = 1024
TF_FFN = 512
TM_PROJ = 1024
TN_PROJ = 1024
MOD_ROWS = 16
MOD_UNROLL = 8
TM_OUT = 512
TL_EVEN = 256
TL_RET = 512
TL_DN = 512
TL_PREP = 512
PREP_ROWS = 64
TN_ADA = 1024
SEQ_PAD = 16


def _seq_info(row0, groups):
    seq = start = length = None
    t0 = s0 = 0
    for gi, (nb, ln) in enumerate(groups):
        rel = row0 - t0
        q = rel // ln
        if gi == 0:
            seq, start, length = q, q * ln, ln
        else:
            here = row0 >= t0
            seq = jnp.where(here, s0 + q, seq)
            start = jnp.where(here, t0 + q * ln, start)
            length = jnp.where(here, ln, length)
        t0 += nb * ln
        s0 += nb
    return seq, start, length


def _params(*sem):
    return pltpu.CompilerParams(dimension_semantics=sem, vmem_limit_bytes=VMEM_LIMIT)


def _dot(a, b):
    return jnp.dot(a, b, preferred_element_type=F32)


def _dot_nt(a, b):
    return lax.dot_general(a, b, (((1,), (1,)), ((), ())), preferred_element_type=F32)


def _dot_tn(a, b):
    return lax.dot_general(a, b, (((0,), (0,)), ((), ())), preferred_element_type=F32)


def _split_bf16(a):
    hi = a.astype(BF16)
    lo = (a - hi.astype(F32)).astype(BF16)
    return hi, lo


def _ada_kernel(c_ref, w_ref, b_ref, o_ref):
    c = c_ref[...]
    act = jax.nn.silu(c).astype(BF16)
    o_ref[...] = _dot(act, w_ref[...].astype(BF16)) + b_ref[...]


def _ada(c_pad, w, b):
    ly, d, n = w.shape
    s = c_pad.shape[0]
    tn = min(TN_ADA, n)
    return pl.pallas_call(
        _ada_kernel,
        out_shape=jax.ShapeDtypeStruct((ly, s, n), F32),
        grid=(ly, n // tn),
        in_specs=[
            pl.BlockSpec((s, d), lambda l, j: (0, 0)),
            pl.BlockSpec((None, d, tn), lambda l, j: (l, 0, j)),
            pl.BlockSpec((None, 1, tn), lambda l, j: (l, 0, j)),
        ],
        out_specs=pl.BlockSpec((None, s, tn), lambda l, j: (l, 0, j)),
        compiler_params=_params("parallel", "parallel"),
        name="ada_rows",
    )(c_pad, w, b.reshape(ly, 1, n))


def _modulate_into(h_ref, x_ref, gain_ref, sh_ref, sc_ref, zero_ref=None):
    tm = x_ref.shape[0]
    amp = gain_ref[...] * (1.0 + sc_ref[...])
    shift = sh_ref[...]

    def body(r, carry):
        rows = pl.ds(pl.multiple_of(r * MOD_ROWS, MOD_ROWS), MOD_ROWS)
        x = x_ref[rows, :]
        ms = jnp.mean(x * x, axis=-1, keepdims=True)
        h_ref[rows, :] = ((x * lax.rsqrt(ms + EPS)) * amp + shift).astype(BF16)
        if zero_ref is not None:
            zero_ref[rows, :] = jnp.zeros((MOD_ROWS, zero_ref.shape[1]), zero_ref.dtype)
        return carry

    lax.fori_loop(0, tm // MOD_ROWS, body, 0, unroll=MOD_UNROLL)


def _ffn_kernel(*refs, nf, carried, final):
    x_ref, gain_ref, sh_ref, sc_ref, gt_ref, wg_ref, wu_ref, wo_ref = refs[:8]
    rest = refs[8:]
    if final:
        fgain_ref, fsh_ref, fsc_ref = rest[:3]
        rest = rest[3:]
    if carried:
        rest = rest[1:]
    o_ref, h_ref = rest[:2]
    if final:
        inv_ref = rest[2]
    f = pl.program_id(1)

    @pl.when(f == 0)
    def _():
        _modulate_into(h_ref, x_ref, gain_ref, sh_ref, sc_ref, zero_ref=o_ref)

    h = h_ref[...]
    g = _dot(h, wg_ref[...])
    u = _dot(h, wu_ref[...])
    a = (jax.nn.silu(g) * u).astype(BF16)
    o_ref[...] += _dot(a, wo_ref[...])

    @pl.when(f == nf - 1)
    def _():
        half_gate = 0.5 * gt_ref[...]
        if not final:
            o_ref[...] = x_ref[...] + half_gate * o_ref[...]
            return
        amp = fgain_ref[...] * (1.0 + fsc_ref[...])
        shift = fsh_ref[...]
        steps = x_ref.shape[0] // MOD_ROWS

        def residual(r):
            rows = pl.ds(pl.multiple_of(r * MOD_ROWS, MOD_ROWS), MOD_ROWS)
            return rows, x_ref[rows, :] + half_gate * o_ref[rows, :]

        def stats(r, carry):
            rows, y = residual(r)
            inv_ref[rows, :] = lax.rsqrt(jnp.mean(y * y, axis=-1, keepdims=True) + EPS)
            return carry

        def apply(r, carry):
            rows, y = residual(r)
            o_ref[rows, :] = (y * inv_ref[rows, :]) * amp + shift
            return carry

        lax.fori_loop(0, steps, stats, 0, unroll=MOD_UNROLL)
        lax.fori_loop(0, steps, apply, 0, unroll=MOD_UNROLL)


def _row_spec(groups, tm, d, row_off=0):
    return pl.BlockSpec((None, 1, d), lambda i, j: (_seq_info(i * tm + row_off, groups)[0], 0, 0))


def _ffn(x, gain, shift, scale, gate, w_in, w_out, layer, groups, *, span=None, x_is_span=False,
         out_is_span=False, carry=None, final=None):
    d = x.shape[1]
    t = sum(nb * ln for nb, ln in groups)
    row_off, n_rows = span if span is not None else (0, t)
    ff = w_out.shape[1]
    tm = min(TM_FFN, groups[0][1])
    tf = min(TF_FFN, ff)
    nf = ff // tf
    tile_off = row_off // tm
    x_off = 0 if x_is_span else tile_off
    o_off = 0 if out_is_span else tile_off
    row = _row_spec(groups, tm, d, row_off)
    in_specs = [
        pl.BlockSpec((tm, d), lambda i, f: (i + x_off, 0)),
        pl.BlockSpec((1, d), lambda i, f: (0, 0)),
        row, row, row,
        pl.BlockSpec((None, d, tf), lambda i, f: (layer, 0, f)),
        pl.BlockSpec((None, d, tf), lambda i, f: (layer, 0, nf + f)),
        pl.BlockSpec((None, tf, d), lambda i, f: (layer, f, 0)),
    ]
    args = [x, gain, shift, scale, gate, w_in, w_in, w_out]
    if final is not None:
        in_specs += [pl.BlockSpec((1, d), lambda i, f: (0, 0)), row, row]
        args += list(final)
    aliases = {}
    if carry is not None:
        in_specs.append(pl.BlockSpec(memory_space=pl.ANY))
        aliases = {len(args): 0}
        args.append(carry)
    return pl.pallas_call(
        functools.partial(_ffn_kernel, nf=nf, carried=carry is not None, final=final is not None),
        out_shape=jax.ShapeDtypeStruct((n_rows if out_is_span else t, d), F32),
        grid=(n_rows // tm, nf),
        in_specs=in_specs,
        out_specs=pl.BlockSpec((tm, d), lambda i, f: (i + o_off, 0)),
        scratch_shapes=[pltpu.VMEM((tm, d), BF16)] + ([pltpu.VMEM((tm, 1), F32)] if final is not None else []),
        input_output_aliases=aliases,
        compiler_params=_params("parallel", "arbitrary"),
        name="ffn",
    )(*args)


def _proj_kernel(*refs, odd):
    if odd:
        x_ref, gain_ref, sh_ref, sc_ref, w_ref, wn_ref, cos_ref, sin_ref, o_ref, on_ref, h_ref = refs
    else:
        x_ref, gain_ref, sh_ref, sc_ref, w_ref, o_ref, h_ref = refs
    j = pl.program_id(1)

    @pl.when(j == 0)
    def _():
        _modulate_into(h_ref, x_ref, gain_ref, sh_ref, sc_ref)
        if odd:
            on_ref[...] = _dot(h_ref[...], wn_ref[...])

    y = _dot(h_ref[...], w_ref[...])
    if not odd:
        o_ref[...] = y.astype(BF16)
        return

    tn = o_ref.shape[1]
    rope_tiles = 2 * D_HALF // tn
    half = RET_D // 2
    rotate = j < rope_tiles
    cos = jnp.where(rotate, cos_ref[...], 1.0)
    sin = jnp.where(rotate, sin_ref[...], 0.0)
    k_scale = jnp.where(rotate & (j >= rope_tiles // 2), RET_D ** -0.5, 1.0).astype(F32)
    for hd in range(tn // RET_D):
        x1 = y[:, hd * RET_D:hd * RET_D + half]
        x2 = y[:, hd * RET_D + half:(hd + 1) * RET_D]
        o_ref[:, hd * RET_D:hd * RET_D + half] = ((x1 * cos - x2 * sin) * k_scale).astype(BF16)
        o_ref[:, hd * RET_D + half:(hd + 1) * RET_D] = ((x1 * sin + x2 * cos) * k_scale).astype(BF16)


def _proj(x, gain, shift, scale, w, idx, n, groups, odd_extras=None):
    t, d = x.shape
    tm = min(TM_PROJ, groups[0][1])
    tn = TN_PROJ
    row = _row_spec(groups, tm, d)
    odd = odd_extras is not None
    in_specs = [
        pl.BlockSpec((tm, d), lambda i, j: (i, 0)),
        pl.BlockSpec((1, d), lambda i, j: (0, 0)),
        row, row,
        pl.BlockSpec((None, d, tn), lambda i, j: (idx, 0, j)),
    ]
    out_shape = jax.ShapeDtypeStruct((t, n), BF16)
    out_specs = pl.BlockSpec((tm, tn), lambda i, j: (i, j))
    args = (x, gain, shift, scale, w)
    if odd:
        half = RET_D // 2

        def pos_block(i, j):
            row0 = i * tm
            return ((row0 - _seq_info(row0, groups)[1]) // tm, 0)

        tab = pl.BlockSpec((tm, half), pos_block)
        in_specs += [pl.BlockSpec((None, d, LANES), lambda i, j: (idx, 0, 0)), tab, tab]
        out_shape = (out_shape, jax.ShapeDtypeStruct((t, LANES), F32))
        out_specs = (out_specs, pl.BlockSpec((tm, LANES), lambda i, j: (i, 0)))
        args = args + tuple(odd_extras)
    return pl.pallas_call(
        functools.partial(_proj_kernel, odd=odd),
        out_shape=out_shape,
        grid=(t // tm, n // tn),
        in_specs=in_specs,
        out_specs=out_specs,
        scratch_shapes=[pltpu.VMEM((tm, d), BF16)],
        compiler_params=_params("parallel", "arbitrary"),
        name="mix_proj",
    )(*args)


def _outproj_kernel(x_ref, ya_ref, yb_ref, gt_ref, wa_ref, wb_ref, o_ref):
    y = _dot(ya_ref[...], wa_ref[...]) + _dot(yb_ref[...], wb_ref[...])
    o_ref[...] = x_ref[...] + gt_ref[...] * y


def _outproj(x, ya, yb, ca, cb, gate, w, idx, groups):
    t, d = x.shape
    dh = d // 2
    tm = min(TM_OUT, groups[0][1])
    row = _row_spec(groups, tm, d)
    return pl.pallas_call(
        _outproj_kernel,
        out_shape=jax.ShapeDtypeStruct((t, d), F32),
        grid=(t // tm, 1),
        in_specs=[
            pl.BlockSpec((tm, d), lambda i, j: (i, 0)),
            pl.BlockSpec((tm, dh), lambda i, j: (i, ca)),
            pl.BlockSpec((tm, dh), lambda i, j: (i, cb)),
            row,
            pl.BlockSpec((None, dh, d), lambda i, j: (idx, 0, 0)),
            pl.BlockSpec((None, dh, d), lambda i, j: (idx, 1, 0)),
        ],
        out_specs=pl.BlockSpec((tm, d), lambda i, j: (i, 0)),
        compiler_params=_params("parallel", "arbitrary"),
        name="mix_out",
    )(x, ya, yb, gate, w, w)


def _halo_specs(tl, width, nrows, col_of):
    per = tl // HALO
    last = nrows // HALO - 1
    prev = pl.BlockSpec((HALO, width), lambda i, *r: (jnp.maximum(i * per - 1, 0), col_of(i, *r)))
    nxt = pl.BlockSpec((HALO, width), lambda i, *r: (jnp.minimum((i + 1) * per, last), col_of(i, *r)))
    return prev, nxt


def _fill_ext(ext_ref, x_ref, prev_ref, next_ref, first, last, tl):
    dt = ext_ref.dtype
    ext_ref[HALO:HALO + tl, :] = x_ref[...].astype(dt)
    ext_ref[0:HALO, :] = jnp.where(first, jnp.zeros_like(prev_ref), prev_ref[...]).astype(dt)
    ext_ref[HALO + tl:2 * HALO + tl, :] = jnp.where(last, jnp.zeros_like(next_ref), next_ref[...]).astype(dt)


def _row_window(tl, lo, hi):
    r = lax.broadcasted_iota(jnp.int32, (tl, tl + 2 * HALO), 0) + HALO
    col = lax.broadcasted_iota(jnp.int32, (tl, tl + 2 * HALO), 1)
    return ((col >= r + lo) & (col < r + hi)).astype(BF16)


def _even_kernel(xa_ref, prev_ref, next_ref, u_ref, v_ref, pw_ref, ps_ref, ng_ref, sw_ref, sb_ref,
                 o_ref, ext_ref, vn_ref, win_ref, *, tl, groups):
    @pl.when(pl.program_id(0) == 0)
    def _():
        for gi, w in enumerate(POOL_WINDOWS):
            win_ref[gi] = _row_window(tl, -(w // 2), w - w // 2)

    row0 = pl.program_id(0) * tl
    _, sstart, slen = _seq_info(row0, groups)
    pos0 = row0 - sstart
    _fill_ext(ext_ref, xa_ref, prev_ref, next_ref, pos0 == 0, pos0 + tl == slen, tl)

    t = pos0 + lax.broadcasted_iota(jnp.int32, (tl, 1), 0)
    for gi, w in enumerate(POOL_WINDOWS):
        c0 = gi * POOL_GROUP
        cols = slice(c0, c0 + POOL_GROUP)
        s = _dot(win_ref[gi], ext_ref[:, cols])
        lo = jnp.clip(t - w // 2, 0, slen)
        hi = jnp.clip(t + (w - w // 2), 0, slen)
        cnt = (hi - lo).astype(F32)
        pooled = (s / cnt - xa_ref[:, cols].astype(F32)).astype(BF16)
        ya = _dot(pooled, pw_ref[gi]) * ps_ref[:, cols]
        o_ref[:, cols] = ya.astype(BF16)

    v = jax.nn.gelu(v_ref[...].astype(F32))
    vms = jnp.mean(v * v, axis=-1, keepdims=True)
    vn_ref[...] = ((v * lax.rsqrt(vms + EPS)) * ng_ref[...]).astype(BF16)
    for n in range(tl // SGU_CHUNK):
        rows = slice(n * SGU_CHUNK, (n + 1) * SGU_CHUNK)
        for g in range(SGU_GROUPS):
            cols = slice(g * SGU_HEAD, (g + 1) * SGU_HEAD)
            mixed = _dot(sw_ref[g], vn_ref[rows, cols]) + sb_ref[g]
            u = jax.nn.gelu(u_ref[rows, cols].astype(F32))
            o_ref[rows, D_HALF + g * SGU_HEAD:D_HALF + (g + 1) * SGU_HEAD] = (u * mixed).astype(BF16)


def _even_mix(proj, pool_w, pool_scale, sgu_norm, sgu_w, sgu_b, groups):
    t = proj.shape[0]
    tl = min(TL_EVEN, groups[0][1])
    prev, nxt = _halo_specs(tl, D_HALF, t, lambda i: 0)
    const2 = lambda i: (0, 0)
    const3 = lambda i: (0, 0, 0)
    return pl.pallas_call(
        functools.partial(_even_kernel, tl=tl, groups=groups),
        out_shape=jax.ShapeDtypeStruct((t, 2 * D_HALF), BF16),
        grid=(t // tl,),
        in_specs=[
            pl.BlockSpec((tl, D_HALF), lambda i: (i, 0)),
            prev, nxt,
            pl.BlockSpec((tl, D_HALF), lambda i: (i, 1)),
            pl.BlockSpec((tl, D_HALF), lambda i: (i, 2)),
            pl.BlockSpec(pool_w.shape, const3),
            pl.BlockSpec((1, D_HALF), const2),
            pl.BlockSpec((1, D_HALF), const2),
            pl.BlockSpec(sgu_w.shape, const3),
            pl.BlockSpec(sgu_b.shape, const3),
        ],
        out_specs=pl.BlockSpec((tl, 2 * D_HALF), lambda i: (i, 0)),
        scratch_shapes=[pltpu.VMEM((tl + 2 * HALO, D_HALF), BF16), pltpu.VMEM((tl, D_HALF), BF16),
                        pltpu.VMEM((len(POOL_WINDOWS), tl, tl + 2 * HALO), BF16)],
        compiler_params=_params("arbitrary"),
        name="even_mix",
    )(proj, proj, proj, proj, proj, pool_w, pool_scale, sgu_norm, sgu_w, sgu_b)


def _rope_kernel(inv_ref, cos_ref, sin_ref, *, tl):
    pos = (pl.program_id(0) * tl + lax.broadcasted_iota(jnp.int32, (tl, 1), 0)).astype(F32)
    ang = pos * inv_ref[...]
    cos_ref[...] = jnp.cos(ang)
    sin_ref[...] = jnp.sin(ang)


def _rope_tables(max_len, tl):
    half = RET_D // 2
    inv = (1.0 / (ROPE_BASE ** jnp.linspace(0.0, 1.0, half, dtype=F32))).reshape(1, half)
    shp = jax.ShapeDtypeStruct((max_len, half), F32)
    return pl.pallas_call(
        functools.partial(_rope_kernel, tl=tl),
        out_shape=(shp, shp),
        grid=(max_len // tl,),
        in_specs=[pl.BlockSpec((1, half), lambda i: (0, 0))],
        out_specs=(pl.BlockSpec((tl, half), lambda i: (i, 0)), pl.BlockSpec((tl, half), lambda i: (i, 0))),
        compiler_params=_params("parallel"),
        name="rope_table",
    )(inv)


def _ret_kernel(*refs, tl, nt, groups, reverse):
    if reverse:
        q_ref, k_ref, v_ref, dec_ref, o_ref, s_ref = refs
    else:
        q_ref, k_ref, v_ref, dec_ref, decb_ref, g_ref, ob_ref, ng_ref, o_ref, s_ref = refs
    j = pl.program_id(0)
    it = nt - 1 - j if reverse else j
    row0 = it * tl
    _, sstart, slen = _seq_info(row0, groups)
    pos0 = row0 - sstart
    reset = (pos0 + tl == slen) if reverse else (pos0 == 0)

    @pl.when(reset)
    def _():
        s_ref[...] = jnp.zeros_like(s_ref)

    c = RET_CHUNK
    heads = range(RET_HEADS)
    idx = lax.broadcasted_iota(jnp.int32, (c, 1), 0).astype(F32)
    lg = [jnp.log1p(-jnp.exp2(-dec_ref[h])) for h in heads]
    if reverse:
        q_dec = [jnp.exp(x * (c - idx)) for x in lg]
        k_dec = [jnp.exp(x * idx) for x in lg]
    else:
        q_dec = [jnp.exp(x * (idx + 1.0)) for x in lg]
        k_dec = [jnp.exp(x * (c - 1.0 - idx)) for x in lg]
        lgb = [jnp.log1p(-jnp.exp2(-decb_ref[h])) for h in heads]
        ri = lax.broadcasted_iota(jnp.int32, (c, c), 0)
        ci = lax.broadcasted_iota(jnp.int32, (c, c), 1)
        rel = (ri - ci).astype(F32)
        dmat = [jnp.where(rel >= 0, jnp.exp(x * jnp.maximum(rel, 0.0)), 0.0)
                + jnp.where(rel <= 0, jnp.exp(y * jnp.maximum(-rel, 0.0)), 0.0) for x, y in zip(lg, lgb)]
    chunk_dec = [jnp.exp(x * float(c)) for x in lg]

    nc = tl // c
    order = list(range(nc - 1, -1, -1) if reverse else range(nc))
    units = [(ch, h) for ch in order for h in heads]

    def rows(ch):
        return slice(ch * c, (ch + 1) * c)

    def cols(h):
        return slice(h * RET_D, (h + 1) * RET_D)

    q = [q_ref[rows(ch), cols(h)] for ch, h in units]
    v = [v_ref[rows(ch), cols(h)] for ch, h in units]
    kv = [_dot_tn(k_ref[rows(ch), cols(h)], (x.astype(F32) * k_dec[h]).astype(BF16))
          for (ch, h), x in zip(units, v)]
    if not reverse:
        scores = [(_dot_nt(x, k_ref[rows(ch), cols(h)]) * dmat[h]).astype(BF16) for (ch, h), x in zip(units, q)]
        intra = [_dot(x, y) for x, y in zip(scores, v)]

    for ci_, ch in enumerate(order):
        s = [s_ref[h] for h in heads]
        inter = [_dot(q[ci_ * RET_HEADS + h], s[h].astype(BF16)) * q_dec[h] for h in heads]
        for h in heads:
            s_ref[h] = s[h] * chunk_dec[h] + kv[ci_ * RET_HEADS + h]
        for h in heads:
            if reverse:
                o_ref[rows(ch), cols(h)] = inter[h]
            else:
                o = intra[ci_ * RET_HEADS + h] + inter[h] + ob_ref[rows(ch), cols(h)]
                mu = jnp.mean(o, axis=-1, keepdims=True)
                var = jnp.mean(jnp.square(o - mu), axis=-1, keepdims=True)
                on = ((o - mu) * lax.rsqrt(var + EPS)) * ng_ref[:, cols(h)]
                gate = jax.nn.silu(g_ref[rows(ch), cols(h)].astype(F32))
                o_ref[rows(ch), cols(h)] = (gate * on).astype(BF16)


def _retention(proj, decay_f, decay_b, norm_g, groups):
    t = proj.shape[0]
    tl = min(TL_RET, groups[0][1])
    nt = t // tl
    dec = pl.BlockSpec((RET_HEADS, 1, 1), lambda j: (0, 0, 0))
    state = pltpu.VMEM((RET_HEADS, RET_D, RET_D), F32)

    def section(reverse):
        tile = (lambda j: nt - 1 - j) if reverse else (lambda j: j)
        return lambda s: pl.BlockSpec((tl, D_HALF), lambda j: (tile(j), s))

    sec = section(True)
    ob = pl.pallas_call(
        functools.partial(_ret_kernel, tl=tl, nt=nt, groups=groups, reverse=True),
        out_shape=jax.ShapeDtypeStruct((t, D_HALF), F32),
        grid=(nt,),
        in_specs=[sec(0), sec(1), sec(2), dec],
        out_specs=sec(0),
        scratch_shapes=[state],
        compiler_params=_params("arbitrary"),
        name="ret_bwd",
    )(proj, proj, proj, decay_b.reshape(RET_HEADS, 1, 1))

    sec = section(False)
    return pl.pallas_call(
        functools.partial(_ret_kernel, tl=tl, nt=nt, groups=groups, reverse=False),
        out_shape=jax.ShapeDtypeStruct((t, D_HALF), BF16),
        grid=(nt,),
        in_specs=[sec(0), sec(1), sec(2), dec, dec, sec(3), sec(0),
                  pl.BlockSpec((1, D_HALF), lambda j: (0, 0))],
        out_specs=sec(0),
        scratch_shapes=[state],
        compiler_params=_params("arbitrary"),
        name="ret_fwd",
    )(proj, proj, proj, decay_f.reshape(RET_HEADS, 1, 1), decay_b.reshape(RET_HEADS, 1, 1),
      proj, ob, norm_g)


def _dnprep_kernel(x_ref, prev_ref, next_ref, w_ref, o_ref, ext_ref, *, tl, groups):
    row0 = pl.program_id(0) * tl
    part = pl.program_id(1)
    _, sstart, slen = _seq_info(row0, groups)
    pos0 = row0 - sstart
    _fill_ext(ext_ref, x_ref, prev_ref, next_ref, pos0 == 0, pos0 + tl == slen, tl)
    left = DN_CONV // 2
    q_scale = jnp.where(part == 0, DN_D ** -0.5, 1.0).astype(F32)
    rb = PREP_ROWS
    taps = [tap for tap in range(DN_CONV) if tap != left]
    r = lax.broadcasted_iota(jnp.int32, (rb, rb + 2 * HALO), 0) + HALO
    col = lax.broadcasted_iota(jnp.int32, (rb, rb + 2 * HALO), 1)
    select = jnp.concatenate([(col == r + (tap - left)).astype(BF16) for tap in taps], axis=0)
    for blk in range(tl // rb):
        rows = slice(blk * rb, (blk + 1) * rb)
        for h in range(DN_HEADS):
            cols = slice(h * DN_D, (h + 1) * DN_D)
            shifted = _dot(select, ext_ref[blk * rb:(blk + 1) * rb + 2 * HALO, cols])
            conv = x_ref[rows, cols].astype(F32) * w_ref[left:left + 1, cols]
            for n, tap in enumerate(taps):
                conv = conv + shifted[n * rb:(n + 1) * rb] * w_ref[tap:tap + 1, cols]
            y = jax.nn.silu(conv)
            inv_norm = lax.rsqrt(jnp.sum(y * y, axis=-1, keepdims=True) + EPS)
            o_ref[rows, cols] = (y * jnp.where(part < 2, inv_norm * q_scale, 1.0)).astype(BF16)


def _dn_prep(proj, conv_w, groups):
    t = proj.shape[0]
    tl = min(TL_PREP, groups[0][1])
    base = P_RET // D_HALF
    prev, nxt = _halo_specs(tl, D_HALF, t, lambda i, part: base + part)
    return pl.pallas_call(
        functools.partial(_dnprep_kernel, tl=tl, groups=groups),
        out_shape=jax.ShapeDtypeStruct((t, 3 * D_HALF), BF16),
        grid=(t // tl, 3),
        in_specs=[pl.BlockSpec((tl, D_HALF), lambda i, part: (i, base + part)), prev, nxt,
                  pl.BlockSpec((DN_CONV, D_HALF), lambda i, part: (0, part))],
        out_specs=pl.BlockSpec((tl, D_HALF), lambda i, part: (i, part)),
        scratch_shapes=[pltpu.VMEM((tl + 2 * HALO, D_HALF), BF16)],
        compiler_params=_params("parallel", "parallel"),
        name="dn_prep",
    )(proj, proj, proj, conv_w)


def _dot_hi_each(lhs_parts, rhs_parts):
    m = lhs_parts[0][0].shape[0]
    n = rhs_parts[0][0].shape[1]
    quads = [_dot(jnp.concatenate(a, axis=0), jnp.concatenate(b, axis=1)) for a, b in zip(lhs_parts, rhs_parts)]
    return [(x[:m, :n] + x[m:, :n]) + (x[:m, n:] + x[m:, n:]) for x in quads]


def _block_diag(y, left):
    zero = jnp.zeros_like(y)
    return jnp.concatenate([jnp.where(left, y, zero), jnp.where(left, zero, y)], axis=0)


def _pair_products(lhs, rhs, left):
    ls = [_split_bf16(x) for x in lhs]
    rs = [tuple(_block_diag(part, left) for part in _split_bf16(y)) for y in rhs]
    return _dot_hi_each(ls, rs)


def _unit_triangular_inverses(mats, eye, ri, ci, left):
    size = SUBLANES
    same = (ri // size) == (ci // size)
    ps = [jnp.where(same, a, 0.0) for a in mats]
    invs = [eye - d for d in ps]
    n = 2
    while n < size:
        ps = _pair_products(ps, ps, left)
        invs = [inv + x for inv, x in zip(invs, _pair_products(invs, ps, left))]
        n *= 2
    while size < DN_CHUNK:
        size *= 2
        merged = (ri // size) == (ci // size)
        es = [jnp.where(merged & ~same, a, 0.0) for a in mats]
        invs = [inv - x for inv, x in zip(invs, _pair_products(_pair_products(invs, es, left), invs, left))]
        same = merged
    return invs


def _dn_kernel(*refs, tl, nt, groups, reverse):
    if reverse:
        q_ref, k_ref, v_ref, gates_ref, alog_ref, dtb_ref, o_ref, s_ref = refs
    else:
        (q_ref, k_ref, v_ref, gates_ref, alog_ref, dtb_ref, z_ref, ob_ref, ng_ref,
         o_ref, s_ref) = refs
    j = pl.program_id(0)
    it = nt - 1 - j if reverse else j
    row0 = it * tl
    _, sstart, slen = _seq_info(row0, groups)
    pos0 = row0 - sstart
    reset = (pos0 + tl == slen) if reverse else (pos0 == 0)

    @pl.when(reset)
    def _():
        s_ref[...] = jnp.zeros_like(s_ref)

    gates = gates_ref[...]
    beta_all = jax.nn.sigmoid(gates)
    la_all = -jnp.exp(alog_ref[...]) * jax.nn.softplus(gates + dtb_ref[...])
    cb0 = DN_HEADS if reverse else 0
    ca0 = cb0 + 2 * DN_HEADS

    c = DN_CHUNK
    left = lax.broadcasted_iota(jnp.int32, (1, 2 * c), 1) < c
    ri = lax.broadcasted_iota(jnp.int32, (c, 2 * c), 0)
    ci = jnp.bitwise_and(lax.broadcasted_iota(jnp.int32, (c, 2 * c), 1), c - 1)
    eye = (ri == ci).astype(F32)
    incl = (ri <= ci) if reverse else (ri >= ci)
    strict = (ri < ci) if reverse else (ri > ci)
    incl_t = (ri >= ci) if reverse else (ri <= ci)
    last = 0 if reverse else c - 1

    nc = tl // c
    order = list(range(nc - 1, -1, -1) if reverse else range(nc))
    heads = range(DN_HEADS)
    half_heads = range(DN_HEADS // 2)
    units = [(ch, h) for ch in order for h in heads]
    pairs = range(len(units) // 2)

    def rows(ch):
        return slice(ch * c, (ch + 1) * c)

    def cols(h):
        return slice(h * DN_D, (h + 1) * DN_D)

    def paired(xs):
        return [jnp.where(left, xs[2 * p], xs[2 * p + 1]) for p in pairs]

    q = [q_ref[rows(ch), cols(h)] for ch, h in units]
    k = [k_ref[rows(ch), cols(h)] for ch, h in units]
    beta = [beta_all[rows(ch), cb0 + h:cb0 + h + 1] for ch, h in units]
    la = [la_all[rows(ch), ca0 + h:ca0 + h + 1] for ch, h in units]
    la2 = paired(la)
    la_row = [jnp.sum(eye * x, axis=0, keepdims=True) for x in la2]
    g_row = [jnp.sum(jnp.where(incl_t, x, 0.0), axis=0, keepdims=True) for x in la2]
    windowed = [jnp.where(incl, x, 0.0) for x in la_row]
    g_col = [jnp.sum(jnp.where(left == (u % 2 == 0), windowed[u // 2], 0.0), axis=1, keepdims=True)
             for u in range(len(units))]
    gam = [jnp.where(incl, jnp.exp(jnp.where(incl, gc - gr, 0.0)), 0.0) for gc, gr in zip(paired(g_col), g_row)]
    eg = [jnp.exp(gc) for gc in g_col]
    g_last = [gc[last:last + 1, :] for gc in g_col]
    kb = [x.astype(F32) * b for x, b in zip(k, beta)]
    kq = [_dot_nt(jnp.concatenate([kb[2 * p].astype(BF16), q[2 * p], kb[2 * p + 1].astype(BF16), q[2 * p + 1]],
                                  axis=0),
                  jnp.concatenate([k[2 * p], k[2 * p + 1]], axis=0)) for p in pairs]
    a = [jnp.where(strict, jnp.where(left, x[:c], x[2 * c:3 * c]) * gm, 0.0) for x, gm in zip(kq, gam)]
    attn = [_block_diag((jnp.where(left, x[c:2 * c], x[3 * c:]) * gm).astype(BF16), left)
            for x, gm in zip(kq, gam)]
    tinv = _unit_triangular_inverses(a, eye, ri, ci, left)
    rhs = [jnp.concatenate([v_ref[rows(ch), cols(h)].astype(F32) * b, x * e], axis=1)
           for (ch, h), b, x, e in zip(units, beta, kb, eg)]
    uw2 = _dot_hi_each([tuple(_block_diag(part, left) for part in _split_bf16(t)) for t in tinv],
                       [_split_bf16(jnp.concatenate([rhs[2 * p], rhs[2 * p + 1]], axis=0)) for p in pairs])
    uw = [uw2[u // 2][(u % 2) * c:(u % 2 + 1) * c] for u in range(len(units))]
    tail = [jnp.exp(gl - gc) for gl, gc in zip(g_last, g_col)]
    dec = [jnp.exp(gl) for gl in g_last]

    for ci_, ch in enumerate(order):
        idx = [ci_ * DN_HEADS + h for h in heads]
        s = [s_ref[h] for h in heads]
        sb = [x.astype(BF16) for x in s]
        ws = [_dot(jnp.concatenate([uw[i][:, DN_D:].astype(BF16), q[i]], axis=0), sb[h])
              for h, i in zip(heads, idx)]
        v_new = [uw[i][:, :DN_D] - x[:c] for i, x in zip(idx, ws)]
        av = [_dot(attn[idx[2 * x] // 2],
                   jnp.concatenate([v_new[2 * x].astype(BF16), v_new[2 * x + 1].astype(BF16)], axis=0))
              for x in half_heads]
        o = [ws[h][c:] * eg[idx[h]] + av[h // 2][(h % 2) * c:(h % 2 + 1) * c] for h in heads]
        for h, i in zip(heads, idx):
            s_ref[h] = s[h] * dec[i] + _dot_tn(k[i], (v_new[h] * tail[i]).astype(BF16))
        for h in heads:
            if reverse:
                o_ref[rows(ch), cols(h)] = o[h]
            else:
                oo = o[h] + ob_ref[rows(ch), cols(h)]
                on = (oo * lax.rsqrt(jnp.mean(oo * oo, axis=-1, keepdims=True) + EPS)) * ng_ref[...]
                z = z_ref[rows(ch), cols(h)].astype(F32)
                o_ref[rows(ch), cols(h)] = (on * jax.nn.silu(z)).astype(BF16)


def _deltanet(proj, gate_cols, qkv, alog_row, dtb_row, norm_g, groups):
    t = proj.shape[0]
    tl = min(TL_DN, groups[0][1])
    nt = t // tl
    z_blk = (P_RET + 3 * D_HALF) // D_HALF
    row = pl.BlockSpec((1, LANES), lambda j: (0, 0))
    state = pltpu.VMEM((DN_HEADS, DN_D, DN_D), F32)

    def common(reverse):
        tile = (lambda j: nt - 1 - j) if reverse else (lambda j: j)
        sec = lambda s: pl.BlockSpec((tl, D_HALF), lambda j: (tile(j), s))
        gates = pl.BlockSpec((tl, LANES), lambda j: (tile(j), 0))
        return sec, gates

    sec, gates = common(True)
    ob = pl.pallas_call(
        functools.partial(_dn_kernel, tl=tl, nt=nt, groups=groups, reverse=True),
        out_shape=jax.ShapeDtypeStruct((t, D_HALF), F32),
        grid=(nt,),
        in_specs=[sec(0), sec(1), sec(2), gates, row, row],
        out_specs=sec(0),
        scratch_shapes=[state],
        compiler_params=_params("arbitrary"),
        name="dn_bwd",
    )(qkv, qkv, qkv, gate_cols, alog_row, dtb_row)

    sec, gates = common(False)
    return pl.pallas_call(
        functools.partial(_dn_kernel, tl=tl, nt=nt, groups=groups, reverse=False),
        out_shape=jax.ShapeDtypeStruct((t, D_HALF), BF16),
        grid=(nt,),
        in_specs=[sec(0), sec(1), sec(2), gates, row, row, sec(z_blk), sec(0), row],
        out_specs=sec(0),
        scratch_shapes=[state],
        compiler_params=_params("arbitrary"),
        name="dn_fwd",
    )(qkv, qkv, qkv, gate_cols, alog_row, dtb_row, proj, ob, norm_g)


def _gate_row(f_vals, b_vals):
    row = jnp.zeros((LANES,), F32)
    row = row.at[2 * DN_HEADS:3 * DN_HEADS].set(f_vals.astype(F32))
    row = row.at[3 * DN_HEADS:4 * DN_HEADS].set(b_vals.astype(F32))
    return row.reshape(1, LANES)


def _trunk(xs, c, p, groups):
    d = xs[0].shape[1]
    spans = []
    for nb, ln in groups:
        spans.append((sum(r for _, r in spans), nb * ln))
    n_seq = c.shape[0]
    c_pad = jnp.zeros((SEQ_PAD, d), F32).at[:n_seq].set(c)
    mods = _ada(c_pad, p['w_ada'], p['b_ada'])
    mods = mods.reshape(DEPTH, SEQ_PAD, N_MOD, 1, d).transpose(0, 2, 1, 3, 4)
    fin = _ada(c_pad, p['w_ada_final'][None], p['b_ada_final'][None])
    fin = fin.reshape(SEQ_PAD, 2, 1, d).transpose(1, 0, 2, 3)

    max_len = max(ln for _, ln in groups)
    cos, sin = _rope_tables(max_len, min(TL_RET, groups[0][1]))

    row = lambda a: a.reshape(1, -1)
    for layer in range(DEPTH):
        sh1, sc1, g1, sh2, sc2, g2, sh3, sc3, g3 = [mods[layer, jm] for jm in range(N_MOD)]
        ffn1 = functools.partial(_ffn, gain=row(p['norm_ffn1'][layer]), shift=sh1, scale=sc1, gate=g1,
                                 w_in=p['w_ffn1_in'], w_out=p['w_ffn1_out'], layer=layer, groups=groups)
        if layer == 0:
            x = None
            for xg, span in zip(xs, spans):
                x = ffn1(xg, span=span, x_is_span=True, carry=x)
        else:
            x = ffn1(x)
        idx = layer // 2
        gain = row(p['norm_mix'][layer])
        if layer % 2 == 0:
            proj = _proj(x, gain, sh2, sc2, p['w_in_even'], idx, P_EVEN, groups)
            y = _even_mix(proj, p['pool_w'][idx], row(p['pool_scale'][idx]), row(p['sgu_norm'][idx]),
                          p['sgu_w'][idx], p['sgu_b'][idx][..., None], groups)
            x = _outproj(x, y, y, 0, 1, g2, p['w_out_even'], idx, groups)
        else:
            proj, gate_cols = _proj(x, gain, sh2, sc2, p['w_in_odd'], idx, P_ODD_MAIN, groups,
                                    odd_extras=(p['w_in_odd_gates'], cos, sin))
            yc = _retention(proj, p['ret_decay_f'][idx], p['ret_decay_b'][idx],
                            row(p['ret_norm'][idx]), groups)
            qkv = _dn_prep(proj, p['dn_conv'][idx], groups)
            yd = _deltanet(proj, gate_cols, qkv,
                           _gate_row(p['dn_a_log_f'][idx], p['dn_a_log_b'][idx]),
                           _gate_row(p['dn_dt_bias_f'][idx], p['dn_dt_bias_b'][idx]),
                           row(p['dn_norm'][idx]), groups)
            x = _outproj(x, yc, yd, 0, 0, g2, p['w_out_odd'], idx, groups)
        ffn2 = functools.partial(_ffn, gain=row(p['norm_ffn2'][layer]), shift=sh3, scale=sc3, gate=g3,
                                 w_in=p['w_ffn2_in'], w_out=p['w_ffn2_out'], layer=layer, groups=groups)
        if layer < DEPTH - 1:
            x = ffn2(x)
    closing = (row(p['norm_final']), fin[0], fin[1])
    return tuple(ffn2(x, span=span, out_is_span=True, final=closing) for span in spans)


def _prepare(p):
    q = dict(p)
    for name in ('w_ffn1_in', 'w_ffn1_out', 'w_ffn2_in', 'w_ffn2_out', 'w_in_even', 'w_out_even',
                 'pool_w', 'sgu_w', 'w_in_odd', 'w_out_odd'):
        q[name] = p[name].astype(BF16)
    gates = q['w_in_odd'][:, :, P_ODD_MAIN:]
    q['w_in_odd_gates'] = jnp.pad(gates, ((0, 0), (0, 0), (0, LANES - gates.shape[-1])))
    return q


def kernel(x_prompt, x_sample, c_prompt, c_sample, w_ada, b_ada, norm_ffn1, w_ffn1_in, w_ffn1_out, norm_mix, norm_ffn2, w_ffn2_in, w_ffn2_out, w_in_even, w_out_even, pool_w, pool_scale, sgu_norm, sgu_w, sgu_b, w_in_odd, w_out_odd, ret_decay_f, ret_decay_b, ret_norm, dn_conv, dn_a_log_f, dn_a_log_b, dn_dt_bias_f, dn_dt_bias_b, dn_norm, norm_final, w_ada_final, b_ada_final):
    p = _prepare({
        'w_ada': w_ada, 'b_ada': b_ada, 'norm_ffn1': norm_ffn1, 'w_ffn1_in': w_ffn1_in,
        'w_ffn1_out': w_ffn1_out, 'norm_mix': norm_mix, 'norm_ffn2': norm_ffn2,
        'w_ffn2_in': w_ffn2_in, 'w_ffn2_out': w_ffn2_out, 'w_in_even': w_in_even,
        'w_out_even': w_out_even, 'pool_w': pool_w, 'pool_scale': pool_scale,
        'sgu_norm': sgu_norm, 'sgu_w': sgu_w, 'sgu_b': sgu_b, 'w_in_odd': w_in_odd,
        'w_out_odd': w_out_odd, 'ret_decay_f': ret_decay_f, 'ret_decay_b': ret_decay_b,
        'ret_norm': ret_norm, 'dn_conv': dn_conv, 'dn_a_log_f': dn_a_log_f,
        'dn_a_log_b': dn_a_log_b, 'dn_dt_bias_f': dn_dt_bias_f, 'dn_dt_bias_b': dn_dt_bias_b,
        'dn_norm': dn_norm, 'norm_final': norm_final, 'w_ada_final': w_ada_final,
        'b_ada_final': b_ada_final,
    })
    bp, lp, d = x_prompt.shape
    bs, ls, _ = x_sample.shape
    groups = ((bp, lp), (bs, ls))
    xs = (x_prompt.reshape(bp * lp, d), x_sample.reshape(bs * ls, d))
    c = jnp.concatenate([c_prompt, c_sample], axis=0)
    y_prompt, y_sample = _trunk(xs, c, p, groups)
    return (y_prompt.reshape(bp, lp, d), y_sample.reshape(bs, ls, d))
```

```python
import functools

import jax
import jax.numpy as jnp
from jax import lax
from jax.experimental import pallas as pl
from jax.experimental.pallas import tpu as pltpu

F32 = jnp.float32
BF16 = jnp.bfloat16

D_MODEL = 2048
DEPTH = 4
D_HALF = D_MODEL // 2
POOL_WINDOWS = (2, 4, 8, 16)
POOL_GROUP = D_HALF // len(POOL_WINDOWS)
SGU_CHUNK = 128
SGU_HEAD = 128
SGU_GROUPS = D_HALF // SGU_HEAD
RET_HEADS = 4
RET_D = D_HALF // RET_HEADS
RET_CHUNK = 128
ROPE_BASE = 10000.0
DN_HEADS = 8
DN_D = D_HALF // DN_HEADS
DN_CONV = 4
DN_CHUNK = 64
D_FF = 5632
N_MOD = 9
EPS = 1e-6
P_EVEN = 3 * D_HALF
P_RET = 4 * D_HALF
P_ODD_MAIN = P_RET + 4 * D_HALF
P_ODD = P_ODD_MAIN + 4 * DN_HEADS

LANES = 128
SUBLANES = 8
HALO = 2 * SUBLANES
VMEM_LIMIT = 56 * 1024 * 1024

TM_FFN = 1024
TF_FFN = 512
TM_PROJ = 1024
TN_PROJ = 1024
MOD_ROWS = 16
MOD_UNROLL = 8
TM_OUT = 512
TL_EVEN = 256
TL_RET = 512
TL_DN = 512
TL_PREP = 512
PREP_ROWS = 64
TN_ADA = 1024
SEQ_PAD = 16


def _seq_info(row0, groups):
    seq = start = length = None
    t0 = s0 = 0
    for gi, (nb, ln) in enumerate(groups):
        rel = row0 - t0
        q = rel // ln
        if gi == 0:
            seq, start, length = q, q * ln, ln
        else:
            here = row0 >= t0
            seq = jnp.where(here, s0 + q, seq)
            start = jnp.where(here, t0 + q * ln, start)
            length = jnp.where(here, ln, length)
        t0 += nb * ln
        s0 += nb
    return seq, start, length


def _params(*sem):
    return pltpu.CompilerParams(dimension_semantics=sem, vmem_limit_bytes=VMEM_LIMIT)


def _dot(a, b):
    return jnp.dot(a, b, preferred_element_type=F32)


def _dot_nt(a, b):
    return lax.dot_general(a, b, (((1,), (1,)), ((), ())), preferred_element_type=F32)


def _dot_tn(a, b):
    return lax.dot_general(a, b, (((0,), (0,)), ((), ())), preferred_element_type=F32)


def _split_bf16(a):
    hi = a.astype(BF16)
    lo = (a - hi.astype(F32)).astype(BF16)
    return hi, lo


def _ada_kernel(c_ref, w_ref, b_ref, o_ref):
    c = c_ref[...]
    act = jax.nn.silu(c).astype(BF16)
    o_ref[...] = _dot(act, w_ref[...].astype(BF16)) + b_ref[...]


def _ada(c_pad, w, b):
    ly, d, n = w.shape
    s = c_pad.shape[0]
    tn = min(TN_ADA, n)
    return pl.pallas_call(
        _ada_kernel,
        out_shape=jax.ShapeDtypeStruct((ly, s, n), F32),
        grid=(ly, n // tn),
        in_specs=[
            pl.BlockSpec((s, d), lambda l, j: (0, 0)),
            pl.BlockSpec((None, d, tn), lambda l, j: (l, 0, j)),
            pl.BlockSpec((None, 1, tn), lambda l, j: (l, 0, j)),
        ],
        out_specs=pl.BlockSpec((None, s, tn), lambda l, j: (l, 0, j)),
        compiler_params=_params("parallel", "parallel"),
        name="ada_rows",
    )(c_pad, w, b.reshape(ly, 1, n))


def _modulate_into(h_ref, x_ref, gain_ref, sh_ref, sc_ref, zero_ref=None):
    tm = x_ref.shape[0]
    amp = gain_ref[...] * (1.0 + sc_ref[...])
    shift = sh_ref[...]

    def body(r, carry):
        rows = pl.ds(pl.multiple_of(r * MOD_ROWS, MOD_ROWS), MOD_ROWS)
        x = x_ref[rows, :]
        ms = jnp.mean(x * x, axis=-1, keepdims=True)
        h_ref[rows, :] = ((x * lax.rsqrt(ms + EPS)) * amp + shift).astype(BF16)
        if zero_ref is not None:
            zero_ref[rows, :] = jnp.zeros((MOD_ROWS, zero_ref.shape[1]), zero_ref.dtype)
        return carry

    lax.fori_loop(0, tm // MOD_ROWS, body, 0, unroll=MOD_UNROLL)


def _ffn_kernel(*refs, nf, final):
    x_ref, gain_ref, sh_ref, sc_ref, gt_ref, wg_ref, wu_ref, wo_ref = refs[:8]
    rest = refs[8:]
    if final:
        fgain_ref, fsh_ref, fsc_ref = rest[:3]
        rest = rest[3:]
    o_ref, h_ref = rest[:2]
    if final:
        inv_ref = rest[2]
    f = pl.program_id(1)

    @pl.when(f == 0)
    def _():
        _modulate_into(h_ref, x_ref, gain_ref, sh_ref, sc_ref, zero_ref=o_ref)

    h = h_ref[...]
    g = _dot(h, wg_ref[...])
    u = _dot(h, wu_ref[...])
    a = (jax.nn.silu(g) * u).astype(BF16)
    o_ref[...] += _dot(a, wo_ref[...])

    @pl.when(f == nf - 1)
    def _():
        half_gate = 0.5 * gt_ref[...]
        if not final:
            o_ref[...] = x_ref[...] + half_gate * o_ref[...]
            return
        amp = fgain_ref[...] * (1.0 + fsc_ref[...])
        shift = fsh_ref[...]
        steps = x_ref.shape[0] // MOD_ROWS

        def residual(r):
            rows = pl.ds(pl.multiple_of(r * MOD_ROWS, MOD_ROWS), MOD_ROWS)
            return rows, x_ref[rows, :] + half_gate * o_ref[rows, :]

        def stats(r, carry):
            rows, y = residual(r)
            inv_ref[rows, :] = lax.rsqrt(jnp.mean(y * y, axis=-1, keepdims=True) + EPS)
            return carry

        def apply(r, carry):
            rows, y = residual(r)
            o_ref[rows, :] = (y * inv_ref[rows, :]) * amp + shift
            return carry

        lax.fori_loop(0, steps, stats, 0, unroll=MOD_UNROLL)
        lax.fori_loop(0, steps, apply, 0, unroll=MOD_UNROLL)


def _row_spec(groups, tm, d, row_off=0):
    return pl.BlockSpec((None, 1, d), lambda i, j: (_seq_info(i * tm + row_off, groups)[0], 0, 0))


def _ffn(x, gain, shift, scale, gate, w_in, w_out, layer, groups, *, span=None, final=None):
    t, d = x.shape
    row_off, n_rows = span if span is not None else (0, t)
    ff = w_out.shape[1]
    tm = min(TM_FFN, groups[0][1])
    tf = min(TF_FFN, ff)
    nf = ff // tf
    tile_off = row_off // tm
    row = _row_spec(groups, tm, d, row_off)
    in_specs = [
        pl.BlockSpec((tm, d), lambda i, f: (i + tile_off, 0)),
        pl.BlockSpec((1, d), lambda i, f: (0, 0)),
        row, row, row,
        pl.BlockSpec((None, d, tf), lambda i, f: (layer, 0, f)),
        pl.BlockSpec((None, d, tf), lambda i, f: (layer, 0, nf + f)),
        pl.BlockSpec((None, tf, d), lambda i, f: (layer, f, 0)),
    ]
    args = [x, gain, shift, scale, gate, w_in, w_in, w_out]
    if final is not None:
        in_specs += [pl.BlockSpec((1, d), lambda i, f: (0, 0)), row, row]
        args += list(final)
    return pl.pallas_call(
        functools.partial(_ffn_kernel, nf=nf, final=final is not None),
        out_shape=jax.ShapeDtypeStruct((n_rows, d), F32),
        grid=(n_rows // tm, nf),
        in_specs=in_specs,
        out_specs=pl.BlockSpec((tm, d), lambda i, f: (i, 0)),
        scratch_shapes=[pltpu.VMEM((tm, d), BF16)] + ([pltpu.VMEM((tm, 1), F32)] if final is not None else []),
        compiler_params=_params("parallel", "arbitrary"),
        name="ffn",
    )(*args)


def _proj_kernel(*refs, odd):
    if odd:
        x_ref, gain_ref, sh_ref, sc_ref, w_ref, wn_ref, cos_ref, sin_ref, o_ref, on_ref, h_ref = refs
    else:
        x_ref, gain_ref, sh_ref, sc_ref, w_ref, o_ref, h_ref = refs
    j = pl.program_id(1)

    @pl.when(j == 0)
    def _():
        _modulate_into(h_ref, x_ref, gain_ref, sh_ref, sc_ref)
        if odd:
            on_ref[...] = _dot(h_ref[...], wn_ref[...])

    y = _dot(h_ref[...], w_ref[...])
    if not odd:
        o_ref[...] = y.astype(BF16)
        return

    tn = o_ref.shape[1]
    rope_tiles = 2 * D_HALF // tn
    half = RET_D // 2
    rotate = j < rope_tiles
    cos = jnp.where(rotate, cos_ref[...], 1.0)
    sin = jnp.where(rotate, sin_ref[...], 0.0)
    k_scale = jnp.where(rotate & (j >= rope_tiles // 2), RET_D ** -0.5, 1.0).astype(F32)
    for hd in range(tn // RET_D):
        x1 = y[:, hd * RET_D:hd * RET_D + half]
        x2 = y[:, hd * RET_D + half:(hd + 1) * RET_D]
        o_ref[:, hd * RET_D:hd * RET_D + half] = ((x1 * cos - x2 * sin) * k_scale).astype(BF16)
        o_ref[:, hd * RET_D + half:(hd + 1) * RET_D] = ((x1 * sin + x2 * cos) * k_scale).astype(BF16)


def _proj(x, gain, shift, scale, w, idx, n, groups, odd_extras=None):
    t, d = x.shape
    tm = min(TM_PROJ, groups[0][1])
    tn = TN_PROJ
    row = _row_spec(groups, tm, d)
    odd = odd_extras is not None
    in_specs = [
        pl.BlockSpec((tm, d), lambda i, j: (i, 0)),
        pl.BlockSpec((1, d), lambda i, j: (0, 0)),
        row, row,
        pl.BlockSpec((None, d, tn), lambda i, j: (idx, 0, j)),
    ]
    out_shape = jax.ShapeDtypeStruct((t, n), BF16)
    out_specs = pl.BlockSpec((tm, tn), lambda i, j: (i, j))
    args = (x, gain, shift, scale, w)
    if odd:
        half = RET_D // 2

        def pos_block(i, j):
            row0 = i * tm
            return ((row0 - _seq_info(row0, groups)[1]) // tm, 0)

        tab = pl.BlockSpec((tm, half), pos_block)
        in_specs += [pl.BlockSpec((None, d, LANES), lambda i, j: (idx, 0, 0)), tab, tab]
        out_shape = (out_shape, jax.ShapeDtypeStruct((t, LANES), F32))
        out_specs = (out_specs, pl.BlockSpec((tm, LANES), lambda i, j: (i, 0)))
        args = args + tuple(odd_extras)
    return pl.pallas_call(
        functools.partial(_proj_kernel, odd=odd),
        out_shape=out_shape,
        grid=(t // tm, n // tn),
        in_specs=in_specs,
        out_specs=out_specs,
        scratch_shapes=[pltpu.VMEM((tm, d), BF16)],
        compiler_params=_params("parallel", "arbitrary"),
        name="mix_proj",
    )(*args)


def _outproj_kernel(x_ref, ya_ref, yb_ref, gt_ref, wa_ref, wb_ref, o_ref):
    y = _dot(ya_ref[...], wa_ref[...]) + _dot(yb_ref[...], wb_ref[...])
    o_ref[...] = x_ref[...] + gt_ref[...] * y


def _outproj(x, ya, yb, ca, cb, gate, w, idx, groups):
    t, d = x.shape
    dh = d // 2
    tm = min(TM_OUT, groups[0][1])
    row = _row_spec(groups, tm, d)
    return pl.pallas_call(
        _outproj_kernel,
        out_shape=jax.ShapeDtypeStruct((t, d), F32),
        grid=(t // tm, 1),
        in_specs=[
            pl.BlockSpec((tm, d), lambda i, j: (i, 0)),
            pl.BlockSpec((tm, dh), lambda i, j: (i, ca)),
            pl.BlockSpec((tm, dh), lambda i, j: (i, cb)),
            row,
            pl.BlockSpec((None, dh, d), lambda i, j: (idx, 0, 0)),
            pl.BlockSpec((None, dh, d), lambda i, j: (idx, 1, 0)),
        ],
        out_specs=pl.BlockSpec((tm, d), lambda i, j: (i, 0)),
        compiler_params=_params("parallel", "arbitrary"),
        name="mix_out",
    )(x, ya, yb, gate, w, w)


def _halo_specs(tl, width, nrows, col_of):
    per = tl // HALO
    last = nrows // HALO - 1
    prev = pl.BlockSpec((HALO, width), lambda i, *r: (jnp.maximum(i * per - 1, 0), col_of(i, *r)))
    nxt = pl.BlockSpec((HALO, width), lambda i, *r: (jnp.minimum((i + 1) * per, last), col_of(i, *r)))
    return prev, nxt


def _fill_ext(ext_ref, x_ref, prev_ref, next_ref, first, last, tl):
    dt = ext_ref.dtype
    ext_ref[HALO:HALO + tl, :] = x_ref[...].astype(dt)
    ext_ref[0:HALO, :] = jnp.where(first, jnp.zeros_like(prev_ref), prev_ref[...]).astype(dt)
    ext_ref[HALO + tl:2 * HALO + tl, :] = jnp.where(last, jnp.zeros_like(next_ref), next_ref[...]).astype(dt)


def _row_window(tl, lo, hi):
    r = lax.broadcasted_iota(jnp.int32, (tl, tl + 2 * HALO), 0) + HALO
    col = lax.broadcasted_iota(jnp.int32, (tl, tl + 2 * HALO), 1)
    return ((col >= r + lo) & (col < r + hi)).astype(BF16)


def _even_kernel(xa_ref, prev_ref, next_ref, u_ref, v_ref, pw_ref, ps_ref, ng_ref, sw_ref, sb_ref,
                 o_ref, ext_ref, vn_ref, win_ref, *, tl, groups):
    @pl.when(pl.program_id(0) == 0)
    def _():
        for gi, w in enumerate(POOL_WINDOWS):
            win_ref[gi] = _row_window(tl, -(w // 2), w - w // 2)

    row0 = pl.program_id(0) * tl
    _, sstart, slen = _seq_info(row0, groups)
    pos0 = row0 - sstart
    _fill_ext(ext_ref, xa_ref, prev_ref, next_ref, pos0 == 0, pos0 + tl == slen, tl)

    t = pos0 + lax.broadcasted_iota(jnp.int32, (tl, 1), 0)
    for gi, w in enumerate(POOL_WINDOWS):
        c0 = gi * POOL_GROUP
        cols = slice(c0, c0 + POOL_GROUP)
        s = _dot(win_ref[gi], ext_ref[:, cols])
        lo = jnp.clip(t - w // 2, 0, slen)
        hi = jnp.clip(t + (w - w // 2), 0, slen)
        cnt = (hi - lo).astype(F32)
        pooled = (s / cnt - xa_ref[:, cols].astype(F32)).astype(BF16)
        ya = _dot(pooled, pw_ref[gi]) * ps_ref[:, cols]
        o_ref[:, cols] = ya.astype(BF16)

    v = jax.nn.gelu(v_ref[...].astype(F32))
    vms = jnp.mean(v * v, axis=-1, keepdims=True)
    vn_ref[...] = ((v * lax.rsqrt(vms + EPS)) * ng_ref[...]).astype(BF16)
    for n in range(tl // SGU_CHUNK):
        rows = slice(n * SGU_CHUNK, (n + 1) * SGU_CHUNK)
        for g in range(SGU_GROUPS):
            cols = slice(g * SGU_HEAD, (g + 1) * SGU_HEAD)
            mixed = _dot(sw_ref[g], vn_ref[rows, cols]) + sb_ref[g]
            u = jax.nn.gelu(u_ref[rows, cols].astype(F32))
            o_ref[rows, D_HALF + g * SGU_HEAD:D_HALF + (g + 1) * SGU_HEAD] = (u * mixed).astype(BF16)


def _even_mix(proj, pool_w, pool_scale, sgu_norm, sgu_w, sgu_b, groups):
    t = proj.shape[0]
    tl = min(TL_EVEN, groups[0][1])
    prev, nxt = _halo_specs(tl, D_HALF, t, lambda i: 0)
    const2 = lambda i: (0, 0)
    const3 = lambda i: (0, 0, 0)
    return pl.pallas_call(
        functools.partial(_even_kernel, tl=tl, groups=groups),
        out_shape=jax.ShapeDtypeStruct((t, 2 * D_HALF), BF16),
        grid=(t // tl,),
        in_specs=[
            pl.BlockSpec((tl, D_HALF), lambda i: (i, 0)),
            prev, nxt,
            pl.BlockSpec((tl, D_HALF), lambda i: (i, 1)),
            pl.BlockSpec((tl, D_HALF), lambda i: (i, 2)),
            pl.BlockSpec(pool_w.shape, const3),
            pl.BlockSpec((1, D_HALF), const2),
            pl.BlockSpec((1, D_HALF), const2),
            pl.BlockSpec(sgu_w.shape, const3),
            pl.BlockSpec(sgu_b.shape, const3),
        ],
        out_specs=pl.BlockSpec((tl, 2 * D_HALF), lambda i: (i, 0)),
        scratch_shapes=[pltpu.VMEM((tl + 2 * HALO, D_HALF), BF16), pltpu.VMEM((tl, D_HALF), BF16),
                        pltpu.VMEM((len(POOL_WINDOWS), tl, tl + 2 * HALO), BF16)],
        compiler_params=_params("arbitrary"),
        name="even_mix",
    )(proj, proj, proj, proj, proj, pool_w, pool_scale, sgu_norm, sgu_w, sgu_b)


def _rope_kernel(inv_ref, cos_ref, sin_ref, *, tl):
    pos = (pl.program_id(0) * tl + lax.broadcasted_iota(jnp.int32, (tl, 1), 0)).astype(F32)
    ang = pos * inv_ref[...]
    cos_ref[...] = jnp.cos(ang)
    sin_ref[...] = jnp.sin(ang)


def _rope_tables(max_len, tl):
    half = RET_D // 2
    inv = (1.0 / (ROPE_BASE ** jnp.linspace(0.0, 1.0, half, dtype=F32))).reshape(1, half)
    shp = jax.ShapeDtypeStruct((max_len, half), F32)
    return pl.pallas_call(
        functools.partial(_rope_kernel, tl=tl),
        out_shape=(shp, shp),
        grid=(max_len // tl,),
        in_specs=[pl.BlockSpec((1, half), lambda i: (0, 0))],
        out_specs=(pl.BlockSpec((tl, half), lambda i: (i, 0)), pl.BlockSpec((tl, half), lambda i: (i, 0))),
        compiler_params=_params("parallel"),
        name="rope_table",
    )(inv)


def _ret_kernel(*refs, tl, nt, groups, reverse):
    if reverse:
        q_ref, k_ref, v_ref, dec_ref, o_ref, s_ref = refs
    else:
        q_ref, k_ref, v_ref, dec_ref, decb_ref, g_ref, ob_ref, ng_ref, o_ref, s_ref = refs
    j = pl.program_id(0)
    it = nt - 1 - j if reverse else j
    row0 = it * tl
    _, sstart, slen = _seq_info(row0, groups)
    pos0 = row0 - sstart
    reset = (pos0 + tl == slen) if reverse else (pos0 == 0)

    @pl.when(reset)
    def _():
        s_ref[...] = jnp.zeros_like(s_ref)

    c = RET_CHUNK
    heads = range(RET_HEADS)
    idx = lax.broadcasted_iota(jnp.int32, (c, 1), 0).astype(F32)
    lg = [jnp.log1p(-jnp.exp2(-dec_ref[h])) for h in heads]
    if reverse:
        q_dec = [jnp.exp(x * (c - idx)) for x in lg]
        k_dec = [jnp.exp(x * idx) for x in lg]
    else:
        q_dec = [jnp.exp(x * (idx + 1.0)) for x in lg]
        k_dec = [jnp.exp(x * (c - 1.0 - idx)) for x in lg]
        lgb = [jnp.log1p(-jnp.exp2(-decb_ref[h])) for h in heads]
        ri = lax.broadcasted_iota(jnp.int32, (c, c), 0)
        ci = lax.broadcasted_iota(jnp.int32, (c, c), 1)
        rel = (ri - ci).astype(F32)
        dmat = [jnp.where(rel >= 0, jnp.exp(x * jnp.maximum(rel, 0.0)), 0.0)
                + jnp.where(rel <= 0, jnp.exp(y * jnp.maximum(-rel, 0.0)), 0.0) for x, y in zip(lg, lgb)]
    chunk_dec = [jnp.exp(x * float(c)) for x in lg]

    nc = tl // c
    order = list(range(nc - 1, -1, -1) if reverse else range(nc))
    units = [(ch, h) for ch in order for h in heads]

    def rows(ch):
        return slice(ch * c, (ch + 1) * c)

    def cols(h):
        return slice(h * RET_D, (h + 1) * RET_D)

    q = [q_ref[rows(ch), cols(h)] for ch, h in units]
    v = [v_ref[rows(ch), cols(h)] for ch, h in units]
    kv = [_dot_tn(k_ref[rows(ch), cols(h)], (x.astype(F32) * k_dec[h]).astype(BF16))
          for (ch, h), x in zip(units, v)]
    if not reverse:
        scores = [(_dot_nt(x, k_ref[rows(ch), cols(h)]) * dmat[h]).astype(BF16) for (ch, h), x in zip(units, q)]
        intra = [_dot(x, y) for x, y in zip(scores, v)]

    for ci_, ch in enumerate(order):
        s = [s_ref[h] for h in heads]
        inter = [_dot(q[ci_ * RET_HEADS + h], s[h].astype(BF16)) * q_dec[h] for h in heads]
        for h in heads:
            s_ref[h] = s[h] * chunk_dec[h] + kv[ci_ * RET_HEADS + h]
        for h in heads:
            if reverse:
                o_ref[rows(ch), cols(h)] = inter[h]
            else:
                o = intra[ci_ * RET_HEADS + h] + inter[h] + ob_ref[rows(ch), cols(h)]
                mu = jnp.mean(o, axis=-1, keepdims=True)
                var = jnp.mean(jnp.square(o - mu), axis=-1, keepdims=True)
                on = ((o - mu) * lax.rsqrt(var + EPS)) * ng_ref[:, cols(h)]
                gate = jax.nn.silu(g_ref[rows(ch), cols(h)].astype(F32))
                o_ref[rows(ch), cols(h)] = (gate * on).astype(BF16)


def _retention(proj, decay_f, decay_b, norm_g, groups):
    t = proj.shape[0]
    tl = min(TL_RET, groups[0][1])
    nt = t // tl
    dec = pl.BlockSpec((RET_HEADS, 1, 1), lambda j: (0, 0, 0))
    state = pltpu.VMEM((RET_HEADS, RET_D, RET_D), F32)

    def section(reverse):
        tile = (lambda j: nt - 1 - j) if reverse else (lambda j: j)
        return lambda s: pl.BlockSpec((tl, D_HALF), lambda j: (tile(j), s))

    sec = section(True)
    ob = pl.pallas_call(
        functools.partial(_ret_kernel, tl=tl, nt=nt, groups=groups, reverse=True),
        out_shape=jax.ShapeDtypeStruct((t, D_HALF), F32),
        grid=(nt,),
        in_specs=[sec(0), sec(1), sec(2), dec],
        out_specs=sec(0),
        scratch_shapes=[state],
        compiler_params=_params("arbitrary"),
        name="ret_bwd",
    )(proj, proj, proj, decay_b.reshape(RET_HEADS, 1, 1))

    sec = section(False)
    return pl.pallas_call(
        functools.partial(_ret_kernel, tl=tl, nt=nt, groups=groups, reverse=False),
        out_shape=jax.ShapeDtypeStruct((t, D_HALF), BF16),
        grid=(nt,),
        in_specs=[sec(0), sec(1), sec(2), dec, dec, sec(3), sec(0),
                  pl.BlockSpec((1, D_HALF), lambda j: (0, 0))],
        out_specs=sec(0),
        scratch_shapes=[state],
        compiler_params=_params("arbitrary"),
        name="ret_fwd",
    )(proj, proj, proj, decay_f.reshape(RET_HEADS, 1, 1), decay_b.reshape(RET_HEADS, 1, 1),
      proj, ob, norm_g)


def _dnprep_kernel(x_ref, prev_ref, next_ref, w_ref, o_ref, ext_ref, *, tl, groups):
    row0 = pl.program_id(0) * tl
    part = pl.program_id(1)
    _, sstart, slen = _seq_info(row0, groups)
    pos0 = row0 - sstart
    _fill_ext(ext_ref, x_ref, prev_ref, next_ref, pos0 == 0, pos0 + tl == slen, tl)
    left = DN_CONV // 2
    q_scale = jnp.where(part == 0, DN_D ** -0.5, 1.0).astype(F32)
    rb = PREP_ROWS
    taps = [tap for tap in range(DN_CONV) if tap != left]
    r = lax.broadcasted_iota(jnp.int32, (rb, rb + 2 * HALO), 0) + HALO
    col = lax.broadcasted_iota(jnp.int32, (rb, rb + 2 * HALO), 1)
    select = jnp.concatenate([(col == r + (tap - left)).astype(BF16) for tap in taps], axis=0)
    for blk in range(tl // rb):
        rows = slice(blk * rb, (blk + 1) * rb)
        for h in range(DN_HEADS):
            cols = slice(h * DN_D, (h + 1) * DN_D)
            shifted = _dot(select, ext_ref[blk * rb:(blk + 1) * rb + 2 * HALO, cols])
            conv = x_ref[rows, cols].astype(F32) * w_ref[left:left + 1, cols]
            for n, tap in enumerate(taps):
                conv = conv + shifted[n * rb:(n + 1) * rb] * w_ref[tap:tap + 1, cols]
            y = jax.nn.silu(conv)
            inv_norm = lax.rsqrt(jnp.sum(y * y, axis=-1, keepdims=True) + EPS)
            o_ref[rows, cols] = (y * jnp.where(part < 2, inv_norm * q_scale, 1.0)).astype(BF16)


def _dn_prep(proj, conv_w, groups):
    t = proj.shape[0]
    tl = min(TL_PREP, groups[0][1])
    base = P_RET // D_HALF
    prev, nxt = _halo_specs(tl, D_HALF, t, lambda i, part: base + part)
    return pl.pallas_call(
        functools.partial(_dnprep_kernel, tl=tl, groups=groups),
        out_shape=jax.ShapeDtypeStruct((t, 3 * D_HALF), BF16),
        grid=(t // tl, 3),
        in_specs=[pl.BlockSpec((tl, D_HALF), lambda i, part: (i, base + part)), prev, nxt,
                  pl.BlockSpec((DN_CONV, D_HALF), lambda i, part: (0, part))],
        out_specs=pl.BlockSpec((tl, D_HALF), lambda i, part: (i, part)),
        scratch_shapes=[pltpu.VMEM((tl + 2 * HALO, D_HALF), BF16)],
        compiler_params=_params("parallel", "parallel"),
        name="dn_prep",
    )(proj, proj, proj, conv_w)


def _dot_hi_each(lhs_parts, rhs_parts):
    m = lhs_parts[0][0].shape[0]
    n = rhs_parts[0][0].shape[1]
    quads = [_dot(jnp.concatenate(a, axis=0), jnp.concatenate(b, axis=1)) for a, b in zip(lhs_parts, rhs_parts)]
    return [(x[:m, :n] + x[m:, :n]) + (x[:m, n:] + x[m:, n:]) for x in quads]


def _block_diag(y, left):
    zero = jnp.zeros_like(y)
    return jnp.concatenate([jnp.where(left, y, zero), jnp.where(left, zero, y)], axis=0)


def _pair_products(lhs, rhs, left):
    ls = [_split_bf16(x) for x in lhs]
    rs = [tuple(_block_diag(part, left) for part in _split_bf16(y)) for y in rhs]
    return _dot_hi_each(ls, rs)


def _unit_triangular_inverses(mats, eye, ri, ci, left):
    size = SUBLANES
    same = (ri // size) == (ci // size)
    ps = [jnp.where(same, a, 0.0) for a in mats]
    invs = [eye - d for d in ps]
    n = 2
    while n < size:
        ps = _pair_products(ps, ps, left)
        invs = [inv + x for inv, x in zip(invs, _pair_products(invs, ps, left))]
        n *= 2
    while size < DN_CHUNK:
        size *= 2
        merged = (ri // size) == (ci // size)
        es = [jnp.where(merged & ~same, a, 0.0) for a in mats]
        invs = [inv - x for inv, x in zip(invs, _pair_products(_pair_products(invs, es, left), invs, left))]
        same = merged
    return invs


def _dn_kernel(*refs, tl, nt, groups, reverse):
    if reverse:
        q_ref, k_ref, v_ref, gates_ref, alog_ref, dtb_ref, o_ref, s_ref = refs
    else:
        (q_ref, k_ref, v_ref, gates_ref, alog_ref, dtb_ref, z_ref, ob_ref, ng_ref,
         o_ref, s_ref) = refs
    j = pl.program_id(0)
    it = nt - 1 - j if reverse else j
    row0 = it * tl
    _, sstart, slen = _seq_info(row0, groups)
    pos0 = row0 - sstart
    reset = (pos0 + tl == slen) if reverse else (pos0 == 0)

    @pl.when(reset)
    def _():
        s_ref[...] = jnp.zeros_like(s_ref)

    gates = gates_ref[...]
    beta_all = jax.nn.sigmoid(gates)
    la_all = -jnp.exp(alog_ref[...]) * jax.nn.softplus(gates + dtb_ref[...])
    cb0 = DN_HEADS if reverse else 0
    ca0 = cb0 + 2 * DN_HEADS

    c = DN_CHUNK
    left = lax.broadcasted_iota(jnp.int32, (1, 2 * c), 1) < c
    ri = lax.broadcasted_iota(jnp.int32, (c, 2 * c), 0)
    ci = jnp.bitwise_and(lax.broadcasted_iota(jnp.int32, (c, 2 * c), 1), c - 1)
    eye = (ri == ci).astype(F32)
    incl = (ri <= ci) if reverse else (ri >= ci)
    strict = (ri < ci) if reverse else (ri > ci)
    incl_t = (ri >= ci) if reverse else (ri <= ci)
    last = 0 if reverse else c - 1

    nc = tl // c
    order = list(range(nc - 1, -1, -1) if reverse else range(nc))
    heads = range(DN_HEADS)
    half_heads = range(DN_HEADS // 2)
    units = [(ch, h) for ch in order for h in heads]
    pairs = range(len(units) // 2)

    def rows(ch):
        return slice(ch * c, (ch + 1) * c)

    def cols(h):
        return slice(h * DN_D, (h + 1) * DN_D)

    def paired(xs):
        return [jnp.where(left, xs[2 * p], xs[2 * p + 1]) for p in pairs]

    q = [q_ref[rows(ch), cols(h)] for ch, h in units]
    k = [k_ref[rows(ch), cols(h)] for ch, h in units]
    beta = [beta_all[rows(ch), cb0 + h:cb0 + h + 1] for ch, h in units]
    la = [la_all[rows(ch), ca0 + h:ca0 + h + 1] for ch, h in units]
    la2 = paired(la)
    la_row = [jnp.sum(eye * x, axis=0, keepdims=True) for x in la2]
    g_row = [jnp.sum(jnp.where(incl_t, x, 0.0), axis=0, keepdims=True) for x in la2]
    windowed = [jnp.where(incl, x, 0.0) for x in la_row]
    g_col = [jnp.sum(jnp.where(left == (u % 2 == 0), windowed[u // 2], 0.0), axis=1, keepdims=True)
             for u in range(len(units))]
    gam = [jnp.where(incl, jnp.exp(jnp.where(incl, gc - gr, 0.0)), 0.0) for gc, gr in zip(paired(g_col), g_row)]
    eg = [jnp.exp(gc) for gc in g_col]
    g_last = [gc[last:last + 1, :] for gc in g_col]
    kb = [x.astype(F32) * b for x, b in zip(k, beta)]
    kq = [_dot_nt(jnp.concatenate([kb[2 * p].astype(BF16), q[2 * p], kb[2 * p + 1].astype(BF16), q[2 * p + 1]],
                                  axis=0),
                  jnp.concatenate([k[2 * p], k[2 * p + 1]], axis=0)) for p in pairs]
    a = [jnp.where(strict, jnp.where(left, x[:c], x[2 * c:3 * c]) * gm, 0.0) for x, gm in zip(kq, gam)]
    attn = [_block_diag((jnp.where(left, x[c:2 * c], x[3 * c:]) * gm).astype(BF16), left)
            for x, gm in zip(kq, gam)]
    tinv = _unit_triangular_inverses(a, eye, ri, ci, left)
    rhs = [jnp.concatenate([v_ref[rows(ch), cols(h)].astype(F32) * b, x * e], axis=1)
           for (ch, h), b, x, e in zip(units, beta, kb, eg)]
    uw2 = _dot_hi_each([tuple(_block_diag(part, left) for part in _split_bf16(t)) for t in tinv],
                       [_split_bf16(jnp.concatenate([rhs[2 * p], rhs[2 * p + 1]], axis=0)) for p in pairs])
    uw = [uw2[u // 2][(u % 2) * c:(u % 2 + 1) * c] for u in range(len(units))]
    tail = [jnp.exp(gl - gc) for gl, gc in zip(g_last, g_col)]
    dec = [jnp.exp(gl) for gl in g_last]

    for ci_, ch in enumerate(order):
        idx = [ci_ * DN_HEADS + h for h in heads]
        s = [s_ref[h] for h in heads]
        sb = [x.astype(BF16) for x in s]
        ws = [_dot(jnp.concatenate([uw[i][:, DN_D:].astype(BF16), q[i]], axis=0), sb[h])
              for h, i in zip(heads, idx)]
        v_new = [uw[i][:, :DN_D] - x[:c] for i, x in zip(idx, ws)]
        av = [_dot(attn[idx[2 * x] // 2],
                   jnp.concatenate([v_new[2 * x].astype(BF16), v_new[2 * x + 1].astype(BF16)], axis=0))
              for x in half_heads]
        o = [ws[h][c:] * eg[idx[h]] + av[h // 2][(h % 2) * c:(h % 2 + 1) * c] for h in heads]
        for h, i in zip(heads, idx):
            s_ref[h] = s[h] * dec[i] + _dot_tn(k[i], (v_new[h] * tail[i]).astype(BF16))
        for h in heads:
            if reverse:
                o_ref[rows(ch), cols(h)] = o[h]
            else:
                oo = o[h] + ob_ref[rows(ch), cols(h)]
                on = (oo * lax.rsqrt(jnp.mean(oo * oo, axis=-1, keepdims=True) + EPS)) * ng_ref[...]
                z = z_ref[rows(ch), cols(h)].astype(F32)
                o_ref[rows(ch), cols(h)] = (on * jax.nn.silu(z)).astype(BF16)


def _deltanet(proj, gate_cols, qkv, alog_row, dtb_row, norm_g, groups):
    t = proj.shape[0]
    tl = min(TL_DN, groups[0][1])
    nt = t // tl
    z_blk = (P_RET + 3 * D_HALF) // D_HALF
    row = pl.BlockSpec((1, LANES), lambda j: (0, 0))
    state = pltpu.VMEM((DN_HEADS, DN_D, DN_D), F32)

    def common(reverse):
        tile = (lambda j: nt - 1 - j) if reverse else (lambda j: j)
        sec = lambda s: pl.BlockSpec((tl, D_HALF), lambda j: (tile(j), s))
        gates = pl.BlockSpec((tl, LANES), lambda j: (tile(j), 0))
        return sec, gates

    sec, gates = common(True)
    ob = pl.pallas_call(
        functools.partial(_dn_kernel, tl=tl, nt=nt, groups=groups, reverse=True),
        out_shape=jax.ShapeDtypeStruct((t, D_HALF), F32),
        grid=(nt,),
        in_specs=[sec(0), sec(1), sec(2), gates, row, row],
        out_specs=sec(0),
        scratch_shapes=[state],
        compiler_params=_params("arbitrary"),
        name="dn_bwd",
    )(qkv, qkv, qkv, gate_cols, alog_row, dtb_row)

    sec, gates = common(False)
    return pl.pallas_call(
        functools.partial(_dn_kernel, tl=tl, nt=nt, groups=groups, reverse=False),
        out_shape=jax.ShapeDtypeStruct((t, D_HALF), BF16),
        grid=(nt,),
        in_specs=[sec(0), sec(1), sec(2), gates, row, row, sec(z_blk), sec(0), row],
        out_specs=sec(0),
        scratch_shapes=[state],
        compiler_params=_params("arbitrary"),
        name="dn_fwd",
    )(qkv, qkv, qkv, gate_cols, alog_row, dtb_row, proj, ob, norm_g)


def _gate_row(f_vals, b_vals):
    row = jnp.zeros((LANES,), F32)
    row = row.at[2 * DN_HEADS:3 * DN_HEADS].set(f_vals.astype(F32))
    row = row.at[3 * DN_HEADS:4 * DN_HEADS].set(b_vals.astype(F32))
    return row.reshape(1, LANES)


def _trunk(x, c, p, groups):
    d = x.shape[1]
    spans = []
    for nb, ln in groups:
        spans.append((sum(r for _, r in spans), nb * ln))
    n_seq = c.shape[0]
    c_pad = jnp.zeros((SEQ_PAD, d), F32).at[:n_seq].set(c)
    mods = _ada(c_pad, p['w_ada'], p['b_ada'])
    mods = mods.reshape(DEPTH, SEQ_PAD, N_MOD, 1, d).transpose(0, 2, 1, 3, 4)
    fin = _ada(c_pad, p['w_ada_final'][None], p['b_ada_final'][None])
    fin = fin.reshape(SEQ_PAD, 2, 1, d).transpose(1, 0, 2, 3)

    max_len = max(ln for _, ln in groups)
    cos, sin = _rope_tables(max_len, min(TL_RET, groups[0][1]))

    row = lambda a: a.reshape(1, -1)
    for layer in range(DEPTH):
        sh1, sc1, g1, sh2, sc2, g2, sh3, sc3, g3 = [mods[layer, jm] for jm in range(N_MOD)]
        ffn1 = functools.partial(_ffn, gain=row(p['norm_ffn1'][layer]), shift=sh1, scale=sc1, gate=g1,
                                 w_in=p['w_ffn1_in'], w_out=p['w_ffn1_out'], layer=layer, groups=groups)
        x = ffn1(x)
        idx = layer // 2
        gain = row(p['norm_mix'][layer])
        if layer % 2 == 0:
            proj = _proj(x, gain, sh2, sc2, p['w_in_even'], idx, P_EVEN, groups)
            y = _even_mix(proj, p['pool_w'][idx], row(p['pool_scale'][idx]), row(p['sgu_norm'][idx]),
                          p['sgu_w'][idx], p['sgu_b'][idx][..., None], groups)
            x = _outproj(x, y, y, 0, 1, g2, p['w_out_even'], idx, groups)
        else:
            proj, gate_cols = _proj(x, gain, sh2, sc2, p['w_in_odd'], idx, P_ODD_MAIN, groups,
                                    odd_extras=(p['w_in_odd_gates'], cos, sin))
            yc = _retention(proj, p['ret_decay_f'][idx], p['ret_decay_b'][idx],
                            row(p['ret_norm'][idx]), groups)
            qkv = _dn_prep(proj, p['dn_conv'][idx], groups)
            yd = _deltanet(proj, gate_cols, qkv,
                           _gate_row(p['dn_a_log_f'][idx], p['dn_a_log_b'][idx]),
                           _gate_row(p['dn_dt_bias_f'][idx], p['dn_dt_bias_b'][idx]),
                           row(p['dn_norm'][idx]), groups)
            x = _outproj(x, yc, yd, 0, 0, g2, p['w_out_odd'], idx, groups)
        ffn2 = functools.partial(_ffn, gain=row(p['norm_ffn2'][layer]), shift=sh3, scale=sc3, gate=g3,
                                 w_in=p['w_ffn2_in'], w_out=p['w_ffn2_out'], layer=layer, groups=groups)
        if layer < DEPTH - 1:
            x = ffn2(x)
    closing = (row(p['norm_final']), fin[0], fin[1])
    return tuple(ffn2(x, span=span, final=closing) for span in spans)


def _prepare(p):
    q = dict(p)
    for name in ('w_ffn1_in', 'w_ffn1_out', 'w_ffn2_in', 'w_ffn2_out', 'w_in_even', 'w_out_even',
                 'pool_w', 'sgu_w', 'w_in_odd', 'w_out_odd'):
        q[name] = p[name].astype(BF16)
    gates = q['w_in_odd'][:, :, P_ODD_MAIN:]
    q['w_in_odd_gates'] = jnp.pad(gates, ((0, 0), (0, 0), (0, LANES - gates.shape[-1])))
    return q


def kernel(x_prompt, x_sample, c_prompt, c_sample, w_ada, b_ada, norm_ffn1, w_ffn1_in, w_ffn1_out, norm_mix, norm_ffn2, w_ffn2_in, w_ffn2_out, w_in_even, w_out_even, pool_w, pool_scale, sgu_norm, sgu_w, sgu_b, w_in_odd, w_out_odd, ret_decay_f, ret_decay_b, ret_norm, dn_conv, dn_a_log_f, dn_a_log_b, dn_dt_bias_f, dn_dt_bias_b, dn_norm, norm_final, w_ada_final, b_ada_final):
    p = _prepare({
        'w_ada': w_ada, 'b_ada': b_ada, 'norm_ffn1': norm_ffn1, 'w_ffn1_in': w_ffn1_in,
        'w_ffn1_out': w_ffn1_out, 'norm_mix': norm_mix, 'norm_ffn2': norm_ffn2,
        'w_ffn2_in': w_ffn2_in, 'w_ffn2_out': w_ffn2_out, 'w_in_even': w_in_even,
        'w_out_even': w_out_even, 'pool_w': pool_w, 'pool_scale': pool_scale,
        'sgu_norm': sgu_norm, 'sgu_w': sgu_w, 'sgu_b': sgu_b, 'w_in_odd': w_in_odd,
        'w_out_odd': w_out_odd, 'ret_decay_f': ret_decay_f, 'ret_decay_b': ret_decay_b,
        'ret_norm': ret_norm, 'dn_conv': dn_conv, 'dn_a_log_f': dn_a_log_f,
        'dn_a_log_b': dn_a_log_b, 'dn_dt_bias_f': dn_dt_bias_f, 'dn_dt_bias_b': dn_dt_bias_b,
        'dn_norm': dn_norm, 'norm_final': norm_final, 'w_ada_final': w_ada_final,
        'b_ada_final': b_ada_final,
    })
    bp, lp, d = x_prompt.shape
    bs, ls, _ = x_sample.shape
    groups = ((bp, lp), (bs, ls))
    x = jnp.concatenate([x_prompt.reshape(bp * lp, d), x_sample.reshape(bs * ls, d)], axis=0)
    c = jnp.concatenate([c_prompt, c_sample], axis=0)
    y_prompt, y_sample = _trunk(x, c, p, groups)
    return (y_prompt.reshape(bp, lp, d), y_sample.reshape(bs, ls, d))
```

```python
import functools

import jax
import jax.numpy as jnp
from jax import lax
from jax.experimental import pallas as pl
from jax.experimental.pallas import tpu as pltpu

F32 = jnp.float32
BF16 = jnp.bfloat16

D_MODEL = 2048
DEPTH = 4
D_HALF = D_MODEL // 2
POOL_WINDOWS = (2, 4, 8, 16)
POOL_GROUP = D_HALF // len(POOL_WINDOWS)
SGU_CHUNK = 128
SGU_HEAD = 128
SGU_GROUPS = D_HALF // SGU_HEAD
RET_HEADS = 4
RET_D = D_HALF // RET_HEADS
RET_CHUNK = 128
ROPE_BASE = 10000.0
DN_HEADS = 8
DN_D = D_HALF // DN_HEADS
DN_CONV = 4
DN_CHUNK = 64
D_FF = 5632
N_MOD = 9
EPS = 1e-6
P_EVEN = 3 * D_HALF
P_RET = 4 * D_HALF
P_ODD_MAIN = P_RET + 4 * D_HALF
P_ODD = P_ODD_MAIN + 4 * DN_HEADS

LANES = 128
SUBLANES = 8
HALO = 2 * SUBLANES
VMEM_LIMIT = 56 * 1024 * 1024

TM_FFN = 1024
TF_FFN = 512
TM_PROJ = 1024
TN_PROJ = 1024
MOD_ROWS = 16
MOD_UNROLL = 8
TM_OUT = 512
TL_EVEN = 256
TL_RET = 512
TL_DN = 512
TL_PREP = 512
PREP_ROWS = 64
TN_ADA = 1024
SEQ_PAD = 16


def _seq_info(row0, groups):
    seq = start = length = None
    t0 = s0 = 0
    for gi, (nb, ln) in enumerate(groups):
        rel = row0 - t0
        q = rel // ln
        if gi == 0:
            seq, start, length = q, q * ln, ln
        else:
            here = row0 >= t0
            seq = jnp.where(here, s0 + q, seq)
            start = jnp.where(here, t0 + q * ln, start)
            length = jnp.where(here, ln, length)
        t0 += nb * ln
        s0 += nb
    return seq, start, length


def _params(*sem):
    return pltpu.CompilerParams(dimension_semantics=sem, vmem_limit_bytes=VMEM_LIMIT)


def _dot(a, b):
    return jnp.dot(a, b, preferred_element_type=F32)


def _dot_nt(a, b):
    return lax.dot_general(a, b, (((1,), (1,)), ((), ())), preferred_element_type=F32)


def _dot_tn(a, b):
    return lax.dot_general(a, b, (((0,), (0,)), ((), ())), preferred_element_type=F32)


def _split_bf16(a):
    hi = a.astype(BF16)
    lo = (a - hi.astype(F32)).astype(BF16)
    return hi, lo


def _ada_kernel(c_ref, w_ref, b_ref, o_ref):
    c = c_ref[...]
    act = jax.nn.silu(c).astype(BF16)
    o_ref[...] = _dot(act, w_ref[...].astype(BF16)) + b_ref[...]


def _ada(c_pad, w, b):
    ly, d, n = w.shape
    s = c_pad.shape[0]
    tn = min(TN_ADA, n)
    return pl.pallas_call(
        _ada_kernel,
        out_shape=jax.ShapeDtypeStruct((ly, s, n), F32),
        grid=(ly, n // tn),
        in_specs=[
            pl.BlockSpec((s, d), lambda l, j: (0, 0)),
            pl.BlockSpec((None, d, tn), lambda l, j: (l, 0, j)),
            pl.BlockSpec((None, 1, tn), lambda l, j: (l, 0, j)),
        ],
        out_specs=pl.BlockSpec((None, s, tn), lambda l, j: (l, 0, j)),
        compiler_params=_params("parallel", "parallel"),
        name="ada_rows",
    )(c_pad, w, b.reshape(ly, 1, n))


def _modulate_into(h_ref, x_ref, gain_ref, sh_ref, sc_ref, zero_ref=None):
    tm = x_ref.shape[0]
    amp = gain_ref[...] * (1.0 + sc_ref[...])
    shift = sh_ref[...]

    def body(r, carry):
        rows = pl.ds(pl.multiple_of(r * MOD_ROWS, MOD_ROWS), MOD_ROWS)
        x = x_ref[rows, :]
        ms = jnp.mean(x * x, axis=-1, keepdims=True)
        h_ref[rows, :] = ((x * lax.rsqrt(ms + EPS)) * amp + shift).astype(BF16)
        if zero_ref is not None:
            zero_ref[rows, :] = jnp.zeros((MOD_ROWS, zero_ref.shape[1]), zero_ref.dtype)
        return carry

    lax.fori_loop(0, tm // MOD_ROWS, body, 0, unroll=MOD_UNROLL)


def _ffn_kernel(*refs, nf, final, layer, n_tiles):
    x_ref, gain_ref, sh_ref, sc_ref, gt_ref, win_hbm, wout_hbm = refs[:7]
    rest = refs[7:]
    if final:
        fgain_ref, fsh_ref, fsc_ref = rest[:3]
        rest = rest[3:]
    o_ref, h_ref, wg_buf, wu_buf, wo_buf, sem = rest[:6]
    if final:
        inv_ref = rest[6]
    i = pl.program_id(0)
    tf = wg_buf.shape[2]

    def weight_copies(f, slot):
        col = f * tf if isinstance(f, int) else pl.multiple_of(f * tf, tf)
        return (pltpu.make_async_copy(win_hbm.at[layer, :, pl.ds(col, tf)], wg_buf.at[slot], sem.at[0, slot]),
                pltpu.make_async_copy(win_hbm.at[layer, :, pl.ds(nf * tf + col, tf)], wu_buf.at[slot],
                                      sem.at[1, slot]),
                pltpu.make_async_copy(wout_hbm.at[layer, pl.ds(col, tf), :], wo_buf.at[slot], sem.at[2, slot]))

    @pl.when(i == 0)
    def _():
        for cp in weight_copies(0, 0):
            cp.start()

    _modulate_into(h_ref, x_ref, gain_ref, sh_ref, sc_ref, zero_ref=o_ref)

    def ff_block(f, carry):
        step = i * nf + f
        slot = step % 2
        for cp in weight_copies(f, slot):
            cp.wait()

        @pl.when(step + 1 < n_tiles * nf)
        def _():
            for cp in weight_copies(jnp.where(f == nf - 1, 0, f + 1), 1 - slot):
                cp.start()

        h = h_ref[...]
        g = _dot(h, wg_buf[slot])
        u = _dot(h, wu_buf[slot])
        a = (jax.nn.silu(g) * u).astype(BF16)
        o_ref[...] += _dot(a, wo_buf[slot])
        return carry

    lax.fori_loop(0, nf, ff_block, 0)

    half_gate = 0.5 * gt_ref[...]
    if not final:
        o_ref[...] = x_ref[...] + half_gate * o_ref[...]
        return
    amp = fgain_ref[...] * (1.0 + fsc_ref[...])
    shift = fsh_ref[...]
    steps = x_ref.shape[0] // MOD_ROWS

    def residual(r):
        rows = pl.ds(pl.multiple_of(r * MOD_ROWS, MOD_ROWS), MOD_ROWS)
        return rows, x_ref[rows, :] + half_gate * o_ref[rows, :]

    def stats(r, carry):
        rows, y = residual(r)
        inv_ref[rows, :] = lax.rsqrt(jnp.mean(y * y, axis=-1, keepdims=True) + EPS)
        return carry

    def apply(r, carry):
        rows, y = residual(r)
        o_ref[rows, :] = (y * inv_ref[rows, :]) * amp + shift
        return carry

    lax.fori_loop(0, steps, stats, 0, unroll=MOD_UNROLL)
    lax.fori_loop(0, steps, apply, 0, unroll=MOD_UNROLL)


def _row_spec(groups, tm, d, row_off=0):
    return pl.BlockSpec((None, 1, d), lambda i, *_: (_seq_info(i * tm + row_off, groups)[0], 0, 0))


def _ffn(x, gain, shift, scale, gate, w_in, w_out, layer, groups, *, span=None, final=None):
    t, d = x.shape
    row_off, n_rows = span if span is not None else (0, t)
    ff = w_out.shape[1]
    tm = min(TM_FFN, groups[0][1])
    tf = min(TF_FFN, ff)
    nf = ff // tf
    tile_off = row_off // tm
    row = _row_spec(groups, tm, d, row_off)
    n_tiles = n_rows // tm
    hbm = pl.BlockSpec(memory_space=pl.ANY)
    in_specs = [
        pl.BlockSpec((tm, d), lambda i: (i + tile_off, 0)),
        pl.BlockSpec((1, d), lambda i: (0, 0)),
        row, row, row,
        hbm, hbm,
    ]
    args = [x, gain, shift, scale, gate, w_in, w_out]
    if final is not None:
        in_specs += [pl.BlockSpec((1, d), lambda i: (0, 0)), row, row]
        args += list(final)
    scratch = [pltpu.VMEM((tm, d), BF16),
               pltpu.VMEM((2, d, tf), BF16), pltpu.VMEM((2, d, tf), BF16), pltpu.VMEM((2, tf, d), BF16),
               pltpu.SemaphoreType.DMA((3, 2))]
    if final is not None:
        scratch.append(pltpu.VMEM((tm, 1), F32))
    return pl.pallas_call(
        functools.partial(_ffn_kernel, nf=nf, final=final is not None, layer=layer, n_tiles=n_tiles),
        out_shape=jax.ShapeDtypeStruct((n_rows, d), F32),
        grid=(n_tiles,),
        in_specs=in_specs,
        out_specs=pl.BlockSpec((tm, d), lambda i: (i, 0)),
        scratch_shapes=scratch,
        compiler_params=_params("arbitrary"),
        name="ffn",
    )(*args)


def _proj_kernel(*refs, odd):
    if odd:
        x_ref, gain_ref, sh_ref, sc_ref, w_ref, wn_ref, cos_ref, sin_ref, o_ref, on_ref, h_ref = refs
    else:
        x_ref, gain_ref, sh_ref, sc_ref, w_ref, o_ref, h_ref = refs
    j = pl.program_id(1)

    @pl.when(j == 0)
    def _():
        _modulate_into(h_ref, x_ref, gain_ref, sh_ref, sc_ref)
        if odd:
            on_ref[...] = _dot(h_ref[...], wn_ref[...])

    y = _dot(h_ref[...], w_ref[...])
    if not odd:
        o_ref[...] = y.astype(BF16)
        return

    tn = o_ref.shape[1]
    rope_tiles = 2 * D_HALF // tn
    half = RET_D // 2
    rotate = j < rope_tiles
    cos = jnp.where(rotate, cos_ref[...], 1.0)
    sin = jnp.where(rotate, sin_ref[...], 0.0)
    k_scale = jnp.where(rotate & (j >= rope_tiles // 2), RET_D ** -0.5, 1.0).astype(F32)
    for hd in range(tn // RET_D):
        x1 = y[:, hd * RET_D:hd * RET_D + half]
        x2 = y[:, hd * RET_D + half:(hd + 1) * RET_D]
        o_ref[:, hd * RET_D:hd * RET_D + half] = ((x1 * cos - x2 * sin) * k_scale).astype(BF16)
        o_ref[:, hd * RET_D + half:(hd + 1) * RET_D] = ((x1 * sin + x2 * cos) * k_scale).astype(BF16)


def _proj(x, gain, shift, scale, w, idx, n, groups, odd_extras=None):
    t, d = x.shape
    tm = min(TM_PROJ, groups[0][1])
    tn = TN_PROJ
    row = _row_spec(groups, tm, d)
    odd = odd_extras is not None
    in_specs = [
        pl.BlockSpec((tm, d), lambda i, j: (i, 0)),
        pl.BlockSpec((1, d), lambda i, j: (0, 0)),
        row, row,
        pl.BlockSpec((None, d, tn), lambda i, j: (idx, 0, j)),
    ]
    out_shape = jax.ShapeDtypeStruct((t, n), BF16)
    out_specs = pl.BlockSpec((tm, tn), lambda i, j: (i, j))
    args = (x, gain, shift, scale, w)
    if odd:
        half = RET_D // 2

        def pos_block(i, j):
            row0 = i * tm
            return ((row0 - _seq_info(row0, groups)[1]) // tm, 0)

        tab = pl.BlockSpec((tm, half), pos_block)
        in_specs += [pl.BlockSpec((None, d, LANES), lambda i, j: (idx, 0, 0)), tab, tab]
        out_shape = (out_shape, jax.ShapeDtypeStruct((t, LANES), F32))
        out_specs = (out_specs, pl.BlockSpec((tm, LANES), lambda i, j: (i, 0)))
        args = args + tuple(odd_extras)
    return pl.pallas_call(
        functools.partial(_proj_kernel, odd=odd),
        out_shape=out_shape,
        grid=(t // tm, n // tn),
        in_specs=in_specs,
        out_specs=out_specs,
        scratch_shapes=[pltpu.VMEM((tm, d), BF16)],
        compiler_params=_params("parallel", "arbitrary"),
        name="mix_proj",
    )(*args)


def _outproj_kernel(x_ref, ya_ref, yb_ref, gt_ref, wa_ref, wb_ref, o_ref):
    y = _dot(ya_ref[...], wa_ref[...]) + _dot(yb_ref[...], wb_ref[...])
    o_ref[...] = x_ref[...] + gt_ref[...] * y


def _outproj(x, ya, yb, ca, cb, gate, w, idx, groups):
    t, d = x.shape
    dh = d // 2
    tm = min(TM_OUT, groups[0][1])
    row = _row_spec(groups, tm, d)
    return pl.pallas_call(
        _outproj_kernel,
        out_shape=jax.ShapeDtypeStruct((t, d), F32),
        grid=(t // tm, 1),
        in_specs=[
            pl.BlockSpec((tm, d), lambda i, j: (i, 0)),
            pl.BlockSpec((tm, dh), lambda i, j: (i, ca)),
            pl.BlockSpec((tm, dh), lambda i, j: (i, cb)),
            row,
            pl.BlockSpec((None, dh, d), lambda i, j: (idx, 0, 0)),
            pl.BlockSpec((None, dh, d), lambda i, j: (idx, 1, 0)),
        ],
        out_specs=pl.BlockSpec((tm, d), lambda i, j: (i, 0)),
        compiler_params=_params("parallel", "arbitrary"),
        name="mix_out",
    )(x, ya, yb, gate, w, w)


def _halo_specs(tl, width, nrows, col_of):
    per = tl // HALO
    last = nrows // HALO - 1
    prev = pl.BlockSpec((HALO, width), lambda i, *r: (jnp.maximum(i * per - 1, 0), col_of(i, *r)))
    nxt = pl.BlockSpec((HALO, width), lambda i, *r: (jnp.minimum((i + 1) * per, last), col_of(i, *r)))
    return prev, nxt


def _fill_ext(ext_ref, x_ref, prev_ref, next_ref, first, last, tl):
    dt = ext_ref.dtype
    ext_ref[HALO:HALO + tl, :] = x_ref[...].astype(dt)
    ext_ref[0:HALO, :] = jnp.where(first, jnp.zeros_like(prev_ref), prev_ref[...]).astype(dt)
    ext_ref[HALO + tl:2 * HALO + tl, :] = jnp.where(last, jnp.zeros_like(next_ref), next_ref[...]).astype(dt)


def _row_window(tl, lo, hi):
    r = lax.broadcasted_iota(jnp.int32, (tl, tl + 2 * HALO), 0) + HALO
    col = lax.broadcasted_iota(jnp.int32, (tl, tl + 2 * HALO), 1)
    return ((col >= r + lo) & (col < r + hi)).astype(BF16)


def _even_kernel(xa_ref, prev_ref, next_ref, u_ref, v_ref, pw_ref, ps_ref, ng_ref, sw_ref, sb_ref,
                 o_ref, ext_ref, vn_ref, win_ref, *, tl, groups):
    @pl.when(pl.program_id(0) == 0)
    def _():
        for gi, w in enumerate(POOL_WINDOWS):
            win_ref[gi] = _row_window(tl, -(w // 2), w - w // 2)

    row0 = pl.program_id(0) * tl
    _, sstart, slen = _seq_info(row0, groups)
    pos0 = row0 - sstart
    _fill_ext(ext_ref, xa_ref, prev_ref, next_ref, pos0 == 0, pos0 + tl == slen, tl)

    t = pos0 + lax.broadcasted_iota(jnp.int32, (tl, 1), 0)
    for gi, w in enumerate(POOL_WINDOWS):
        c0 = gi * POOL_GROUP
        cols = slice(c0, c0 + POOL_GROUP)
        s = _dot(win_ref[gi], ext_ref[:, cols])
        lo = jnp.clip(t - w // 2, 0, slen)
        hi = jnp.clip(t + (w - w // 2), 0, slen)
        cnt = (hi - lo).astype(F32)
        pooled = (s / cnt - xa_ref[:, cols].astype(F32)).astype(BF16)
        ya = _dot(pooled, pw_ref[gi]) * ps_ref[:, cols]
        o_ref[:, cols] = ya.astype(BF16)

    v = jax.nn.gelu(v_ref[...].astype(F32))
    vms = jnp.mean(v * v, axis=-1, keepdims=True)
    vn_ref[...] = ((v * lax.rsqrt(vms + EPS)) * ng_ref[...]).astype(BF16)
    for n in range(tl // SGU_CHUNK):
        rows = slice(n * SGU_CHUNK, (n + 1) * SGU_CHUNK)
        for g in range(SGU_GROUPS):
            cols = slice(g * SGU_HEAD, (g + 1) * SGU_HEAD)
            mixed = _dot(sw_ref[g], vn_ref[rows, cols]) + sb_ref[g]
            u = jax.nn.gelu(u_ref[rows, cols].astype(F32))
            o_ref[rows, D_HALF + g * SGU_HEAD:D_HALF + (g + 1) * SGU_HEAD] = (u * mixed).astype(BF16)


def _even_mix(proj, pool_w, pool_scale, sgu_norm, sgu_w, sgu_b, groups):
    t = proj.shape[0]
    tl = min(TL_EVEN, groups[0][1])
    prev, nxt = _halo_specs(tl, D_HALF, t, lambda i: 0)
    const2 = lambda i: (0, 0)
    const3 = lambda i: (0, 0, 0)
    return pl.pallas_call(
        functools.partial(_even_kernel, tl=tl, groups=groups),
        out_shape=jax.ShapeDtypeStruct((t, 2 * D_HALF), BF16),
        grid=(t // tl,),
        in_specs=[
            pl.BlockSpec((tl, D_HALF), lambda i: (i, 0)),
            prev, nxt,
            pl.BlockSpec((tl, D_HALF), lambda i: (i, 1)),
            pl.BlockSpec((tl, D_HALF), lambda i: (i, 2)),
            pl.BlockSpec(pool_w.shape, const3),
            pl.BlockSpec((1, D_HALF), const2),
            pl.BlockSpec((1, D_HALF), const2),
            pl.BlockSpec(sgu_w.shape, const3),
            pl.BlockSpec(sgu_b.shape, const3),
        ],
        out_specs=pl.BlockSpec((tl, 2 * D_HALF), lambda i: (i, 0)),
        scratch_shapes=[pltpu.VMEM((tl + 2 * HALO, D_HALF), BF16), pltpu.VMEM((tl, D_HALF), BF16),
                        pltpu.VMEM((len(POOL_WINDOWS), tl, tl + 2 * HALO), BF16)],
        compiler_params=_params("arbitrary"),
        name="even_mix",
    )(proj, proj, proj, proj, proj, pool_w, pool_scale, sgu_norm, sgu_w, sgu_b)


def _rope_kernel(inv_ref, cos_ref, sin_ref, *, tl):
    pos = (pl.program_id(0) * tl + lax.broadcasted_iota(jnp.int32, (tl, 1), 0)).astype(F32)
    ang = pos * inv_ref[...]
    cos_ref[...] = jnp.cos(ang)
    sin_ref[...] = jnp.sin(ang)


def _rope_tables(max_len, tl):
    half = RET_D // 2
    inv = (1.0 / (ROPE_BASE ** jnp.linspace(0.0, 1.0, half, dtype=F32))).reshape(1, half)
    shp = jax.ShapeDtypeStruct((max_len, half), F32)
    return pl.pallas_call(
        functools.partial(_rope_kernel, tl=tl),
        out_shape=(shp, shp),
        grid=(max_len // tl,),
        in_specs=[pl.BlockSpec((1, half), lambda i: (0, 0))],
        out_specs=(pl.BlockSpec((tl, half), lambda i: (i, 0)), pl.BlockSpec((tl, half), lambda i: (i, 0))),
        compiler_params=_params("parallel"),
        name="rope_table",
    )(inv)


def _ret_kernel(*refs, tl, nt, groups, reverse):
    if reverse:
        q_ref, k_ref, v_ref, dec_ref, o_ref, s_ref = refs
    else:
        q_ref, k_ref, v_ref, dec_ref, decb_ref, g_ref, ob_ref, ng_ref, o_ref, s_ref = refs
    j = pl.program_id(0)
    it = nt - 1 - j if reverse else j
    row0 = it * tl
    _, sstart, slen = _seq_info(row0, groups)
    pos0 = row0 - sstart
    reset = (pos0 + tl == slen) if reverse else (pos0 == 0)

    @pl.when(reset)
    def _():
        s_ref[...] = jnp.zeros_like(s_ref)

    c = RET_CHUNK
    heads = range(RET_HEADS)
    idx = lax.broadcasted_iota(jnp.int32, (c, 1), 0).astype(F32)
    lg = [jnp.log1p(-jnp.exp2(-dec_ref[h])) for h in heads]
    if reverse:
        q_dec = [jnp.exp(x * (c - idx)) for x in lg]
        k_dec = [jnp.exp(x * idx) for x in lg]
    else:
        q_dec = [jnp.exp(x * (idx + 1.0)) for x in lg]
        k_dec = [jnp.exp(x * (c - 1.0 - idx)) for x in lg]
        lgb = [jnp.log1p(-jnp.exp2(-decb_ref[h])) for h in heads]
        ri = lax.broadcasted_iota(jnp.int32, (c, c), 0)
        ci = lax.broadcasted_iota(jnp.int32, (c, c), 1)
        rel = (ri - ci).astype(F32)
        dmat = [jnp.where(rel >= 0, jnp.exp(x * jnp.maximum(rel, 0.0)), 0.0)
                + jnp.where(rel <= 0, jnp.exp(y * jnp.maximum(-rel, 0.0)), 0.0) for x, y in zip(lg, lgb)]
    chunk_dec = [jnp.exp(x * float(c)) for x in lg]

    nc = tl // c
    order = list(range(nc - 1, -1, -1) if reverse else range(nc))
    units = [(ch, h) for ch in order for h in heads]

    def rows(ch):
        return slice(ch * c, (ch + 1) * c)

    def cols(h):
        return slice(h * RET_D, (h + 1) * RET_D)

    q = [q_ref[rows(ch), cols(h)] for ch, h in units]
    v = [v_ref[rows(ch), cols(h)] for ch, h in units]
    kv = [_dot_tn(k_ref[rows(ch), cols(h)], (x.astype(F32) * k_dec[h]).astype(BF16))
          for (ch, h), x in zip(units, v)]
    if not reverse:
        scores = [(_dot_nt(x, k_ref[rows(ch), cols(h)]) * dmat[h]).astype(BF16) for (ch, h), x in zip(units, q)]
        intra = [_dot(x, y) for x, y in zip(scores, v)]

    for ci_, ch in enumerate(order):
        s = [s_ref[h] for h in heads]
        inter = [_dot(q[ci_ * RET_HEADS + h], s[h].astype(BF16)) * q_dec[h] for h in heads]
        for h in heads:
            s_ref[h] = s[h] * chunk_dec[h] + kv[ci_ * RET_HEADS + h]
        for h in heads:
            if reverse:
                o_ref[rows(ch), cols(h)] = inter[h]
            else:
                o = intra[ci_ * RET_HEADS + h] + inter[h] + ob_ref[rows(ch), cols(h)]
                mu = jnp.mean(o, axis=-1, keepdims=True)
                var = jnp.mean(jnp.square(o - mu), axis=-1, keepdims=True)
                on = ((o - mu) * lax.rsqrt(var + EPS)) * ng_ref[:, cols(h)]
                gate = jax.nn.silu(g_ref[rows(ch), cols(h)].astype(F32))
                o_ref[rows(ch), cols(h)] = (gate * on).astype(BF16)


def _retention(proj, decay_f, decay_b, norm_g, groups):
    t = proj.shape[0]
    tl = min(TL_RET, groups[0][1])
    nt = t // tl
    dec = pl.BlockSpec((RET_HEADS, 1, 1), lambda j: (0, 0, 0))
    state = pltpu.VMEM((RET_HEADS, RET_D, RET_D), F32)

    def section(reverse):
        tile = (lambda j: nt - 1 - j) if reverse else (lambda j: j)
        return lambda s: pl.BlockSpec((tl, D_HALF), lambda j: (tile(j), s))

    sec = section(True)
    ob = pl.pallas_call(
        functools.partial(_ret_kernel, tl=tl, nt=nt, groups=groups, reverse=True),
        out_shape=jax.ShapeDtypeStruct((t, D_HALF), F32),
        grid=(nt,),
        in_specs=[sec(0), sec(1), sec(2), dec],
        out_specs=sec(0),
        scratch_shapes=[state],
        compiler_params=_params("arbitrary"),
        name="ret_bwd",
    )(proj, proj, proj, decay_b.reshape(RET_HEADS, 1, 1))

    sec = section(False)
    return pl.pallas_call(
        functools.partial(_ret_kernel, tl=tl, nt=nt, groups=groups, reverse=False),
        out_shape=jax.ShapeDtypeStruct((t, D_HALF), BF16),
        grid=(nt,),
        in_specs=[sec(0), sec(1), sec(2), dec, dec, sec(3), sec(0),
                  pl.BlockSpec((1, D_HALF), lambda j: (0, 0))],
        out_specs=sec(0),
        scratch_shapes=[state],
        compiler_params=_params("arbitrary"),
        name="ret_fwd",
    )(proj, proj, proj, decay_f.reshape(RET_HEADS, 1, 1), decay_b.reshape(RET_HEADS, 1, 1),
      proj, ob, norm_g)


def _dnprep_kernel(x_ref, prev_ref, next_ref, w_ref, o_ref, ext_ref, *, tl, groups):
    row0 = pl.program_id(0) * tl
    part = pl.program_id(1)
    _, sstart, slen = _seq_info(row0, groups)
    pos0 = row0 - sstart
    _fill_ext(ext_ref, x_ref, prev_ref, next_ref, pos0 == 0, pos0 + tl == slen, tl)
    left = DN_CONV // 2
    q_scale = jnp.where(part == 0, DN_D ** -0.5, 1.0).astype(F32)
    rb = PREP_ROWS
    taps = [tap for tap in range(DN_CONV) if tap != left]
    r = lax.broadcasted_iota(jnp.int32, (rb, rb + 2 * HALO), 0) + HALO
    col = lax.broadcasted_iota(jnp.int32, (rb, rb + 2 * HALO), 1)
    select = jnp.concatenate([(col == r + (tap - left)).astype(BF16) for tap in taps], axis=0)
    for blk in range(tl // rb):
        rows = slice(blk * rb, (blk + 1) * rb)
        for h in range(DN_HEADS):
            cols = slice(h * DN_D, (h + 1) * DN_D)
            shifted = _dot(select, ext_ref[blk * rb:(blk + 1) * rb + 2 * HALO, cols])
            conv = x_ref[rows, cols].astype(F32) * w_ref[left:left + 1, cols]
            for n, tap in enumerate(taps):
                conv = conv + shifted[n * rb:(n + 1) * rb] * w_ref[tap:tap + 1, cols]
            y = jax.nn.silu(conv)
            inv_norm = lax.rsqrt(jnp.sum(y * y, axis=-1, keepdims=True) + EPS)
            o_ref[rows, cols] = (y * jnp.where(part < 2, inv_norm * q_scale, 1.0)).astype(BF16)


def _dn_prep(proj, conv_w, groups):
    t = proj.shape[0]
    tl = min(TL_PREP, groups[0][1])
    base = P_RET // D_HALF
    prev, nxt = _halo_specs(tl, D_HALF, t, lambda i, part: base + part)
    return pl.pallas_call(
        functools.partial(_dnprep_kernel, tl=tl, groups=groups),
        out_shape=jax.ShapeDtypeStruct((t, 3 * D_HALF), BF16),
        grid=(t // tl, 3),
        in_specs=[pl.BlockSpec((tl, D_HALF), lambda i, part: (i, base + part)), prev, nxt,
                  pl.BlockSpec((DN_CONV, D_HALF), lambda i, part: (0, part))],
        out_specs=pl.BlockSpec((tl, D_HALF), lambda i, part: (i, part)),
        scratch_shapes=[pltpu.VMEM((tl + 2 * HALO, D_HALF), BF16)],
        compiler_params=_params("parallel", "parallel"),
        name="dn_prep",
    )(proj, proj, proj, conv_w)


def _dot_hi_each(lhs_parts, rhs_parts):
    m = lhs_parts[0][0].shape[0]
    n = rhs_parts[0][0].shape[1]
    quads = [_dot(jnp.concatenate(a, axis=0), jnp.concatenate(b, axis=1)) for a, b in zip(lhs_parts, rhs_parts)]
    return [(x[:m, :n] + x[m:, :n]) + (x[:m, n:] + x[m:, n:]) for x in quads]


def _block_diag(y, left):
    zero = jnp.zeros_like(y)
    return jnp.concatenate([jnp.where(left, y, zero), jnp.where(left, zero, y)], axis=0)


def _pair_products(lhs, rhs, left):
    ls = [_split_bf16(x) for x in lhs]
    rs = [tuple(_block_diag(part, left) for part in _split_bf16(y)) for y in rhs]
    return _dot_hi_each(ls, rs)


def _unit_triangular_inverses(mats, eye, ri, ci, left):
    size = SUBLANES
    same = (ri // size) == (ci // size)
    ps = [jnp.where(same, a, 0.0) for a in mats]
    invs = [eye - d for d in ps]
    n = 2
    while n < size:
        ps = _pair_products(ps, ps, left)
        invs = [inv + x for inv, x in zip(invs, _pair_products(invs, ps, left))]
        n *= 2
    while size < DN_CHUNK:
        size *= 2
        merged = (ri // size) == (ci // size)
        es = [jnp.where(merged & ~same, a, 0.0) for a in mats]
        invs = [inv - x for inv, x in zip(invs, _pair_products(_pair_products(invs, es, left), invs, left))]
        same = merged
    return invs


def _dn_kernel(*refs, tl, nt, groups, reverse):
    if reverse:
        q_ref, k_ref, v_ref, gates_ref, alog_ref, dtb_ref, o_ref, s_ref = refs
    else:
        (q_ref, k_ref, v_ref, gates_ref, alog_ref, dtb_ref, z_ref, ob_ref, ng_ref,
         o_ref, s_ref) = refs
    j = pl.program_id(0)
    it = nt - 1 - j if reverse else j
    row0 = it * tl
    _, sstart, slen = _seq_info(row0, groups)
    pos0 = row0 - sstart
    reset = (pos0 + tl == slen) if reverse else (pos0 == 0)

    @pl.when(reset)
    def _():
        s_ref[...] = jnp.zeros_like(s_ref)

    gates = gates_ref[...]
    beta_all = jax.nn.sigmoid(gates)
    la_all = -jnp.exp(alog_ref[...]) * jax.nn.softplus(gates + dtb_ref[...])
    cb0 = DN_HEADS if reverse else 0
    ca0 = cb0 + 2 * DN_HEADS

    c = DN_CHUNK
    left = lax.broadcasted_iota(jnp.int32, (1, 2 * c), 1) < c
    ri = lax.broadcasted_iota(jnp.int32, (c, 2 * c), 0)
    ci = jnp.bitwise_and(lax.broadcasted_iota(jnp.int32, (c, 2 * c), 1), c - 1)
    eye = (ri == ci).astype(F32)
    incl = (ri <= ci) if reverse else (ri >= ci)
    strict = (ri < ci) if reverse else (ri > ci)
    incl_t = (ri >= ci) if reverse else (ri <= ci)
    last = 0 if reverse else c - 1

    nc = tl // c
    order = list(range(nc - 1, -1, -1) if reverse else range(nc))
    heads = range(DN_HEADS)
    half_heads = range(DN_HEADS // 2)
    units = [(ch, h) for ch in order for h in heads]
    pairs = range(len(units) // 2)

    def rows(ch):
        return slice(ch * c, (ch + 1) * c)

    def cols(h):
        return slice(h * DN_D, (h + 1) * DN_D)

    def paired(xs):
        return [jnp.where(left, xs[2 * p], xs[2 * p + 1]) for p in pairs]

    q = [q_ref[rows(ch), cols(h)] for ch, h in units]
    k = [k_ref[rows(ch), cols(h)] for ch, h in units]
    beta = [beta_all[rows(ch), cb0 + h:cb0 + h + 1] for ch, h in units]
    la = [la_all[rows(ch), ca0 + h:ca0 + h + 1] for ch, h in units]
    la2 = paired(la)
    la_row = [jnp.sum(eye * x, axis=0, keepdims=True) for x in la2]
    g_row = [jnp.sum(jnp.where(incl_t, x, 0.0), axis=0, keepdims=True) for x in la2]
    windowed = [jnp.where(incl, x, 0.0) for x in la_row]
    g_col = [jnp.sum(jnp.where(left == (u % 2 == 0), windowed[u // 2], 0.0), axis=1, keepdims=True)
             for u in range(len(units))]
    gam = [jnp.where(incl, jnp.exp(jnp.where(incl, gc - gr, 0.0)), 0.0) for gc, gr in zip(paired(g_col), g_row)]
    eg = [jnp.exp(gc) for gc in g_col]
    g_last = [gc[last:last + 1, :] for gc in g_col]
    kb = [x.astype(F32) * b for x, b in zip(k, beta)]
    kq = [_dot_nt(jnp.concatenate([kb[2 * p].astype(BF16), q[2 * p], kb[2 * p + 1].astype(BF16), q[2 * p + 1]],
                                  axis=0),
                  jnp.concatenate([k[2 * p], k[2 * p + 1]], axis=0)) for p in pairs]
    a = [jnp.where(strict, jnp.where(left, x[:c], x[2 * c:3 * c]) * gm, 0.0) for x, gm in zip(kq, gam)]
    attn = [_block_diag((jnp.where(left, x[c:2 * c], x[3 * c:]) * gm).astype(BF16), left)
            for x, gm in zip(kq, gam)]
    tinv = _unit_triangular_inverses(a, eye, ri, ci, left)
    rhs = [jnp.concatenate([v_ref[rows(ch), cols(h)].astype(F32) * b, x * e], axis=1)
           for (ch, h), b, x, e in zip(units, beta, kb, eg)]
    uw2 = _dot_hi_each([tuple(_block_diag(part, left) for part in _split_bf16(t)) for t in tinv],
                       [_split_bf16(jnp.concatenate([rhs[2 * p], rhs[2 * p + 1]], axis=0)) for p in pairs])
    uw = [uw2[u // 2][(u % 2) * c:(u % 2 + 1) * c] for u in range(len(units))]
    tail = [jnp.exp(gl - gc) for gl, gc in zip(g_last, g_col)]
    dec = [jnp.exp(gl) for gl in g_last]

    for ci_, ch in enumerate(order):
        idx = [ci_ * DN_HEADS + h for h in heads]
        s = [s_ref[h] for h in heads]
        sb = [x.astype(BF16) for x in s]
        ws = [_dot(jnp.concatenate([uw[i][:, DN_D:].astype(BF16), q[i]], axis=0), sb[h])
              for h, i in zip(heads, idx)]
        v_new = [uw[i][:, :DN_D] - x[:c] for i, x in zip(idx, ws)]
        av = [_dot(attn[idx[2 * x] // 2],
                   jnp.concatenate([v_new[2 * x].astype(BF16), v_new[2 * x + 1].astype(BF16)], axis=0))
              for x in half_heads]
        o = [ws[h][c:] * eg[idx[h]] + av[h // 2][(h % 2) * c:(h % 2 + 1) * c] for h in heads]
        for h, i in zip(heads, idx):
            s_ref[h] = s[h] * dec[i] + _dot_tn(k[i], (v_new[h] * tail[i]).astype(BF16))
        for h in heads:
            if reverse:
                o_ref[rows(ch), cols(h)] = o[h]
            else:
                oo = o[h] + ob_ref[rows(ch), cols(h)]
                on = (oo * lax.rsqrt(jnp.mean(oo * oo, axis=-1, keepdims=True) + EPS)) * ng_ref[...]
                z = z_ref[rows(ch), cols(h)].astype(F32)
                o_ref[rows(ch), cols(h)] = (on * jax.nn.silu(z)).astype(BF16)


def _deltanet(proj, gate_cols, qkv, alog_row, dtb_row, norm_g, groups):
    t = proj.shape[0]
    tl = min(TL_DN, groups[0][1])
    nt = t // tl
    z_blk = (P_RET + 3 * D_HALF) // D_HALF
    row = pl.BlockSpec((1, LANES), lambda j: (0, 0))
    state = pltpu.VMEM((DN_HEADS, DN_D, DN_D), F32)

    def common(reverse):
        tile = (lambda j: nt - 1 - j) if reverse else (lambda j: j)
        sec = lambda s: pl.BlockSpec((tl, D_HALF), lambda j: (tile(j), s))
        gates = pl.BlockSpec((tl, LANES), lambda j: (tile(j), 0))
        return sec, gates

    sec, gates = common(True)
    ob = pl.pallas_call(
        functools.partial(_dn_kernel, tl=tl, nt=nt, groups=groups, reverse=True),
        out_shape=jax.ShapeDtypeStruct((t, D_HALF), F32),
        grid=(nt,),
        in_specs=[sec(0), sec(1), sec(2), gates, row, row],
        out_specs=sec(0),
        scratch_shapes=[state],
        compiler_params=_params("arbitrary"),
        name="dn_bwd",
    )(qkv, qkv, qkv, gate_cols, alog_row, dtb_row)

    sec, gates = common(False)
    return pl.pallas_call(
        functools.partial(_dn_kernel, tl=tl, nt=nt, groups=groups, reverse=False),
        out_shape=jax.ShapeDtypeStruct((t, D_HALF), BF16),
        grid=(nt,),
        in_specs=[sec(0), sec(1), sec(2), gates, row, row, sec(z_blk), sec(0), row],
        out_specs=sec(0),
        scratch_shapes=[state],
        compiler_params=_params("arbitrary"),
        name="dn_fwd",
    )(qkv, qkv, qkv, gate_cols, alog_row, dtb_row, proj, ob, norm_g)


def _gate_row(f_vals, b_vals):
    row = jnp.zeros((LANES,), F32)
    row = row.at[2 * DN_HEADS:3 * DN_HEADS].set(f_vals.astype(F32))
    row = row.at[3 * DN_HEADS:4 * DN_HEADS].set(b_vals.astype(F32))
    return row.reshape(1, LANES)


def _trunk(x, c, p, groups):
    d = x.shape[1]
    spans = []
    for nb, ln in groups:
        spans.append((sum(r for _, r in spans), nb * ln))
    n_seq = c.shape[0]
    c_pad = jnp.zeros((SEQ_PAD, d), F32).at[:n_seq].set(c)
    mods = _ada(c_pad, p['w_ada'], p['b_ada'])
    mods = mods.reshape(DEPTH, SEQ_PAD, N_MOD, 1, d).transpose(0, 2, 1, 3, 4)
    fin = _ada(c_pad, p['w_ada_final'][None], p['b_ada_final'][None])
    fin = fin.reshape(SEQ_PAD, 2, 1, d).transpose(1, 0, 2, 3)

    max_len = max(ln for _, ln in groups)
    cos, sin = _rope_tables(max_len, min(TL_RET, groups[0][1]))

    row = lambda a: a.reshape(1, -1)
    for layer in range(DEPTH):
        sh1, sc1, g1, sh2, sc2, g2, sh3, sc3, g3 = [mods[layer, jm] for jm in range(N_MOD)]
        ffn1 = functools.partial(_ffn, gain=row(p['norm_ffn1'][layer]), shift=sh1, scale=sc1, gate=g1,
                                 w_in=p['w_ffn1_in'], w_out=p['w_ffn1_out'], layer=layer, groups=groups)
        x = ffn1(x)
        idx = layer // 2
        gain = row(p['norm_mix'][layer])
        if layer % 2 == 0:
            proj = _proj(x, gain, sh2, sc2, p['w_in_even'], idx, P_EVEN, groups)
            y = _even_mix(proj, p['pool_w'][idx], row(p['pool_scale'][idx]), row(p['sgu_norm'][idx]),
                          p['sgu_w'][idx], p['sgu_b'][idx][..., None], groups)
            x = _outproj(x, y, y, 0, 1, g2, p['w_out_even'], idx, groups)
        else:
            proj, gate_cols = _proj(x, gain, sh2, sc2, p['w_in_odd'], idx, P_ODD_MAIN, groups,
                                    odd_extras=(p['w_in_odd_gates'], cos, sin))
            yc = _retention(proj, p['ret_decay_f'][idx], p['ret_decay_b'][idx],
                            row(p['ret_norm'][idx]), groups)
            qkv = _dn_prep(proj, p['dn_conv'][idx], groups)
            yd = _deltanet(proj, gate_cols, qkv,
                           _gate_row(p['dn_a_log_f'][idx], p['dn_a_log_b'][idx]),
                           _gate_row(p['dn_dt_bias_f'][idx], p['dn_dt_bias_b'][idx]),
                           row(p['dn_norm'][idx]), groups)
            x = _outproj(x, yc, yd, 0, 0, g2, p['w_out_odd'], idx, groups)
        ffn2 = functools.partial(_ffn, gain=row(p['norm_ffn2'][layer]), shift=sh3, scale=sc3, gate=g3,
                                 w_in=p['w_ffn2_in'], w_out=p['w_ffn2_out'], layer=layer, groups=groups)
        if layer < DEPTH - 1:
            x = ffn2(x)
    closing = (row(p['norm_final']), fin[0], fin[1])
    return tuple(ffn2(x, span=span, final=closing) for span in spans)


def _prepare(p):
    q = dict(p)
    for name in ('w_ffn1_in', 'w_ffn1_out', 'w_ffn2_in', 'w_ffn2_out', 'w_in_even', 'w_out_even',
                 'pool_w', 'sgu_w', 'w_in_odd', 'w_out_odd'):
        q[name] = p[name].astype(BF16)
    gates = q['w_in_odd'][:, :, P_ODD_MAIN:]
    q['w_in_odd_gates'] = jnp.pad(gates, ((0, 0), (0, 0), (0, LANES - gates.shape[-1])))
    return q


def kernel(x_prompt, x_sample, c_prompt, c_sample, w_ada, b_ada, norm_ffn1, w_ffn1_in, w_ffn1_out, norm_mix, norm_ffn2, w_ffn2_in, w_ffn2_out, w_in_even, w_out_even, pool_w, pool_scale, sgu_norm, sgu_w, sgu_b, w_in_odd, w_out_odd, ret_decay_f, ret_decay_b, ret_norm, dn_conv, dn_a_log_f, dn_a_log_b, dn_dt_bias_f, dn_dt_bias_b, dn_norm, norm_final, w_ada_final, b_ada_final):
    p = _prepare({
        'w_ada': w_ada, 'b_ada': b_ada, 'norm_ffn1': norm_ffn1, 'w_ffn1_in': w_ffn1_in,
        'w_ffn1_out': w_ffn1_out, 'norm_mix': norm_mix, 'norm_ffn2': norm_ffn2,
        'w_ffn2_in': w_ffn2_in, 'w_ffn2_out': w_ffn2_out, 'w_in_even': w_in_even,
        'w_out_even': w_out_even, 'pool_w': pool_w, 'pool_scale': pool_scale,
        'sgu_norm': sgu_norm, 'sgu_w': sgu_w, 'sgu_b': sgu_b, 'w_in_odd': w_in_odd,
        'w_out_odd': w_out_odd, 'ret_decay_f': ret_decay_f, 'ret_decay_b': ret_decay_b,
        'ret_norm': ret_norm, 'dn_conv': dn_conv, 'dn_a_log_f': dn_a_log_f,
        'dn_a_log_b': dn_a_log_b, 'dn_dt_bias_f': dn_dt_bias_f, 'dn_dt_bias_b': dn_dt_bias_b,
        'dn_norm': dn_norm, 'norm_final': norm_final, 'w_ada_final': w_ada_final,
        'b_ada_final': b_ada_final,
    })
    bp, lp, d = x_prompt.shape
    bs, ls, _ = x_sample.shape
    groups = ((bp, lp), (bs, ls))
    x = jnp.concatenate([x_prompt.reshape(bp * lp, d), x_sample.reshape(bs * ls, d)], axis=0)
    c = jnp.concatenate([c_prompt, c_sample], axis=0)
    y_prompt, y_sample = _trunk(x, c, p, groups)
    return (y_prompt.reshape(bp, lp, d), y_sample.reshape(bs, ls, d))
```
